```python
import jax, jax.numpy as jnp
from jax import lax
import numpy as np

D_MODEL = 1024
BATCH = 8
SEQ = 2048
DEPTH = 1
DEC_BATCH = 128
DEC_SEQ = 4
PAST_LEN = 2048
PAGE_SIZE = 128

HEAD_DIM = 64
ATTN_PATTERNS = ((128, 1), (512, 4), (2048, 16))
N_ATTN_GROUPS = len(ATTN_PATTERNS)
HEADS_PER_GROUP = 4
N_HEADS = N_ATTN_GROUPS * HEADS_PER_GROUP
ATTN_WIDTH = N_HEADS * HEAD_DIM
ATTN_MIX_WIDTH = HEADS_PER_GROUP * HEAD_DIM
ROT_DIM = HEAD_DIM // 4
ROPE_THETA = 500000.0
CONV_CH = D_MODEL // 2
CONV_WIDTH = 31
N_BRANCHES = 2
IN_WIDTH = 3 * ATTN_WIDTH + 2 * CONV_CH + N_BRANCHES * D_MODEL
N_EXPERT_GROUPS = 4
EXPERTS_PER_GROUP = 8
N_EXPERTS = N_EXPERT_GROUPS * EXPERTS_PER_GROUP
TOP_K = 2
D_EXPERT = D_MODEL // 2
MOE_BLOCK = 128
DN_ALPHA = (2 * DEPTH) ** 0.25
DN_BETA = (8 * DEPTH) ** -0.25
LN_EPS = 1e-5
SCALE = HEAD_DIM ** -0.5
NEG_INF = -1e30

kernel_name = "dilated_conv_hier_moe_decoder_step"


def layer_norm(x, g, b):
    xf = x.astype(jnp.float32)
    mu = jnp.mean(xf, -1, keepdims=True)
    var = jnp.mean(jnp.square(xf - mu), -1, keepdims=True)
    return ((xf - mu) * lax.rsqrt(var + LN_EPS) * g.astype(jnp.float32) + b.astype(jnp.float32)).astype(x.dtype)


def partial_rotary(x, pos):
    half = ROT_DIM // 2
    inv_freq = ROPE_THETA ** (-jnp.arange(half, dtype=jnp.float32) / half)
    ang = pos.astype(jnp.float32)[:, None] * inv_freq[None, :]
    cos, sin = jnp.cos(ang)[:, None, :], jnp.sin(ang)[:, None, :]
    xr = x[..., :ROT_DIM].astype(jnp.float32)
    x1, x2 = xr[..., :half], xr[..., half:]
    rot = jnp.concatenate([x1 * cos - x2 * sin, x2 * cos + x1 * sin], -1).astype(x.dtype)
    return jnp.concatenate([rot, x[..., ROT_DIM:]], -1)


def project_inputs(x, w_in, b_in, pos):
    z = x @ w_in + b_in
    sizes = [ATTN_WIDTH] * 3 + [CONV_CH, CONV_CH, N_BRANCHES * D_MODEL]
    q, k, v, ua, ub, gate_logits = jnp.split(z, np.cumsum(sizes)[:-1].tolist(), axis=-1)
    hs = x.shape[:-1] + (N_HEADS, HEAD_DIM)
    q = partial_rotary(q.reshape(hs), pos)
    k = partial_rotary(k.reshape(hs), pos)
    u = ua * jax.nn.sigmoid(ub)
    return q, k, v.reshape(hs), u, gate_logits


def masked_softmax_stats(s):
    m = jnp.max(s, -1, keepdims=True)
    p = jnp.exp(s - m)
    l = jnp.sum(p, -1, keepdims=True)
    return p / l, (m + jnp.log(l))[..., 0]


def dilated_group_prompt(q, k, v, dil, steps):
    B, S, H, E = q.shape
    L = S // dil
    nb = -(-L // steps)
    Lp = nb * steps

    def to_blocks(t):
        t = t.reshape(B, L, dil, H, E).transpose(0, 2, 1, 3, 4)
        t = jnp.pad(t, ((0, 0), (0, 0), (0, Lp - L), (0, 0), (0, 0)))
        return t.reshape(B, dil, nb, steps, H, E)

    qb, kb, vb = to_blocks(q), to_blocks(k), to_blocks(v)

    def with_prev(t):
        prev = jnp.pad(t[:, :, :-1], ((0, 0), (0, 0), (1, 0), (0, 0), (0, 0), (0, 0)))
        return jnp.concatenate([prev, t], axis=3)

    kw, vw = with_prev(kb), with_prev(vb)
    s = jnp.einsum('bdnqhe,bdnkhe->bdnhqk', qb, kw, preferred_element_type=jnp.float32) * SCALE
    qi = jnp.arange(steps)[:, None]
    kj = jnp.arange(2 * steps)[None, :]
    band = (kj >= qi) & (kj <= qi + steps)
    has_prev = (jnp.arange(nb) > 0)[:, None, None] | (kj >= steps)[None]
    mask = (band[None] & has_prev)[:, None]
    p, lse = masked_softmax_stats(jnp.where(mask, s, NEG_INF))
    o = jnp.einsum('bdnhqk,bdnkhe->bdnqhe', p, vw.astype(jnp.float32))
    o = o.reshape(B, dil, Lp, H, E)[:, :, :L].transpose(0, 2, 1, 3, 4).reshape(B, S, H, E)
    lse = jnp.swapaxes(lse, -1, -2).reshape(B, dil, Lp, H)[:, :, :L].transpose(0, 2, 1, 3).reshape(B, S, H)
    return o, lse


def dilated_group_sample(q, k, v, cache_kv, window, dil):
    steps = window // dil
    T = q.shape[1]
    Wb = cache_kv.shape[1]
    full = jnp.concatenate([cache_kv, jnp.stack([k, v], axis=2).astype(cache_kv.dtype)], axis=1)
    idx = Wb + jnp.arange(T)[:, None] - dil * jnp.arange(steps + 1)[None, :]
    valid = idx >= 0
    sel = full[:, jnp.maximum(idx, 0)]
    s = jnp.einsum('bthe,btkhe->bthk', q, sel[:, :, :, 0], preferred_element_type=jnp.float32) * SCALE
    p, lse = masked_softmax_stats(jnp.where(valid[:, None, :], s, NEG_INF))
    o = jnp.einsum('bthk,btkhe->bthe', p, sel[:, :, :, 1].astype(jnp.float32))
    keep = min(window, Wb + T)
    return o, lse, full[:, -keep:]


def combine_groups(outs, lses):
    o = jnp.stack(outs, axis=-3)
    lse = jnp.stack(lses, axis=-2)
    w = jax.nn.softmax(lse, axis=-2)
    mixed = jnp.sum(o * w[..., None], axis=-3)
    return mixed.reshape(mixed.shape[:-2] + (ATTN_MIX_WIDTH,))


def causal_depthwise(u_ext, w_dw, b_dw):
    out = lax.conv_general_dilated(u_ext, w_dw[:, None, :].astype(u_ext.dtype), (1,), 'VALID',
                                   dimension_numbers=('NWC', 'WIO', 'NWC'), feature_group_count=CONV_CH)
    return out + b_dw


def mix_prompt(x, w_in, b_in, w_dw, b_dw):
    S = x.shape[1]
    q, k, v, u, gate_logits = project_inputs(x, w_in, b_in, jnp.arange(S))
    outs, lses, kv_states = [], [], []
    for g, (window, dil) in enumerate(ATTN_PATTERNS):
        hs = slice(g * HEADS_PER_GROUP, (g + 1) * HEADS_PER_GROUP)
        o, lse = dilated_group_prompt(q[:, :, hs], k[:, :, hs], v[:, :, hs], dil, window // dil)
        outs.append(o)
        lses.append(lse)
        kv_states.append(jnp.stack([k[:, :, hs], v[:, :, hs]], axis=2)[:, -min(window, S):])
    attn = combine_groups(outs, lses).astype(x.dtype)
    conv = causal_depthwise(jnp.pad(u, ((0, 0), (CONV_WIDTH - 1, 0), (0, 0))), w_dw, b_dw)
    return attn, conv, gate_logits, kv_states, u[:, -(CONV_WIDTH - 1):]


def mix_sample(x, kv_caches, conv_state, w_in, b_in, w_dw, b_dw):
    T = x.shape[1]
    q, k, v, u, gate_logits = project_inputs(x, w_in, b_in, PAST_LEN + jnp.arange(T))
    outs, lses, kv_states = [], [], []
    for g, (window, dil) in enumerate(ATTN_PATTERNS):
        hs = slice(g * HEADS_PER_GROUP, (g + 1) * HEADS_PER_GROUP)
        o, lse, kv_new = dilated_group_sample(q[:, :, hs], k[:, :, hs], v[:, :, hs], kv_caches[g], window, dil)
        outs.append(o)
        lses.append(lse)
        kv_states.append(kv_new)
    attn = combine_groups(outs, lses).astype(x.dtype)
    u_ext = jnp.concatenate([conv_state.astype(u.dtype), u], axis=1)
    conv = causal_depthwise(u_ext, w_dw, b_dw)
    return attn, conv, gate_logits, kv_states, u_ext[:, -(CONV_WIDTH - 1):]


def hierarchical_moe(h, w_rg, b_rg, w_re, b_re, w_eg, w_eu, w_ed):
    xt = h.reshape(-1, D_MODEL)
    N = xt.shape[0]
    xf = xt.astype(jnp.float32)
    g_logits = xf @ w_rg.astype(jnp.float32) + b_rg.astype(jnp.float32)
    g_sel = jnp.argmax(g_logits, axis=-1)
    p_g = jnp.take_along_axis(jax.nn.softmax(g_logits, -1), g_sel[:, None], 1)
    e_logits = (xf @ w_re.astype(jnp.float32) + b_re.astype(jnp.float32)).reshape(N, N_EXPERT_GROUPS, EXPERTS_PER_GROUP)
    e_logits = jnp.take_along_axis(e_logits, g_sel[:, None, None], 1)[:, 0]
    top_v, top_i = lax.top_k(e_logits, TOP_K)
    gate = p_g * jax.nn.softmax(top_v, -1)
    expert = g_sel[:, None] * EXPERTS_PER_GROUP + top_i
    A = N * TOP_K
    e_flat = expert.reshape(A)
    order = jnp.argsort(e_flat)
    e_s = e_flat[order]
    tok_s = (jnp.arange(A) // TOP_K)[order]
    gate_s = gate.reshape(A)[order]
    counts = jnp.bincount(e_flat, length=N_EXPERTS)
    padded = ((counts + MOE_BLOCK - 1) // MOE_BLOCK) * MOE_BLOCK
    pad_end = jnp.cumsum(padded)
    pad_start = pad_end - padded
    start = jnp.cumsum(counts) - counts
    dest = pad_start[e_s] + jnp.arange(A) - start[e_s]
    n_blocks = -(-A // MOE_BLOCK) + N_EXPERTS
    rows = n_blocks * MOE_BLOCK
    buf = jnp.zeros((rows, D_MODEL), xt.dtype).at[dest].set(xt[tok_s])
    block_expert = jnp.minimum(jnp.searchsorted(pad_end, jnp.arange(n_blocks) * MOE_BLOCK, side='right'), N_EXPERTS - 1)

    def expert_block(args):
        xb, e = args
        return (jax.nn.silu(xb @ w_eg[e]) * (xb @ w_eu[e])) @ w_ed[e]

    yb = lax.map(expert_block, (buf.reshape(n_blocks, MOE_BLOCK, D_MODEL), block_expert)).reshape(rows, D_MODEL)
    y_rows = yb[dest] * gate_s[:, None].astype(yb.dtype)
    out = jnp.zeros((N, D_MODEL), yb.dtype).at[tok_s].add(y_rows)
    return out.reshape(h.shape)


def merge_and_channel_mix(x, attn, conv, gate_logits, conv_ln_g, conv_ln_b, w_conv_out, b_conv_out, w_attn_out,
                          w_o, ln1_g, ln1_b, w_rg, b_rg, w_re, b_re, w_eg, w_eu, w_ed, ln2_g, ln2_b):
    branch_c = jax.nn.silu(layer_norm(conv, conv_ln_g, conv_ln_b)) @ w_conv_out + b_conv_out
    branch_a = attn @ w_attn_out
    g_a, g_c = jnp.split(jax.nn.sigmoid(gate_logits), 2, axis=-1)
    y = (g_a * branch_a + g_c * branch_c) @ w_o
    h = layer_norm(DN_ALPHA * x + y, ln1_g, ln1_b)
    return layer_norm(DN_ALPHA * h + hierarchical_moe(h, w_rg, b_rg, w_re, b_re, w_eg, w_eu, w_ed), ln2_g, ln2_b)


def setup_inputs(seed: int = 0) -> dict:
    key = jax.random.key(seed)
    ks = jax.random.split(key, 32)

    def nrm(k, shape, scale=1.0):
        return scale * jax.random.normal(k, shape, jnp.float32)

    L = DEPTH
    wb = [min(w, PAST_LEN) for w, _ in ATTN_PATTERNS]
    kvs = (HEADS_PER_GROUP, HEAD_DIM)
    return {
        "x_prompt": nrm(ks[0], (BATCH, SEQ, D_MODEL)),
        "x_sample": nrm(ks[1], (DEC_BATCH, DEC_SEQ, D_MODEL)),
        "cache_kv_w128": nrm(ks[2], (L, DEC_BATCH, wb[0], 2) + kvs),
        "cache_kv_w512": nrm(ks[3], (L, DEC_BATCH, wb[1], 2) + kvs),
        "cache_kv_w2048": nrm(ks[4], (L, DEC_BATCH, wb[2], 2) + kvs),
        "state_conv": nrm(ks[5], (L, DEC_BATCH, CONV_WIDTH - 1, CONV_CH), 0.5),
        "w_in": nrm(ks[6], (L, D_MODEL, IN_WIDTH), D_MODEL ** -0.5),
        "b_in": nrm(ks[7], (L, IN_WIDTH), 0.01),
        "w_dw": nrm(ks[8], (L, CONV_WIDTH, CONV_CH), CONV_WIDTH ** -0.5),
        "b_dw": nrm(ks[9], (L, CONV_CH), 0.01),
        "conv_ln_g": 1.0 + nrm(ks[10], (L, CONV_CH), 0.01),
        "conv_ln_b": nrm(ks[11], (L, CONV_CH), 0.01),
        "w_conv_out": nrm(ks[12], (L, CONV_CH, D_MODEL), CONV_CH ** -0.5),
        "b_conv_out": nrm(ks[13], (L, D_MODEL), 0.01),
        "w_attn_out": nrm(ks[14], (L, ATTN_MIX_WIDTH, D_MODEL), ATTN_MIX_WIDTH ** -0.5),
        "w_o": nrm(ks[15], (L, D_MODEL, D_MODEL), DN_BETA * D_MODEL ** -0.5),
        "ln1_g": 1.0 + nrm(ks[16], (L, D_MODEL), 0.01),
        "ln1_b": nrm(ks[17], (L, D_MODEL), 0.01),
        "w_router_group": nrm(ks[18], (L, D_MODEL, N_EXPERT_GROUPS), D_MODEL ** -0.5),
        "b_router_group": nrm(ks[19], (L, N_EXPERT_GROUPS), 0.01),
        "w_router_expert": nrm(ks[20], (L, D_MODEL, N_EXPERTS), D_MODEL ** -0.5),
        "b_router_expert": nrm(ks[21], (L, N_EXPERTS), 0.01),
        "w_expert_gate": nrm(ks[22], (L, N_EXPERTS, D_MODEL, D_EXPERT), D_MODEL ** -0.5),
        "w_expert_up": nrm(ks[23], (L, N_EXPERTS, D_MODEL, D_EXPERT), D_MODEL ** -0.5),
        "w_expert_down": nrm(ks[24], (L, N_EXPERTS, D_EXPERT, D_MODEL), DN_BETA * D_EXPERT ** -0.5),
        "ln2_g": 1.0 + nrm(ks[25], (L, D_MODEL), 0.01),
        "ln2_b": nrm(ks[26], (L, D_MODEL), 0.01),
    }


def reference(x_prompt, x_sample, cache_kv_w128, cache_kv_w512, cache_kv_w2048, state_conv, w_in, b_in, w_dw,
              b_dw, conv_ln_g, conv_ln_b, w_conv_out, b_conv_out, w_attn_out, w_o, ln1_g, ln1_b, w_router_group,
              b_router_group, w_router_expert, b_router_expert, w_expert_gate, w_expert_up, w_expert_down,
              ln2_g, ln2_b):
    y_prompt, y_sample = x_prompt, x_sample
    kvp = [[], [], []]
    kvs = [[], [], []]
    convp, convs = [], []
    for l in range(DEPTH):
        tail = (conv_ln_g[l], conv_ln_b[l], w_conv_out[l], b_conv_out[l], w_attn_out[l], w_o[l], ln1_g[l], ln1_b[l],
                w_router_group[l], b_router_group[l], w_router_expert[l], b_router_expert[l], w_expert_gate[l],
                w_expert_up[l], w_expert_down[l], ln2_g[l], ln2_b[l])
        a, c, gl, kv_new, conv_new = mix_prompt(y_prompt, w_in[l], b_in[l], w_dw[l], b_dw[l])
        y_prompt = merge_and_channel_mix(y_prompt, a, c, gl, *tail)
        for g in range(N_ATTN_GROUPS):
            kvp[g].append(kv_new[g])
        convp.append(conv_new)
        caches = (cache_kv_w128[l], cache_kv_w512[l], cache_kv_w2048[l])
        a, c, gl, kv_new, conv_new = mix_sample(y_sample, caches, state_conv[l], w_in[l], b_in[l], w_dw[l], b_dw[l])
        y_sample = merge_and_channel_mix(y_sample, a, c, gl, *tail)
        for g in range(N_ATTN_GROUPS):
            kvs[g].append(kv_new[g])
        convs.append(conv_new)
    kv_w128_prompt = jnp.stack(kvp[0])
    kv_w512_prompt = jnp.stack(kvp[1])
    kv_w2048_prompt = jnp.stack(kvp[2])
    conv_prompt = jnp.stack(convp)
    kv_w128_sample = jnp.stack(kvs[0])
    kv_w512_sample = jnp.stack(kvs[1])
    kv_w2048_sample = jnp.stack(kvs[2])
    conv_sample = jnp.stack(convs)
    return (y_prompt, y_sample, kv_w128_prompt, kv_w512_prompt, kv_w2048_prompt, conv_prompt,
            kv_w128_sample, kv_w512_sample, kv_w2048_sample, conv_sample)
```

```python
import functools

import jax
import jax.numpy as jnp
import numpy as np
from jax import lax
from jax.experimental import pallas as pl
from jax.experimental.pallas import tpu as pltpu

F32 = jnp.float32
BF16 = jnp.bfloat16

D_MODEL = 1024
HEAD_DIM = 64
HEADS_PER_GROUP = 4
GROUP_W = HEADS_PER_GROUP * HEAD_DIM
ATTN_PATTERNS = ((128, 1), (512, 4), (2048, 16))
N_GROUPS = len(ATTN_PATTERNS)
ATTN_WIDTH = N_GROUPS * GROUP_W
STEPS = 128
ROT_HALF = 8
ROPE_THETA = 500000.0
CONV_CH = 512
CONV_WIDTH = 31
HALO = 32
N_EXPERTS = 32
EXPERTS_PER_GROUP = 8
N_EXPERT_GROUPS = 4
D_EXPERT = 512
DEPTH = 1
DN_ALPHA = (2 * DEPTH) ** 0.25
LN_EPS = 1e-5
SCALE = HEAD_DIM ** -0.5
NEG_INF = -1e30
LANES = 128
ROW_TILE = 256
CHUNK = 256
VMEM_LIMIT = 56 * 1024 * 1024

_Q0, _K0, _V0 = 0, ATTN_WIDTH, 2 * ATTN_WIDTH
_UA0 = 3 * ATTN_WIDTH
_UB0 = _UA0 + CONV_CH
_G0 = _UB0 + CONV_CH
IN_WIDTH = _G0 + 2 * D_MODEL


def _cparams(sem):
    return pltpu.CompilerParams(dimension_semantics=sem, vmem_limit_bytes=VMEM_LIMIT)


def _layer_norm(x, g, b):
    mu = jnp.mean(x, -1, keepdims=True)
    xc = x - mu
    var = jnp.mean(xc * xc, -1, keepdims=True)
    return xc * lax.rsqrt(var + LN_EPS) * g + b


def _rotary_tables(pos):
    inv_freq = ROPE_THETA ** (-jnp.arange(ROT_HALF, dtype=F32) / ROT_HALF)
    ang = pos.astype(F32)[:, None] * inv_freq[None, :]
    cos, sin = jnp.cos(ang), jnp.sin(ang)
    one = jnp.ones_like(cos)
    zero = jnp.zeros_like(cos)
    pad = HEAD_DIM - 2 * ROT_HALF
    p = pos.shape[0]
    c_head = jnp.concatenate([cos, cos, jnp.ones((p, pad), F32)], -1)
    sa_head = jnp.concatenate([zero, sin, jnp.zeros((p, pad), F32)], -1)
    sb_head = jnp.concatenate([-sin, zero, jnp.zeros((p, pad), F32)], -1)
    del one
    rep = LANES // HEAD_DIM
    return (jnp.tile(c_head, (1, rep)), jnp.tile(sa_head, (1, rep)), jnp.tile(sb_head, (1, rep)))


def _in_proj_kernel(x_ref, w_ref, b_ref, c_ref, sa_ref, sb_ref, qkv_ref, kvf_ref, u_ref, gate_ref):
    xb = x_ref[...].astype(BF16)

    def mm(c0, c1):
        return jnp.dot(xb, w_ref[:, c0:c1], preferred_element_type=F32) + b_ref[:, c0:c1]

    c, sa, sb = c_ref[...], sa_ref[...], sb_ref[...]

    def rotary(z):
        parts = []
        for j in range(z.shape[1] // LANES):
            zc = z[:, j * LANES:(j + 1) * LANES]
            parts.append(zc * c + pltpu.roll(zc, ROT_HALF, 1) * sa + pltpu.roll(zc, LANES - ROT_HALF, 1) * sb)
        return jnp.concatenate(parts, axis=1)

    q = rotary(mm(_Q0, _K0)) * SCALE
    k = rotary(mm(_K0, _V0))
    v = mm(_V0, _UA0)
    for g in range(N_GROUPS):
        gs = slice(g * GROUP_W, (g + 1) * GROUP_W)
        base = g * 3 * GROUP_W
        qkv_ref[:, base:base + GROUP_W] = q[:, gs].astype(BF16)
        qkv_ref[:, base + GROUP_W:base + 2 * GROUP_W] = k[:, gs].astype(BF16)
        qkv_ref[:, base + 2 * GROUP_W:base + 3 * GROUP_W] = v[:, gs].astype(BF16)
        kvf_ref[:, 2 * g * GROUP_W:(2 * g + 1) * GROUP_W] = k[:, gs]
        kvf_ref[:, (2 * g + 1) * GROUP_W:(2 * g + 2) * GROUP_W] = v[:, gs]
    u_ref[...] = mm(_UA0, _UB0) * jax.nn.sigmoid(mm(_UB0, _G0))
    gate_ref[...] = jax.nn.sigmoid(mm(_G0, IN_WIDTH))


def _in_proj(x2d, w_bf, b2d, tables, tm):
    n = x2d.shape[0]
    p = tables[0].shape[0]
    assert n % tm == 0 and p % tm == 0
    tpb = p // tm
    row = lambda i: (i, 0)
    tab = lambda i: (i % tpb, 0)
    whole = lambda i: (0, 0)
    return pl.pallas_call(
        _in_proj_kernel,
        grid=(n // tm,),
        in_specs=[
            pl.BlockSpec((tm, D_MODEL), row),
            pl.BlockSpec((D_MODEL, IN_WIDTH), whole),
            pl.BlockSpec((1, IN_WIDTH), whole),
            pl.BlockSpec((tm, LANES), tab),
            pl.BlockSpec((tm, LANES), tab),
            pl.BlockSpec((tm, LANES), tab),
        ],
        out_specs=[
            pl.BlockSpec((tm, 3 * ATTN_WIDTH), row),
            pl.BlockSpec((tm, 2 * ATTN_WIDTH), row),
            pl.BlockSpec((tm, CONV_CH), row),
            pl.BlockSpec((tm, 2 * D_MODEL), row),
        ],
        out_shape=[
            jax.ShapeDtypeStruct((n, 3 * ATTN_WIDTH), BF16),
            jax.ShapeDtypeStruct((n, 2 * ATTN_WIDTH), F32),
            jax.ShapeDtypeStruct((n, CONV_CH), F32),
            jax.ShapeDtypeStruct((n, 2 * D_MODEL), F32),
        ],
        compiler_params=_cparams(("parallel",)),
        name="in_proj",
    )(x2d, w_bf, b2d, *tables)


def _head_masks(shape, lane_axis):
    lane = lax.broadcasted_iota(jnp.int32, shape, lane_axis)
    return [(lane >= h * HEAD_DIM) & (lane < (h + 1) * HEAD_DIM) for h in range(HEADS_PER_GROUP)]


def _prompt_attn_kernel(*refs, has_prev):
    if has_prev:
        q_ref, kp_ref, kc_ref, vp_ref, vc_ref, o_ref, lse_ref = refs
    else:
        q_ref, kc_ref, vc_ref, o_ref, lse_ref = refs
    q = q_ref[0]
    masks = _head_masks(q.shape, 1)
    zero = jnp.zeros_like(q)
    q_stack = jnp.concatenate([jnp.where(m, q, zero) for m in masks], axis=0)
    if has_prev:
        keys = jnp.concatenate([kp_ref[0], kc_ref[0]], axis=0)
        vals = jnp.concatenate([vp_ref[0], vc_ref[0]], axis=0)
    else:
        keys, vals = kc_ref[0], vc_ref[0]
    nk = keys.shape[0]
    s = lax.dot_general(q_stack, keys, (((1,), (1,)), ((), ())), preferred_element_type=F32)
    qi = lax.broadcasted_iota(jnp.int32, s.shape, 0) % STEPS
    kj = lax.broadcasted_iota(jnp.int32, s.shape, 1)
    if has_prev:
        ok = (kj >= qi) & (kj <= qi + STEPS) & ((pl.program_id(2) > 0) | (kj >= STEPS))
    else:
        ok = kj <= qi
    s = jnp.where(ok, s, NEG_INF)
    m = jnp.max(s, -1, keepdims=True)
    p = jnp.exp(s - m)
    l = jnp.sum(p, -1, keepdims=True)
    pv = jnp.dot(p.astype(BF16), vals, preferred_element_type=F32)
    lse = m + jnp.log(l)
    o = jnp.zeros((STEPS, GROUP_W), F32)
    lse_o = jnp.zeros((STEPS, GROUP_W), F32)
    for h, hm in enumerate(masks):
        rs = slice(h * STEPS, (h + 1) * STEPS)
        o = jnp.where(hm, pv[rs] / l[rs], o)
        lse_o = jnp.where(hm, lse[rs], lse_o)
    o_ref[0] = o
    lse_ref[0] = lse_o
    del nk


def _prompt_attn(qkv3, g, dil):
    b, s, w = qkv3.shape
    l = s // dil
    nb = l // STEPS
    has_prev = nb > 1
    view = qkv3.reshape(b, l, dil * w)
    cpr = w // GROUP_W
    qc, kc, vc = 3 * g, 3 * g + 1, 3 * g + 2
    cur = lambda col: pl.BlockSpec((1, STEPS, GROUP_W), lambda bi, r, n: (bi, n, r * cpr + col))
    prev = lambda col: pl.BlockSpec((1, STEPS, GROUP_W), lambda bi, r, n: (bi, jnp.maximum(n - 1, 0), r * cpr + col))
    if has_prev:
        in_specs = [cur(qc), prev(kc), cur(kc), prev(vc), cur(vc)]
        args = (view,) * 5
    else:
        in_specs = [cur(qc), cur(kc), cur(vc)]
        args = (view,) * 3
    out_spec = pl.BlockSpec((1, STEPS, GROUP_W), lambda bi, r, n: (bi, n, r))
    o, lse = pl.pallas_call(
        functools.partial(_prompt_attn_kernel, has_prev=has_prev),
        grid=(b, dil, nb),
        in_specs=in_specs,
        out_specs=[out_spec, out_spec],
        out_shape=[jax.ShapeDtypeStruct((b, l, dil * GROUP_W), F32)] * 2,
        compiler_params=_cparams(("parallel", "parallel", "arbitrary")),
        name=f"prompt_attn_g{g}",
    )(*args)
    return o.reshape(b * s, GROUP_W), lse.reshape(b * s, GROUP_W)


def _sample_attn_kernel(q_ref, c0_ref, c1_ref, c2_ref, n0_ref, n1_ref, n2_ref,
                        c0_any, c1_any, c2_any, n0_any, n1_any, n2_any,
                        attn_ref, o0_any, o1_any, o2_any, sem, *, t_new, windows):
    b = pl.program_id(0)
    caches_any = (c0_any, c1_any, c2_any)
    news_any = (n0_any, n1_any, n2_any)
    outs_any = (o0_any, o1_any, o2_any)

    def copies():
        cps = []
        for g in range(N_GROUPS):
            wb = windows[g]
            cps.append(pltpu.make_async_copy(caches_any[g].at[0, :, pl.ds(t_new, wb - t_new)],
                                             outs_any[g].at[0, :, pl.ds(0, wb - t_new)], sem.at[2 * g]))
            cps.append(pltpu.make_async_copy(news_any[g], outs_any[g].at[0, :, pl.ds(wb - t_new, t_new)],
                                             sem.at[2 * g + 1]))
        return cps

    @pl.when(b == 0)
    def _():
        for cp in copies():
            cp.start()

    cache_refs = (c0_ref, c1_ref, c2_ref)
    new_refs = (n0_ref, n1_ref, n2_ref)
    for t in range(t_new):
        outs, lses = [], []
        for g, (_, dil) in enumerate(ATTN_PATTERNS):
            q = q_ref[0, t, g].astype(F32)
            cref, nref = cache_refs[g], new_refs[g]
            tt = t % dil
            kc = cref[0, :, tt, 0]
            vc = cref[0, :, tt, 1]
            s_c = jnp.sum(kc * q[None], -1, keepdims=True)
            if dil == 1:
                row = lax.broadcasted_iota(jnp.int32, s_c.shape, 0)
                s_c = jnp.where(row >= t, s_c, NEG_INF)
                new_rows = list(range(t + 1))
            else:
                new_rows = [t]
            s_n = [jnp.sum(nref[0, tn, 0] * q, -1, keepdims=True) for tn in new_rows]
            m = jnp.max(s_c, axis=0)
            for sn in s_n:
                m = jnp.maximum(m, sn)
            p_c = jnp.exp(s_c - m[None])
            l = jnp.sum(p_c, axis=0)
            acc = jnp.sum(p_c * vc, axis=0)
            for tn, sn in zip(new_rows, s_n):
                p_n = jnp.exp(sn - m)
                l = l + p_n
                acc = acc + p_n * nref[0, tn, 1]
            outs.append(acc / l)
            lses.append(m + jnp.log(l))
        mx = jnp.maximum(jnp.maximum(lses[0], lses[1]), lses[2])
        ws = [jnp.exp(ls - mx) for ls in lses]
        den = ws[0] + ws[1] + ws[2]
        attn_ref[0, t] = (outs[0] * ws[0] + outs[1] * ws[1] + outs[2] * ws[2]) / den

    @pl.when(b == pl.num_programs(0) - 1)
    def _():
        for cp in copies():
            cp.wait()


def _sample_attn(q5, caches, news):
    db, t_new = q5.shape[0], q5.shape[1]
    windows = tuple(c.shape[2] for c in caches)
    views, cache_specs = [], []
    for c, (_, dil) in zip(caches, ATTN_PATTERNS):
        wb = c.shape[2]
        assert wb == STEPS * dil and t_new <= STEPS
        views.append(c.reshape(db, STEPS, dil, 2, HEADS_PER_GROUP, HEAD_DIM))
        td = min(dil, t_new)
        cache_specs.append(pl.BlockSpec((1, STEPS, td, 2, HEADS_PER_GROUP, HEAD_DIM),
                                        lambda b: (b, 0, 0, 0, 0, 0)))
    new_spec = pl.BlockSpec((1, t_new, 2, HEADS_PER_GROUP, HEAD_DIM), lambda b: (b, 0, 0, 0, 0))
    any_spec = pl.BlockSpec(memory_space=pl.ANY)
    outs = pl.pallas_call(
        functools.partial(_sample_attn_kernel, t_new=t_new, windows=windows),
        grid=(db,),
        in_specs=[pl.BlockSpec((1, t_new, N_GROUPS, HEADS_PER_GROUP, HEAD_DIM), lambda b: (b, 0, 0, 0, 0))]
        + cache_specs + [new_spec] * 3 + [any_spec] * 6,
        out_specs=[pl.BlockSpec((1, t_new, HEADS_PER_GROUP, HEAD_DIM), lambda b: (b, 0, 0, 0))] + [any_spec] * 3,
        out_shape=[jax.ShapeDtypeStruct((db, t_new, HEADS_PER_GROUP, HEAD_DIM), F32)]
        + [jax.ShapeDtypeStruct(c.shape, c.dtype) for c in caches],
        scratch_shapes=[pltpu.SemaphoreType.DMA((2 * N_GROUPS,))],
        compiler_params=_cparams(("arbitrary",)),
        name="sample_attn",
    )(q5, *views, *news, *caches, *news)
    return outs[0], outs[1:]


def _sample_conv_kernel(state_ref, u_ref, w_ref, b_ref, conv_ref, sout_ref):
    hist = state_ref.shape[1]
    t_new = u_ref.shape[1]
    bb = u_ref.shape[0]
    for t in range(t_new):
        acc = jnp.broadcast_to(b_ref[...], (bb, CONV_CH))
        for j in range(CONV_WIDTH):
            r = t + j
            row = state_ref[:, r, :] if r < hist else u_ref[:, r - hist, :]
            acc = acc + row * w_ref[j:j + 1, :]
        conv_ref[:, t, :] = acc
    sout_ref[:, 0:hist - t_new, :] = state_ref[:, t_new:hist, :]
    sout_ref[:, hist - t_new:hist, :] = u_ref[...]


def _sample_conv(state, u3, w_dw, b_dw, bb=32):
    db, hist, _ = state.shape
    t_new = u3.shape[1]
    assert db % bb == 0 and hist == CONV_WIDTH - 1
    blk = lambda i: (i, 0, 0)
    whole = lambda i: (0, 0)
    return pl.pallas_call(
        _sample_conv_kernel,
        grid=(db // bb,),
        in_specs=[pl.BlockSpec((bb, hist, CONV_CH), blk), pl.BlockSpec((bb, t_new, CONV_CH), blk),
                  pl.BlockSpec((CONV_WIDTH, CONV_CH), whole), pl.BlockSpec((1, CONV_CH), whole)],
        out_specs=[pl.BlockSpec((bb, t_new, CONV_CH), blk), pl.BlockSpec((bb, hist, CONV_CH), blk)],
        out_shape=[jax.ShapeDtypeStruct((db, t_new, CONV_CH), F32), jax.ShapeDtypeStruct((db, hist, CONV_CH), F32)],
        compiler_params=_cparams(("parallel",)),
        name="sample_conv",
    )(state, u3, w_dw, b_dw)


ROUTE_LANES = 2 * LANES
ROW_SUB = D_MODEL // LANES


def _store_rows_as_tiles(ref3, val2d):
    for c in range(ROW_SUB):
        ref3[:, c, :] = val2d[:, c * LANES:(c + 1) * LANES]


def _load_tiles_as_rows(ref3):
    return jnp.concatenate([ref3[:, c, :] for c in range(ROW_SUB)], axis=1)


def _merge_kernel(*refs, prompt, tiles_per_seq):
    if prompt:
        (x_ref, u_ref, halo_ref, o0, o1, o2, l0, l1, l2, gate_ref, wdw_ref, bdw_ref, clg_ref, clb_ref, wco_ref,
         bco_ref, wao_ref, wo_ref, g1_ref, b1_ref, wr_ref, br_ref, base_ref, _h_any, _ri_any, _rg_any,
         h_ref, ri_ref, rg_ref, cnt_ref, base_scr, ext_scr) = refs
    else:
        (x_ref, conv_ref, attn_ref, gate_ref, clg_ref, clb_ref, wco_ref,
         bco_ref, wao_ref, wo_ref, g1_ref, b1_ref, wr_ref, br_ref, base_ref, _h_any, _ri_any, _rg_any,
         h_ref, ri_ref, rg_ref, cnt_ref, base_scr) = refs
    i = pl.program_id(0)
    tm = x_ref.shape[0]

    @pl.when(i == 0)
    def _():
        base_scr[...] = base_ref[...]

    if prompt:
        first = (i % tiles_per_seq) == 0
        ext_scr[0:HALO, :] = jnp.where(first, 0.0, halo_ref[...])
        ext_scr[HALO:HALO + tm, :] = u_ref[...]
        off = HALO - (CONV_WIDTH - 1)
        conv = jnp.broadcast_to(bdw_ref[...], (tm, CONV_CH))
        for j in range(CONV_WIDTH):
            conv = conv + ext_scr[off + j:off + j + tm, :] * wdw_ref[j:j + 1, :]
        lses = [l0[...], l1[...], l2[...]]
        mx = jnp.maximum(jnp.maximum(lses[0], lses[1]), lses[2])
        ws = [jnp.exp(ls - mx) for ls in lses]
        attn = (o0[...] * ws[0] + o1[...] * ws[1] + o2[...] * ws[2]) / (ws[0] + ws[1] + ws[2])
    else:
        conv = conv_ref[...]
        attn = attn_ref[...]

    cn = _layer_norm(conv, clg_ref[...], clb_ref[...])
    branch_c = jnp.dot(jax.nn.silu(cn).astype(BF16), wco_ref[...], preferred_element_type=F32) + bco_ref[...]
    branch_a = jnp.dot(attn.astype(BF16), wao_ref[...], preferred_element_type=F32)
    mixed = gate_ref[:, 0:D_MODEL] * branch_a + gate_ref[:, D_MODEL:2 * D_MODEL] * branch_c
    y = jnp.dot(mixed.astype(BF16), wo_ref[...], preferred_element_type=F32)
    h = _layer_norm(DN_ALPHA * x_ref[...] + y, g1_ref[...], b1_ref[...])
    _store_rows_as_tiles(h_ref, h)

    logits = jnp.dot(h.astype(BF16), wr_ref[...], preferred_element_type=F32) + br_ref[...]
    lane = lax.broadcasted_iota(jnp.int32, (tm, LANES), 1)
    big = jnp.int32(LANES)
    gl = jnp.where(lane < N_EXPERT_GROUPS, logits[:, 0:LANES], NEG_INF)
    gmax = jnp.max(gl, -1, keepdims=True)
    g_sel = jnp.min(jnp.where(gl == gmax, lane, big), -1, keepdims=True)
    p_g = 1.0 / jnp.sum(jnp.exp(gl - gmax), -1, keepdims=True)
    in_group = (lane < N_EXPERTS) & (lax.shift_right_logical(lane, 3) == g_sel)
    el = jnp.where(in_group, logits[:, LANES:2 * LANES], NEG_INF)
    v1 = jnp.max(el, -1, keepdims=True)
    i1 = jnp.min(jnp.where(el == v1, lane, big), -1, keepdims=True)
    el2 = jnp.where(lane == i1, NEG_INF, el)
    v2 = jnp.max(el2, -1, keepdims=True)
    i2 = jnp.min(jnp.where(el2 == v2, lane, big), -1, keepdims=True)
    e21 = jnp.exp(v2 - v1)
    gate1 = p_g / (1.0 + e21)
    gate2 = p_g * e21 / (1.0 + e21)

    oh = ((lane == i1) | (lane == i2)).astype(BF16)
    ri_ = lax.broadcasted_iota(jnp.int32, (tm, tm), 0)
    ci_ = lax.broadcasted_iota(jnp.int32, (tm, tm), 1)
    tri = (ci_ < ri_).astype(BF16)
    before = jnp.dot(tri, oh, preferred_element_type=F32) + base_scr[...]
    rank1 = jnp.sum(jnp.where(lane == i1, before, 0.0), -1, keepdims=True).astype(jnp.int32)
    rank2 = jnp.sum(jnp.where(lane == i2, before, 0.0), -1, keepdims=True).astype(jnp.int32)
    base_scr[...] = base_scr[...] + jnp.sum(oh.astype(F32), 0, keepdims=True)
    cnt_ref[...] = base_scr[...]

    ri_ref[...] = jnp.where(lane == 0, i1, jnp.where(lane == 1, i2, jnp.where(lane == 2, rank1,
                            jnp.where(lane == 3, rank2, 0))))
    rg_ref[...] = jnp.where(lane == 0, gate1, jnp.where(lane == 1, gate2, 0.0))


def _merge(prompt, n_total, row0, acts, weights, base, prev, tm, tiles_per_seq=1):
    n = acts[0].shape[0]
    assert n % tm == 0 and row0 % tm == 0
    t0 = row0 // tm
    row = lambda i: (i, 0)
    whole = lambda i: (0, 0)
    any_spec = pl.BlockSpec(memory_space=pl.ANY)
    if prompt:
        x2d, u, o0, o1, o2, l0, l1, l2, gate = acts
        hpt = tm // HALO
        halo_spec = pl.BlockSpec((HALO, CONV_CH), lambda i: (jnp.maximum(i * hpt - 1, 0), 0))
        act_args = [x2d, u, u, o0, o1, o2, l0, l1, l2, gate]
        act_specs = [pl.BlockSpec((tm, D_MODEL), row), pl.BlockSpec((tm, CONV_CH), row), halo_spec] + \
                    [pl.BlockSpec((tm, GROUP_W), row)] * 6 + [pl.BlockSpec((tm, 2 * D_MODEL), row)]
        scratch = [pltpu.VMEM((1, LANES), F32), pltpu.VMEM((HALO + tm, CONV_CH), F32)]
    else:
        x2d, conv, attn, gate = acts
        act_args = [x2d, conv, attn, gate]
        act_specs = [pl.BlockSpec((tm, D_MODEL), row), pl.BlockSpec((tm, CONV_CH), row),
                     pl.BlockSpec((tm, GROUP_W), row), pl.BlockSpec((tm, 2 * D_MODEL), row)]
        scratch = [pltpu.VMEM((1, LANES), F32)]
        weights = weights[2:]
    w_specs = [pl.BlockSpec(w.shape, whole) for w in weights]
    out_shape = [jax.ShapeDtypeStruct((n_total, ROW_SUB, LANES), F32), jax.ShapeDtypeStruct((n_total, LANES), jnp.int32),
                 jax.ShapeDtypeStruct((n_total, LANES), F32), jax.ShapeDtypeStruct((1, LANES), F32)]
    if prev is None:
        prev = [jnp.zeros((8, LANES), F32)] * 3
        aliases = {}
    else:
        first_prev = len(act_args) + len(weights) + 1
        aliases = {first_prev: 0, first_prev + 1: 1, first_prev + 2: 2}
    orow = lambda i: (i + t0, 0)
    return pl.pallas_call(
        functools.partial(_merge_kernel, prompt=prompt, tiles_per_seq=tiles_per_seq),
        grid=(n // tm,),
        in_specs=act_specs + w_specs + [pl.BlockSpec((1, LANES), whole)] + [any_spec] * 3,
        out_specs=[pl.BlockSpec((tm, ROW_SUB, LANES), lambda i: (i + t0, 0, 0)), pl.BlockSpec((tm, LANES), orow),
                   pl.BlockSpec((tm, LANES), orow),
                   pl.BlockSpec((1, LANES), whole)],
        out_shape=out_shape,
        scratch_shapes=scratch,
        input_output_aliases=aliases,
        compiler_params=_cparams(("arbitrary",)),
        name="merge_prompt" if prompt else "merge_sample",
    )(*act_args, *weights, base, *prev)


def _dispatch_kernel(bounds_ref, route_ref, h_ref, xg_any, zero_scr, sem_z, sem_r):
    i = pl.program_id(0)
    tm = h_ref.shape[0]

    def tail_copy(e):
        return pltpu.make_async_copy(zero_scr, xg_any.at[pl.ds(bounds_ref[e + 1] - CHUNK, CHUNK)], sem_z)

    @pl.when(i == 0)
    def _():
        zero_scr[...] = jnp.zeros_like(zero_scr)
        for e in range(N_EXPERTS):
            @pl.when(bounds_ref[e + 1] > bounds_ref[e])
            def _():
                tail_copy(e).start()
        for e in range(N_EXPERTS):
            @pl.when(bounds_ref[e + 1] > bounds_ref[e])
            def _():
                tail_copy(e).wait()

    def row_copy(t, k):
        dest = bounds_ref[route_ref[0, 0, 4 * t + k]] + route_ref[0, 0, 4 * t + 2 + k]
        return pltpu.make_async_copy(h_ref.at[t], xg_any.at[dest], sem_r)

    def issue(t, carry):
        row_copy(t, 0).start()
        row_copy(t, 1).start()
        return carry

    lax.fori_loop(0, tm, issue, 0)

    def drain(t, carry):
        row_copy(t, 0).wait()
        row_copy(t, 1).wait()
        return carry

    lax.fori_loop(0, tm, drain, 0)


def _dispatch(bounds, route_flat, h_all, n_rows, tm):
    n = h_all.shape[0]
    return pl.pallas_call(
        _dispatch_kernel,
        grid_spec=pltpu.PrefetchScalarGridSpec(
            num_scalar_prefetch=1,
            grid=(n // tm,),
            in_specs=[pl.BlockSpec((1, 1, 4 * tm), lambda i, b: (i, 0, 0), memory_space=pltpu.SMEM),
                      pl.BlockSpec((tm, ROW_SUB, LANES), lambda i, b: (i, 0, 0))],
            out_specs=pl.BlockSpec(memory_space=pl.ANY),
            scratch_shapes=[pltpu.VMEM((CHUNK, ROW_SUB, LANES), F32), pltpu.SemaphoreType.DMA(()),
                            pltpu.SemaphoreType.DMA(())],
        ),
        out_shape=jax.ShapeDtypeStruct((n_rows, ROW_SUB, LANES), F32),
        compiler_params=_cparams(("arbitrary",)),
        name="moe_dispatch",
    )(bounds, route_flat, h_all)


def _experts_kernel(ce_ref, nv_ref, x_ref, wg_ref, wu_ref, wd_ref, y_ref, wg_scr, wu_scr, wd_scr):
    c = pl.program_id(0)
    changed = (c == 0) | (ce_ref[c] != ce_ref[jnp.maximum(c - 1, 0)])

    @pl.when(changed)
    def _():
        wg_scr[...] = wg_ref[0].astype(BF16)
        wu_scr[...] = wu_ref[0].astype(BF16)
        wd_scr[...] = wd_ref[0].astype(BF16)

    @pl.when(c < nv_ref[0])
    def _():
        x = _load_tiles_as_rows(x_ref).astype(BF16)
        a = jnp.dot(x, wg_scr[...], preferred_element_type=F32)
        b = jnp.dot(x, wu_scr[...], preferred_element_type=F32)
        _store_rows_as_tiles(y_ref, jnp.dot((jax.nn.silu(a) * b).astype(BF16), wd_scr[...], preferred_element_type=F32))


def _experts(chunk_expert, n_valid, xg, w_eg, w_eu, w_ed):
    n_chunks = xg.shape[0] // CHUNK
    rows = lambda c, ce, nv: (jnp.minimum(c, nv[0] - 1), 0, 0)
    wsel = lambda c, ce, nv: (ce[c], 0, 0)
    return pl.pallas_call(
        _experts_kernel,
        grid_spec=pltpu.PrefetchScalarGridSpec(
            num_scalar_prefetch=2,
            grid=(n_chunks,),
            in_specs=[pl.BlockSpec((CHUNK, ROW_SUB, LANES), rows),
                      pl.BlockSpec((1, D_MODEL, D_EXPERT), wsel), pl.BlockSpec((1, D_MODEL, D_EXPERT), wsel),
                      pl.BlockSpec((1, D_EXPERT, D_MODEL), wsel)],
            out_specs=pl.BlockSpec((CHUNK, ROW_SUB, LANES), rows),
            scratch_shapes=[pltpu.VMEM((D_MODEL, D_EXPERT), BF16), pltpu.VMEM((D_MODEL, D_EXPERT), BF16),
                            pltpu.VMEM((D_EXPERT, D_MODEL), BF16)],
        ),
        out_shape=jax.ShapeDtypeStruct(xg.shape, F32),
        compiler_params=_cparams(("arbitrary",)),
        name="moe_experts",
    )(chunk_expert, n_valid, xg, w_eg, w_eu, w_ed)


def _combine_kernel(bounds_ref, route_ref, h_ref, rg_ref, g2_ref, b2_ref, yg_any, yp_ref, ys_ref, y1_scr, y2_scr, sem,
                    *, prompt_tiles):
    i = pl.program_id(0)
    tm = h_ref.shape[0]
    bufs = (y1_scr, y2_scr)

    def row_copy(t, k):
        src = bounds_ref[route_ref[0, 0, 4 * t + k]] + route_ref[0, 0, 4 * t + 2 + k]
        return pltpu.make_async_copy(yg_any.at[src], bufs[k].at[t], sem)

    def issue(t, carry):
        row_copy(t, 0).start()
        row_copy(t, 1).start()
        return carry

    lax.fori_loop(0, tm, issue, 0)

    def drain(t, carry):
        row_copy(t, 0).wait()
        row_copy(t, 1).wait()
        return carry

    lax.fori_loop(0, tm, drain, 0)

    moe = _load_tiles_as_rows(y1_scr) * rg_ref[:, 0:1] + _load_tiles_as_rows(y2_scr) * rg_ref[:, 1:2]
    out = _layer_norm(DN_ALPHA * _load_tiles_as_rows(h_ref) + moe, g2_ref[...], b2_ref[...])

    @pl.when(i < prompt_tiles)
    def _():
        yp_ref[...] = out

    @pl.when(i >= prompt_tiles)
    def _():
        ys_ref[...] = out


def _combine(bounds, route_flat, h_all, route_g, ln2_g, ln2_b, yg, n_prompt, tm):
    n = h_all.shape[0]
    pt = n_prompt // tm
    row = lambda i, b: (i, 0)
    whole = lambda i, b: (0, 0)
    return pl.pallas_call(
        functools.partial(_combine_kernel, prompt_tiles=pt),
        grid_spec=pltpu.PrefetchScalarGridSpec(
            num_scalar_prefetch=1,
            grid=(n // tm,),
            in_specs=[pl.BlockSpec((1, 1, 4 * tm), lambda i, b: (i, 0, 0), memory_space=pltpu.SMEM),
                      pl.BlockSpec((tm, ROW_SUB, LANES), lambda i, b: (i, 0, 0)), pl.BlockSpec((tm, LANES), row),
                      pl.BlockSpec((1, D_MODEL), whole), pl.BlockSpec((1, D_MODEL), whole),
                      pl.BlockSpec(memory_space=pl.ANY)],
            out_specs=[pl.BlockSpec((tm, D_MODEL), lambda i, b: (jnp.minimum(i, pt - 1), 0)),
                       pl.BlockSpec((tm, D_MODEL), lambda i, b: (jnp.maximum(i - pt, 0), 0))],
            scratch_shapes=[pltpu.VMEM((tm, ROW_SUB, LANES), F32), pltpu.VMEM((tm, ROW_SUB, LANES), F32),
                            pltpu.SemaphoreType.DMA(())],
        ),
        out_shape=[jax.ShapeDtypeStruct((n_prompt, D_MODEL), F32), jax.ShapeDtypeStruct((n - n_prompt, D_MODEL), F32)],
        compiler_params=_cparams(("arbitrary",)),
        name="moe_combine",
    )(bounds, route_flat, h_all, route_g, ln2_g, ln2_b, yg)


def kernel(x_prompt, x_sample, cache_kv_w128, cache_kv_w512, cache_kv_w2048, state_conv, w_in, b_in, w_dw, b_dw, conv_ln_g, conv_ln_b, w_conv_out, b_conv_out, w_attn_out, w_o, ln1_g, ln1_b, w_router_group, b_router_group, w_router_expert, b_router_expert, w_expert_gate, w_expert_up, w_expert_down, ln2_g, ln2_b):
    assert w_in.shape[0] == DEPTH == 1
    bsz, seq, _ = x_prompt.shape
    db, t_new, _ = x_sample.shape
    caches = (cache_kv_w128, cache_kv_w512, cache_kv_w2048)
    past = cache_kv_w2048.shape[2]
    n_p, n_s = bsz * seq, db * t_new
    n_all = n_p + n_s
    tm = ROW_TILE
    kvs = (HEADS_PER_GROUP, HEAD_DIM)

    w_bf = w_in[0].astype(BF16)
    xp2, xs2 = x_prompt.reshape(n_p, D_MODEL), x_sample.reshape(n_s, D_MODEL)
    qkv_p, kvf_p, u_p, gate_p = _in_proj(xp2, w_bf, b_in, _rotary_tables(jnp.arange(seq)), tm)
    pos_s = past + jnp.arange(n_s) % t_new
    qkv_s, kvf_s, u_s, gate_s = _in_proj(xs2, w_bf, b_in, _rotary_tables(pos_s), tm)

    qkv3 = qkv_p.reshape(bsz, seq, 3 * ATTN_WIDTH)
    og = [_prompt_attn(qkv3, g, dil) for g, (_, dil) in enumerate(ATTN_PATTERNS)]

    q5 = qkv_s.reshape(db, t_new, N_GROUPS, 3, *kvs)[:, :, :, 0]
    kv6 = kvf_s.reshape(db, t_new, N_GROUPS, 2, *kvs)
    news = [kv6[:, :, g] for g in range(N_GROUPS)]
    attn_s, kv_sample = _sample_attn(q5, caches, news)

    conv_s, conv_state_s = _sample_conv(state_conv[0], u_s.reshape(db, t_new, CONV_CH), w_dw[0], b_dw)

    w_r = jnp.zeros((D_MODEL, ROUTE_LANES), F32)
    w_r = w_r.at[:, 0:N_EXPERT_GROUPS].set(w_router_group[0]).at[:, LANES:LANES + N_EXPERTS].set(w_router_expert[0])
    b_r = jnp.zeros((1, ROUTE_LANES), F32)
    b_r = b_r.at[:, 0:N_EXPERT_GROUPS].set(b_router_group).at[:, LANES:LANES + N_EXPERTS].set(b_router_expert)
    weights = [w_dw[0], b_dw, conv_ln_g, conv_ln_b, w_conv_out[0].astype(BF16), b_conv_out, w_attn_out[0].astype(BF16),
               w_o[0].astype(BF16), ln1_g, ln1_b, w_r.astype(BF16), b_r]
    acts_p = [xp2, u_p, og[0][0], og[1][0], og[2][0], og[0][1], og[1][1], og[2][1], gate_p]
    h_all, route_i, route_g, counts_p = _merge(True, n_all, 0, acts_p, weights, jnp.zeros((1, LANES), F32), None, tm,
                                               tiles_per_seq=seq // tm)
    acts_s = [xs2, conv_s.reshape(n_s, CONV_CH), attn_s.reshape(n_s, GROUP_W), gate_s]
    h_all, route_i, route_g, counts = _merge(False, n_all, n_p, acts_s, weights, counts_p,
                                             (h_all, route_i, route_g), tm)

    cnt = counts[0, :N_EXPERTS].astype(jnp.int32)
    padded = ((cnt + CHUNK - 1) // CHUNK) * CHUNK
    pad_end = jnp.cumsum(padded)
    bounds = jnp.concatenate([jnp.zeros((1,), jnp.int32), pad_end]).astype(jnp.int32)
    n_chunks = (2 * n_all) // CHUNK + N_EXPERTS
    chunk_row0 = jnp.arange(n_chunks, dtype=jnp.int32) * CHUNK
    chunk_expert = jnp.minimum(jnp.sum((pad_end[None, :] <= chunk_row0[:, None]).astype(jnp.int32), axis=1),
                               N_EXPERTS - 1).astype(jnp.int32)
    n_valid = (pad_end[-1:] // CHUNK).astype(jnp.int32)
    route_flat = route_i[:, 0:4].reshape(n_all // tm, 1, 4 * tm)

    xg = _dispatch(bounds, route_flat, h_all, n_chunks * CHUNK, tm)
    yg = _experts(chunk_expert, n_valid, xg, w_expert_gate[0], w_expert_up[0], w_expert_down[0])
    y_p, y_s = _combine(bounds, route_flat, h_all, route_g, ln2_g, ln2_b, yg, n_p, tm)

    kvp = kvf_p.reshape(1, bsz, seq, N_GROUPS, 2, *kvs)
    kv_prompt = [kvp[:, :, seq - min(w, seq):, g] for g, (w, _) in enumerate(ATTN_PATTERNS)]
    conv_prompt = u_p.reshape(1, bsz, seq, CONV_CH)[:, :, seq - (CONV_WIDTH - 1):]
    return (y_p.reshape(bsz, seq, D_MODEL), y_s.reshape(db, t_new, D_MODEL),
            kv_prompt[0], kv_prompt[1], kv_prompt[2], conv_prompt,
            kv_sample[0], kv_sample[1], kv_sample[2], conv_state_s[None])
```

```python
import functools

import jax
import jax.numpy as jnp
import numpy as np
from jax import lax
from jax.experimental import pallas as pl
from jax.experimental.pallas import tpu as pltpu

F32 = jnp.float32
BF16 = jnp.bfloat16

D_MODEL = 1024
HEAD_DIM = 64
HEADS_PER_GROUP = 4
GROUP_W = HEADS_PER_GROUP * HEAD_DIM
ATTN_PATTERNS = ((128, 1), (512, 4), (2048, 16))
N_GROUPS = len(ATTN_PATTERNS)
ATTN_WIDTH = N_GROUPS * GROUP_W
STEPS = 128
ROT_HALF = 8
ROPE_THETA = 500000.0
CONV_CH = 512
CONV_WIDTH = 31
HALO = 32
N_EXPERTS = 32
EXPERTS_PER_GROUP = 8
N_EXPERT_GROUPS = 4
D_EXPERT = 512
DEPTH = 1
DN_ALPHA = (2 * DEPTH) ** 0.25
LN_EPS = 1e-5
SCALE = HEAD_DIM ** -0.5
NEG_INF = -1e30
LANES = 128
ROW_TILE = 256
CHUNK = 256
VMEM_LIMIT = 56 * 1024 * 1024

_Q0, _K0, _V0 = 0, ATTN_WIDTH, 2 * ATTN_WIDTH
_UA0 = 3 * ATTN_WIDTH
_UB0 = _UA0 + CONV_CH
_G0 = _UB0 + CONV_CH
IN_WIDTH = _G0 + 2 * D_MODEL


def _cparams(sem):
    return pltpu.CompilerParams(dimension_semantics=sem, vmem_limit_bytes=VMEM_LIMIT)


def _layer_norm(x, g, b):
    mu = jnp.mean(x, -1, keepdims=True)
    xc = x - mu
    var = jnp.mean(xc * xc, -1, keepdims=True)
    return xc * lax.rsqrt(var + LN_EPS) * g + b


def _rotary_tables(pos):
    inv_freq = ROPE_THETA ** (-jnp.arange(ROT_HALF, dtype=F32) / ROT_HALF)
    ang = pos.astype(F32)[:, None] * inv_freq[None, :]
    cos, sin = jnp.cos(ang), jnp.sin(ang)
    one = jnp.ones_like(cos)
    zero = jnp.zeros_like(cos)
    pad = HEAD_DIM - 2 * ROT_HALF
    p = pos.shape[0]
    c_head = jnp.concatenate([cos, cos, jnp.ones((p, pad), F32)], -1)
    sa_head = jnp.concatenate([zero, sin, jnp.zeros((p, pad), F32)], -1)
    sb_head = jnp.concatenate([-sin, zero, jnp.zeros((p, pad), F32)], -1)
    del one
    rep = LANES // HEAD_DIM
    return (jnp.tile(c_head, (1, rep)), jnp.tile(sa_head, (1, rep)), jnp.tile(sb_head, (1, rep)))


def _in_proj_kernel(x_ref, w_ref, b_ref, c_ref, sa_ref, sb_ref, qkv_ref, kvf_ref, u_ref, gate_ref):
    xb = x_ref[...].astype(BF16)

    def mm(c0, c1):
        return jnp.dot(xb, w_ref[:, c0:c1], preferred_element_type=F32) + b_ref[:, c0:c1]

    c, sa, sb = c_ref[...], sa_ref[...], sb_ref[...]

    def rotary(z):
        parts = []
        for j in range(z.shape[1] // LANES):
            zc = z[:, j * LANES:(j + 1) * LANES]
            parts.append(zc * c + pltpu.roll(zc, ROT_HALF, 1) * sa + pltpu.roll(zc, LANES - ROT_HALF, 1) * sb)
        return jnp.concatenate(parts, axis=1)

    q = rotary(mm(_Q0, _K0)) * SCALE
    k = rotary(mm(_K0, _V0))
    v = mm(_V0, _UA0)
    for g in range(N_GROUPS):
        gs = slice(g * GROUP_W, (g + 1) * GROUP_W)
        base = g * 3 * GROUP_W
        qkv_ref[:, base:base + GROUP_W] = q[:, gs].astype(BF16)
        qkv_ref[:, base + GROUP_W:base + 2 * GROUP_W] = k[:, gs].astype(BF16)
        qkv_ref[:, base + 2 * GROUP_W:base + 3 * GROUP_W] = v[:, gs].astype(BF16)
        kvf_ref[:, 2 * g * GROUP_W:(2 * g + 1) * GROUP_W] = k[:, gs]
        kvf_ref[:, (2 * g + 1) * GROUP_W:(2 * g + 2) * GROUP_W] = v[:, gs]
    u_ref[...] = mm(_UA0, _UB0) * jax.nn.sigmoid(mm(_UB0, _G0))
    gate_ref[...] = jax.nn.sigmoid(mm(_G0, IN_WIDTH))


def _in_proj(x2d, w_bf, b2d, tables, tm):
    n = x2d.shape[0]
    p = tables[0].shape[0]
    assert n % tm == 0 and p % tm == 0
    tpb = p // tm
    row = lambda i: (i, 0)
    tab = lambda i: (i % tpb, 0)
    whole = lambda i: (0, 0)
    return pl.pallas_call(
        _in_proj_kernel,
        grid=(n // tm,),
        in_specs=[
            pl.BlockSpec((tm, D_MODEL), row),
            pl.BlockSpec((D_MODEL, IN_WIDTH), whole),
            pl.BlockSpec((1, IN_WIDTH), whole),
            pl.BlockSpec((tm, LANES), tab),
            pl.BlockSpec((tm, LANES), tab),
            pl.BlockSpec((tm, LANES), tab),
        ],
        out_specs=[
            pl.BlockSpec((tm, 3 * ATTN_WIDTH), row),
            pl.BlockSpec((tm, 2 * ATTN_WIDTH), row),
            pl.BlockSpec((tm, CONV_CH), row),
            pl.BlockSpec((tm, 2 * D_MODEL), row),
        ],
        out_shape=[
            jax.ShapeDtypeStruct((n, 3 * ATTN_WIDTH), BF16),
            jax.ShapeDtypeStruct((n, 2 * ATTN_WIDTH), F32),
            jax.ShapeDtypeStruct((n, CONV_CH), F32),
            jax.ShapeDtypeStruct((n, 2 * D_MODEL), F32),
        ],
        compiler_params=_cparams(("parallel",)),
        name="in_proj",
    )(x2d, w_bf, b2d, *tables)


def _head_masks(shape, lane_axis):
    lane = lax.broadcasted_iota(jnp.int32, shape, lane_axis)
    return [(lane >= h * HEAD_DIM) & (lane < (h + 1) * HEAD_DIM) for h in range(HEADS_PER_GROUP)]


def _prompt_attn_kernel(*refs, has_prev):
    if has_prev:
        q_ref, kp_ref, kc_ref, vp_ref, vc_ref, o_ref, lse_ref = refs
    else:
        q_ref, kc_ref, vc_ref, o_ref, lse_ref = refs
    q = q_ref[0]
    masks = _head_masks(q.shape, 1)
    zero = jnp.zeros_like(q)
    q_stack = jnp.concatenate([jnp.where(m, q, zero) for m in masks], axis=0)
    if has_prev:
        keys = jnp.concatenate([kp_ref[0], kc_ref[0]], axis=0)
        vals = jnp.concatenate([vp_ref[0], vc_ref[0]], axis=0)
    else:
        keys, vals = kc_ref[0], vc_ref[0]
    nk = keys.shape[0]
    s = lax.dot_general(q_stack, keys, (((1,), (1,)), ((), ())), preferred_element_type=F32)
    qi = lax.broadcasted_iota(jnp.int32, s.shape, 0) % STEPS
    kj = lax.broadcasted_iota(jnp.int32, s.shape, 1)
    if has_prev:
        ok = (kj >= qi) & (kj <= qi + STEPS) & ((pl.program_id(2) > 0) | (kj >= STEPS))
    else:
        ok = kj <= qi
    s = jnp.where(ok, s, NEG_INF)
    m = jnp.max(s, -1, keepdims=True)
    p = jnp.exp(s - m)
    l = jnp.sum(p, -1, keepdims=True)
    pv = jnp.dot(p.astype(BF16), vals, preferred_element_type=F32)
    lse = m + jnp.log(l)
    o = jnp.zeros((STEPS, GROUP_W), F32)
    lse_o = jnp.zeros((STEPS, GROUP_W), F32)
    for h, hm in enumerate(masks):
        rs = slice(h * STEPS, (h + 1) * STEPS)
        o = jnp.where(hm, pv[rs] / l[rs], o)
        lse_o = jnp.where(hm, lse[rs], lse_o)
    o_ref[0] = o
    lse_ref[0] = lse_o
    del nk


def _prompt_attn(qkv3, g, dil):
    b, s, w = qkv3.shape
    l = s // dil
    nb = l // STEPS
    has_prev = nb > 1
    view = qkv3.reshape(b, l, dil * w)
    cpr = w // GROUP_W
    qc, kc, vc = 3 * g, 3 * g + 1, 3 * g + 2
    cur = lambda col: pl.BlockSpec((1, STEPS, GROUP_W), lambda bi, r, n: (bi, n, r * cpr + col))
    prev = lambda col: pl.BlockSpec((1, STEPS, GROUP_W), lambda bi, r, n: (bi, jnp.maximum(n - 1, 0), r * cpr + col))
    if has_prev:
        in_specs = [cur(qc), prev(kc), cur(kc), prev(vc), cur(vc)]
        args = (view,) * 5
    else:
        in_specs = [cur(qc), cur(kc), cur(vc)]
        args = (view,) * 3
    out_spec = pl.BlockSpec((1, STEPS, GROUP_W), lambda bi, r, n: (bi, n, r))
    o, lse = pl.pallas_call(
        functools.partial(_prompt_attn_kernel, has_prev=has_prev),
        grid=(b, dil, nb),
        in_specs=in_specs,
        out_specs=[out_spec, out_spec],
        out_shape=[jax.ShapeDtypeStruct((b, l, dil * GROUP_W), F32)] * 2,
        compiler_params=_cparams(("parallel", "parallel", "arbitrary")),
        name=f"prompt_attn_g{g}",
    )(*args)
    return o.reshape(b * s, GROUP_W), lse.reshape(b * s, GROUP_W)


KV_ROWS = 2 * GROUP_W


def _sample_attn_kernel(q_ref, c0_ref, c1_ref, c2_ref, n0_ref, n1_ref, n2_ref, attn_ref, o0_ref, o1_ref, o2_ref,
                        *, t_new):
    q = q_ref[0]
    outs, lses = [], []
    for g, (cref, nref, oref) in enumerate(((c0_ref, n0_ref, o0_ref), (c1_ref, n1_ref, o1_ref),
                                            (c2_ref, n2_ref, o2_ref))):
        dil = ATTN_PATTERNS[g][1]
        wb = cref.shape[2]
        full = jnp.concatenate([cref[0], nref[0]], axis=1)
        oref[0] = full[:, t_new:wb + t_new]
        k_t = full[0:GROUP_W].astype(BF16)
        v_t = full[GROUP_W:KV_ROWS].astype(BF16)
        qg = q[:, g * GROUP_W:(g + 1) * GROUP_W]
        masks = _head_masks(qg.shape, 1)
        q_stack = jnp.concatenate([jnp.where(m, qg, 0.0) for m in masks], axis=0).astype(BF16)
        s = jnp.dot(q_stack, k_t, preferred_element_type=F32)
        r = lax.broadcasted_iota(jnp.int32, s.shape, 1)
        t = lax.broadcasted_iota(jnp.int32, s.shape, 0) % t_new
        ok = (r >= t) & (r <= wb + t) & (((r - t) & (dil - 1)) == 0)
        s = jnp.where(ok, s, NEG_INF)
        m = jnp.max(s, -1, keepdims=True)
        p = jnp.exp(s - m)
        l = jnp.sum(p, -1, keepdims=True)
        pv = lax.dot_general(p.astype(BF16), v_t, (((1,), (1,)), ((), ())), preferred_element_type=F32)
        lse = m + jnp.log(l)
        o = jnp.zeros((t_new, GROUP_W), F32)
        lse_o = jnp.zeros((t_new, GROUP_W), F32)
        for h, hm in enumerate(masks):
            rs = slice(h * t_new, (h + 1) * t_new)
            o = jnp.where(hm, pv[rs] / l[rs], o)
            lse_o = jnp.where(hm, lse[rs], lse_o)
        outs.append(o)
        lses.append(lse_o)
    mx = jnp.maximum(jnp.maximum(lses[0], lses[1]), lses[2])
    ws = [jnp.exp(ls - mx) for ls in lses]
    attn_ref[0] = (outs[0] * ws[0] + outs[1] * ws[1] + outs[2] * ws[2]) / (ws[0] + ws[1] + ws[2])


def _sample_attn(q3, caches_t, news_t):
    db, t_new, _ = q3.shape
    for c, (_, dil) in zip(caches_t, ATTN_PATTERNS):
        assert c.shape[2] == STEPS * dil and t_new <= LANES and dil & (dil - 1) == 0
    blk = lambda b: (b, 0, 0)
    cache_specs = [pl.BlockSpec((1, KV_ROWS, c.shape[2]), blk) for c in caches_t]
    outs = pl.pallas_call(
        functools.partial(_sample_attn_kernel, t_new=t_new),
        grid=(db,),
        in_specs=[pl.BlockSpec((1, t_new, ATTN_WIDTH), blk)] + cache_specs + [pl.BlockSpec((1, KV_ROWS, LANES), blk)] * 3,
        out_specs=[pl.BlockSpec((1, t_new, GROUP_W), blk)] + cache_specs,
        out_shape=[jax.ShapeDtypeStruct((db, t_new, GROUP_W), F32)]
        + [jax.ShapeDtypeStruct(c.shape, c.dtype) for c in caches_t],
        compiler_params=_cparams(("parallel",)),
        name="sample_attn",
    )(q3, *caches_t, *news_t)
    return outs[0], outs[1:]


def _sample_conv_kernel(state_ref, u_ref, w_ref, b_ref, conv_ref, sout_ref):
    hist = state_ref.shape[1]
    t_new = u_ref.shape[1]
    bb = u_ref.shape[0]
    for t in range(t_new):
        acc = jnp.broadcast_to(b_ref[...], (bb, CONV_CH))
        for j in range(CONV_WIDTH):
            r = t + j
            row = state_ref[:, r, :] if r < hist else u_ref[:, r - hist, :]
            acc = acc + row * w_ref[j:j + 1, :]
        conv_ref[:, t, :] = acc
    sout_ref[:, 0:hist - t_new, :] = state_ref[:, t_new:hist, :]
    sout_ref[:, hist - t_new:hist, :] = u_ref[...]


def _sample_conv(state, u3, w_dw, b_dw, bb=32):
    db, hist, _ = state.shape
    t_new = u3.shape[1]
    assert db % bb == 0 and hist == CONV_WIDTH - 1
    blk = lambda i: (i, 0, 0)
    whole = lambda i: (0, 0)
    return pl.pallas_call(
        _sample_conv_kernel,
        grid=(db // bb,),
        in_specs=[pl.BlockSpec((bb, hist, CONV_CH), blk), pl.BlockSpec((bb, t_new, CONV_CH), blk),
                  pl.BlockSpec((CONV_WIDTH, CONV_CH), whole), pl.BlockSpec((1, CONV_CH), whole)],
        out_specs=[pl.BlockSpec((bb, t_new, CONV_CH), blk), pl.BlockSpec((bb, hist, CONV_CH), blk)],
        out_shape=[jax.ShapeDtypeStruct((db, t_new, CONV_CH), F32), jax.ShapeDtypeStruct((db, hist, CONV_CH), F32)],
        compiler_params=_cparams(("parallel",)),
        name="sample_conv",
    )(state, u3, w_dw, b_dw)


ROUTE_LANES = 2 * LANES
ROW_SUB = D_MODEL // LANES


def _store_rows_as_tiles(ref3, val2d):
    for c in range(ROW_SUB):
        ref3[:, c, :] = val2d[:, c * LANES:(c + 1) * LANES]


def _load_tiles_as_rows(ref3):
    return jnp.concatenate([ref3[:, c, :] for c in range(ROW_SUB)], axis=1)


def _merge_kernel(*refs, prompt, tiles_per_seq):
    if prompt:
        (x_ref, u_ref, halo_ref, o0, o1, o2, l0, l1, l2, gate_ref, wdw_ref, bdw_ref, clg_ref, clb_ref, wco_ref,
         bco_ref, wao_ref, wo_ref, g1_ref, b1_ref, wr_ref, br_ref, base_ref, _h_any, _ri_any, _rg_any,
         h_ref, ri_ref, rg_ref, cnt_ref, base_scr, ext_scr) = refs
    else:
        (x_ref, conv_ref, attn_ref, gate_ref, clg_ref, clb_ref, wco_ref,
         bco_ref, wao_ref, wo_ref, g1_ref, b1_ref, wr_ref, br_ref, base_ref, _h_any, _ri_any, _rg_any,
         h_ref, ri_ref, rg_ref, cnt_ref, base_scr) = refs
    i = pl.program_id(0)
    tm = x_ref.shape[0]

    @pl.when(i == 0)
    def _():
        base_scr[...] = base_ref[...]

    if prompt:
        first = (i % tiles_per_seq) == 0
        ext_scr[0:HALO, :] = jnp.where(first, 0.0, halo_ref[...])
        ext_scr[HALO:HALO + tm, :] = u_ref[...]
        off = HALO - (CONV_WIDTH - 1)
        conv = jnp.broadcast_to(bdw_ref[...], (tm, CONV_CH))
        for j in range(CONV_WIDTH):
            conv = conv + ext_scr[off + j:off + j + tm, :] * wdw_ref[j:j + 1, :]
        lses = [l0[...], l1[...], l2[...]]
        mx = jnp.maximum(jnp.maximum(lses[0], lses[1]), lses[2])
        ws = [jnp.exp(ls - mx) for ls in lses]
        attn = (o0[...] * ws[0] + o1[...] * ws[1] + o2[...] * ws[2]) / (ws[0] + ws[1] + ws[2])
    else:
        conv = conv_ref[...]
        attn = attn_ref[...]

    cn = _layer_norm(conv, clg_ref[...], clb_ref[...])
    branch_c = jnp.dot(jax.nn.silu(cn).astype(BF16), wco_ref[...], preferred_element_type=F32) + bco_ref[...]
    branch_a = jnp.dot(attn.astype(BF16), wao_ref[...], preferred_element_type=F32)
    mixed = gate_ref[:, 0:D_MODEL] * branch_a + gate_ref[:, D_MODEL:2 * D_MODEL] * branch_c
    y = jnp.dot(mixed.astype(BF16), wo_ref[...], preferred_element_type=F32)
    h = _layer_norm(DN_ALPHA * x_ref[...] + y, g1_ref[...], b1_ref[...])
    _store_rows_as_tiles(h_ref, h)

    logits = jnp.dot(h.astype(BF16), wr_ref[...], preferred_element_type=F32) + br_ref[...]
    lane = lax.broadcasted_iota(jnp.int32, (tm, LANES), 1)
    big = jnp.int32(LANES)
    gl = jnp.where(lane < N_EXPERT_GROUPS, logits[:, 0:LANES], NEG_INF)
    gmax = jnp.max(gl, -1, keepdims=True)
    g_sel = jnp.min(jnp.where(gl == gmax, lane, big), -1, keepdims=True)
    p_g = 1.0 / jnp.sum(jnp.exp(gl - gmax), -1, keepdims=True)
    in_group = (lane < N_EXPERTS) & (lax.shift_right_logical(lane, 3) == g_sel)
    el = jnp.where(in_group, logits[:, LANES:2 * LANES], NEG_INF)
    v1 = jnp.max(el, -1, keepdims=True)
    i1 = jnp.min(jnp.where(el == v1, lane, big), -1, keepdims=True)
    el2 = jnp.where(lane == i1, NEG_INF, el)
    v2 = jnp.max(el2, -1, keepdims=True)
    i2 = jnp.min(jnp.where(el2 == v2, lane, big), -1, keepdims=True)
    e21 = jnp.exp(v2 - v1)
    gate1 = p_g / (1.0 + e21)
    gate2 = p_g * e21 / (1.0 + e21)

    oh = ((lane == i1) | (lane == i2)).astype(BF16)
    ri_ = lax.broadcasted_iota(jnp.int32, (tm, tm), 0)
    ci_ = lax.broadcasted_iota(jnp.int32, (tm, tm), 1)
    tri = (ci_ < ri_).astype(BF16)
    before = jnp.dot(tri, oh, preferred_element_type=F32) + base_scr[...]
    rank1 = jnp.sum(jnp.where(lane == i1, before, 0.0), -1, keepdims=True).astype(jnp.int32)
    rank2 = jnp.sum(jnp.where(lane == i2, before, 0.0), -1, keepdims=True).astype(jnp.int32)
    base_scr[...] = base_scr[...] + jnp.sum(oh.astype(F32), 0, keepdims=True)
    cnt_ref[...] = base_scr[...]

    ri_ref[...] = jnp.where(lane == 0, i1, jnp.where(lane == 1, i2, jnp.where(lane == 2, rank1,
                            jnp.where(lane == 3, rank2, 0))))
    rg_ref[...] = jnp.where(lane == 0, gate1, jnp.where(lane == 1, gate2, 0.0))


def _merge(prompt, n_total, row0, acts, weights, base, prev, tm, tiles_per_seq=1):
    n = acts[0].shape[0]
    assert n % tm == 0 and row0 % tm == 0
    t0 = row0 // tm
    row = lambda i: (i, 0)
    whole = lambda i: (0, 0)
    any_spec = pl.BlockSpec(memory_space=pl.ANY)
    if prompt:
        x2d, u, o0, o1, o2, l0, l1, l2, gate = acts
        hpt = tm // HALO
        halo_spec = pl.BlockSpec((HALO, CONV_CH), lambda i: (jnp.maximum(i * hpt - 1, 0), 0))
        act_args = [x2d, u, u, o0, o1, o2, l0, l1, l2, gate]
        act_specs = [pl.BlockSpec((tm, D_MODEL), row), pl.BlockSpec((tm, CONV_CH), row), halo_spec] + \
                    [pl.BlockSpec((tm, GROUP_W), row)] * 6 + [pl.BlockSpec((tm, 2 * D_MODEL), row)]
        scratch = [pltpu.VMEM((1, LANES), F32), pltpu.VMEM((HALO + tm, CONV_CH), F32)]
    else:
        x2d, conv, attn, gate = acts
        act_args = [x2d, conv, attn, gate]
        act_specs = [pl.BlockSpec((tm, D_MODEL), row), pl.BlockSpec((tm, CONV_CH), row),
                     pl.BlockSpec((tm, GROUP_W), row), pl.BlockSpec((tm, 2 * D_MODEL), row)]
        scratch = [pltpu.VMEM((1, LANES), F32)]
        weights = weights[2:]
    w_specs = [pl.BlockSpec(w.shape, whole) for w in weights]
    out_shape = [jax.ShapeDtypeStruct((n_total, ROW_SUB, LANES), F32), jax.ShapeDtypeStruct((n_total, LANES), jnp.int32),
                 jax.ShapeDtypeStruct((n_total, LANES), F32), jax.ShapeDtypeStruct((1, LANES), F32)]
    if prev is None:
        prev = [jnp.zeros((8, LANES), F32)] * 3
        aliases = {}
    else:
        first_prev = len(act_args) + len(weights) + 1
        aliases = {first_prev: 0, first_prev + 1: 1, first_prev + 2: 2}
    orow = lambda i: (i + t0, 0)
    return pl.pallas_call(
        functools.partial(_merge_kernel, prompt=prompt, tiles_per_seq=tiles_per_seq),
        grid=(n // tm,),
        in_specs=act_specs + w_specs + [pl.BlockSpec((1, LANES), whole)] + [any_spec] * 3,
        out_specs=[pl.BlockSpec((tm, ROW_SUB, LANES), lambda i: (i + t0, 0, 0)), pl.BlockSpec((tm, LANES), orow),
                   pl.BlockSpec((tm, LANES), orow),
                   pl.BlockSpec((1, LANES), whole)],
        out_shape=out_shape,
        scratch_shapes=scratch,
        input_output_aliases=aliases,
        compiler_params=_cparams(("arbitrary",)),
        name="merge_prompt" if prompt else "merge_sample",
    )(*act_args, *weights, base, *prev)


def _dispatch_kernel(bounds_ref, route_ref, h_ref, xg_any, zero_scr, sem_z, sem_r):
    i = pl.program_id(0)
    tm = h_ref.shape[0]

    def tail_copy(e):
        return pltpu.make_async_copy(zero_scr, xg_any.at[pl.ds(bounds_ref[e + 1] - CHUNK, CHUNK)], sem_z)

    @pl.when(i == 0)
    def _():
        zero_scr[...] = jnp.zeros_like(zero_scr)
        for e in range(N_EXPERTS):
            @pl.when(bounds_ref[e + 1] > bounds_ref[e])
            def _():
                tail_copy(e).start()
        for e in range(N_EXPERTS):
            @pl.when(bounds_ref[e + 1] > bounds_ref[e])
            def _():
                tail_copy(e).wait()

    def row_copy(t, k):
        dest = bounds_ref[route_ref[0, 0, 4 * t + k]] + route_ref[0, 0, 4 * t + 2 + k]
        return pltpu.make_async_copy(h_ref.at[t], xg_any.at[dest], sem_r)

    def issue(t, carry):
        row_copy(t, 0).start()
        row_copy(t, 1).start()
        return carry

    lax.fori_loop(0, tm, issue, 0)

    def drain(t, carry):
        row_copy(t, 0).wait()
        row_copy(t, 1).wait()
        return carry

    lax.fori_loop(0, tm, drain, 0)


def _dispatch(bounds, route_flat, h_all, n_rows, tm):
    n = h_all.shape[0]
    return pl.pallas_call(
        _dispatch_kernel,
        grid_spec=pltpu.PrefetchScalarGridSpec(
            num_scalar_prefetch=1,
            grid=(n // tm,),
            in_specs=[pl.BlockSpec((1, 1, 4 * tm), lambda i, b: (i, 0, 0), memory_space=pltpu.SMEM),
                      pl.BlockSpec((tm, ROW_SUB, LANES), lambda i, b: (i, 0, 0))],
            out_specs=pl.BlockSpec(memory_space=pl.ANY),
            scratch_shapes=[pltpu.VMEM((CHUNK, ROW_SUB, LANES), F32), pltpu.SemaphoreType.DMA(()),
                            pltpu.SemaphoreType.DMA(())],
        ),
        out_shape=jax.ShapeDtypeStruct((n_rows, ROW_SUB, LANES), F32),
        compiler_params=_cparams(("arbitrary",)),
        name="moe_dispatch",
    )(bounds, route_flat, h_all)


def _experts_kernel(ce_ref, nv_ref, x_ref, wg_ref, wu_ref, wd_ref, y_ref, wg_scr, wu_scr, wd_scr):
    c = pl.program_id(0)
    changed = (c == 0) | (ce_ref[c] != ce_ref[jnp.maximum(c - 1, 0)])

    @pl.when(changed)
    def _():
        wg_scr[...] = wg_ref[0].astype(BF16)
        wu_scr[...] = wu_ref[0].astype(BF16)
        wd_scr[...] = wd_ref[0].astype(BF16)

    @pl.when(c < nv_ref[0])
    def _():
        x = _load_tiles_as_rows(x_ref).astype(BF16)
        a = jnp.dot(x, wg_scr[...], preferred_element_type=F32)
        b = jnp.dot(x, wu_scr[...], preferred_element_type=F32)
        _store_rows_as_tiles(y_ref, jnp.dot((jax.nn.silu(a) * b).astype(BF16), wd_scr[...], preferred_element_type=F32))


def _experts(chunk_expert, n_valid, xg, w_eg, w_eu, w_ed):
    n_chunks = xg.shape[0] // CHUNK
    rows = lambda c, ce, nv: (jnp.minimum(c, nv[0] - 1), 0, 0)
    wsel = lambda c, ce, nv: (ce[c], 0, 0)
    return pl.pallas_call(
        _experts_kernel,
        grid_spec=pltpu.PrefetchScalarGridSpec(
            num_scalar_prefetch=2,
            grid=(n_chunks,),
            in_specs=[pl.BlockSpec((CHUNK, ROW_SUB, LANES), rows),
                      pl.BlockSpec((1, D_MODEL, D_EXPERT), wsel), pl.BlockSpec((1, D_MODEL, D_EXPERT), wsel),
                      pl.BlockSpec((1, D_EXPERT, D_MODEL), wsel)],
            out_specs=pl.BlockSpec((CHUNK, ROW_SUB, LANES), rows),
            scratch_shapes=[pltpu.VMEM((D_MODEL, D_EXPERT), BF16), pltpu.VMEM((D_MODEL, D_EXPERT), BF16),
                            pltpu.VMEM((D_EXPERT, D_MODEL), BF16)],
        ),
        out_shape=jax.ShapeDtypeStruct(xg.shape, F32),
        compiler_params=_cparams(("arbitrary",)),
        name="moe_experts",
    )(chunk_expert, n_valid, xg, w_eg, w_eu, w_ed)


def _combine_kernel(bounds_ref, route_ref, h_ref, rg_ref, g2_ref, b2_ref, yg_any, yp_ref, ys_ref, y1_scr, y2_scr, sem,
                    *, prompt_tiles):
    i = pl.program_id(0)
    tm = h_ref.shape[0]
    bufs = (y1_scr, y2_scr)

    def row_copy(t, k):
        src = bounds_ref[route_ref[0, 0, 4 * t + k]] + route_ref[0, 0, 4 * t + 2 + k]
        return pltpu.make_async_copy(yg_any.at[src], bufs[k].at[t], sem)

    def issue(t, carry):
        row_copy(t, 0).start()
        row_copy(t, 1).start()
        return carry

    lax.fori_loop(0, tm, issue, 0)

    def drain(t, carry):
        row_copy(t, 0).wait()
        row_copy(t, 1).wait()
        return carry

    lax.fori_loop(0, tm, drain, 0)

    moe = _load_tiles_as_rows(y1_scr) * rg_ref[:, 0:1] + _load_tiles_as_rows(y2_scr) * rg_ref[:, 1:2]
    out = _layer_norm(DN_ALPHA * _load_tiles_as_rows(h_ref) + moe, g2_ref[...], b2_ref[...])

    @pl.when(i < prompt_tiles)
    def _():
        yp_ref[...] = out

    @pl.when(i >= prompt_tiles)
    def _():
        ys_ref[...] = out


def _combine(bounds, route_flat, h_all, route_g, ln2_g, ln2_b, yg, n_prompt, tm):
    n = h_all.shape[0]
    pt = n_prompt // tm
    row = lambda i, b: (i, 0)
    whole = lambda i, b: (0, 0)
    return pl.pallas_call(
        functools.partial(_combine_kernel, prompt_tiles=pt),
        grid_spec=pltpu.PrefetchScalarGridSpec(
            num_scalar_prefetch=1,
            grid=(n // tm,),
            in_specs=[pl.BlockSpec((1, 1, 4 * tm), lambda i, b: (i, 0, 0), memory_space=pltpu.SMEM),
                      pl.BlockSpec((tm, ROW_SUB, LANES), lambda i, b: (i, 0, 0)), pl.BlockSpec((tm, LANES), row),
                      pl.BlockSpec((1, D_MODEL), whole), pl.BlockSpec((1, D_MODEL), whole),
                      pl.BlockSpec(memory_space=pl.ANY)],
            out_specs=[pl.BlockSpec((tm, D_MODEL), lambda i, b: (jnp.minimum(i, pt - 1), 0)),
                       pl.BlockSpec((tm, D_MODEL), lambda i, b: (jnp.maximum(i - pt, 0), 0))],
            scratch_shapes=[pltpu.VMEM((tm, ROW_SUB, LANES), F32), pltpu.VMEM((tm, ROW_SUB, LANES), F32),
                            pltpu.SemaphoreType.DMA(())],
        ),
        out_shape=[jax.ShapeDtypeStruct((n_prompt, D_MODEL), F32), jax.ShapeDtypeStruct((n - n_prompt, D_MODEL), F32)],
        compiler_params=_cparams(("arbitrary",)),
        name="moe_combine",
    )(bounds, route_flat, h_all, route_g, ln2_g, ln2_b, yg)


def kernel(x_prompt, x_sample, cache_kv_w128, cache_kv_w512, cache_kv_w2048, state_conv, w_in, b_in, w_dw, b_dw, conv_ln_g, conv_ln_b, w_conv_out, b_conv_out, w_attn_out, w_o, ln1_g, ln1_b, w_router_group, b_router_group, w_router_expert, b_router_expert, w_expert_gate, w_expert_up, w_expert_down, ln2_g, ln2_b):
    assert w_in.shape[0] == DEPTH == 1
    bsz, seq, _ = x_prompt.shape
    db, t_new, _ = x_sample.shape
    caches = (cache_kv_w128, cache_kv_w512, cache_kv_w2048)
    past = cache_kv_w2048.shape[2]
    n_p, n_s = bsz * seq, db * t_new
    n_all = n_p + n_s
    tm = ROW_TILE
    kvs = (HEADS_PER_GROUP, HEAD_DIM)

    w_bf = w_in[0].astype(BF16)
    xp2, xs2 = x_prompt.reshape(n_p, D_MODEL), x_sample.reshape(n_s, D_MODEL)
    qkv_p, kvf_p, u_p, gate_p = _in_proj(xp2, w_bf, b_in, _rotary_tables(jnp.arange(seq)), tm)
    pos_s = past + jnp.arange(n_s) % t_new
    qkv_s, kvf_s, u_s, gate_s = _in_proj(xs2, w_bf, b_in, _rotary_tables(pos_s), tm)

    qkv3 = qkv_p.reshape(bsz, seq, 3 * ATTN_WIDTH)
    og = [_prompt_attn(qkv3, g, dil) for g, (_, dil) in enumerate(ATTN_PATTERNS)]

    q3 = qkv_s.reshape(db, t_new, N_GROUPS, 3, GROUP_W)[:, :, :, 0].astype(F32).reshape(db, t_new, ATTN_WIDTH)
    kv_new = kvf_s.reshape(db, t_new, N_GROUPS, KV_ROWS)
    caches_t = [jnp.transpose(c[0], (0, 2, 3, 4, 1)).reshape(db, KV_ROWS, c.shape[2]) for c in caches]
    news_t = [jnp.pad(jnp.transpose(kv_new[:, :, g], (0, 2, 1)), ((0, 0), (0, 0), (0, LANES - t_new)))
              for g in range(N_GROUPS)]
    attn_s, kv_sample_t = _sample_attn(q3, caches_t, news_t)
    kv_sample = [jnp.transpose(o.reshape(db, 2, *kvs, o.shape[2]), (0, 4, 1, 2, 3))[None] for o in kv_sample_t]

    conv_s, conv_state_s = _sample_conv(state_conv[0], u_s.reshape(db, t_new, CONV_CH), w_dw[0], b_dw)

    w_r = jnp.zeros((D_MODEL, ROUTE_LANES), F32)
    w_r = w_r.at[:, 0:N_EXPERT_GROUPS].set(w_router_group[0]).at[:, LANES:LANES + N_EXPERTS].set(w_router_expert[0])
    b_r = jnp.zeros((1, ROUTE_LANES), F32)
    b_r = b_r.at[:, 0:N_EXPERT_GROUPS].set(b_router_group).at[:, LANES:LANES + N_EXPERTS].set(b_router_expert)
    weights = [w_dw[0], b_dw, conv_ln_g, conv_ln_b, w_conv_out[0].astype(BF16), b_conv_out, w_attn_out[0].astype(BF16),
               w_o[0].astype(BF16), ln1_g, ln1_b, w_r.astype(BF16), b_r]
    acts_p = [xp2, u_p, og[0][0], og[1][0], og[2][0], og[0][1], og[1][1], og[2][1], gate_p]
    h_all, route_i, route_g, counts_p = _merge(True, n_all, 0, acts_p, weights, jnp.zeros((1, LANES), F32), None, tm,
                                               tiles_per_seq=seq // tm)
    acts_s = [xs2, conv_s.reshape(n_s, CONV_CH), attn_s.reshape(n_s, GROUP_W), gate_s]
    h_all, route_i, route_g, counts = _merge(False, n_all, n_p, acts_s, weights, counts_p,
                                             (h_all, route_i, route_g), tm)

    cnt = counts[0, :N_EXPERTS].astype(jnp.int32)
    padded = ((cnt + CHUNK - 1) // CHUNK) * CHUNK
    pad_end = jnp.cumsum(padded)
    bounds = jnp.concatenate([jnp.zeros((1,), jnp.int32), pad_end]).astype(jnp.int32)
    n_chunks = (2 * n_all) // CHUNK + N_EXPERTS
    chunk_row0 = jnp.arange(n_chunks, dtype=jnp.int32) * CHUNK
    chunk_expert = jnp.minimum(jnp.sum((pad_end[None, :] <= chunk_row0[:, None]).astype(jnp.int32), axis=1),
                               N_EXPERTS - 1).astype(jnp.int32)
    n_valid = (pad_end[-1:] // CHUNK).astype(jnp.int32)
    route_flat = route_i[:, 0:4].reshape(n_all // tm, 1, 4 * tm)

    xg = _dispatch(bounds, route_flat, h_all, n_chunks * CHUNK, tm)
    yg = _experts(chunk_expert, n_valid, xg, w_expert_gate[0], w_expert_up[0], w_expert_down[0])
    y_p, y_s = _combine(bounds, route_flat, h_all, route_g, ln2_g, ln2_b, yg, n_p, tm)

    kvp = kvf_p.reshape(1, bsz, seq, N_GROUPS, 2, *kvs)
    kv_prompt = [kvp[:, :, seq - min(w, seq):, g] for g, (w, _) in enumerate(ATTN_PATTERNS)]
    conv_prompt = u_p.reshape(1, bsz, seq, CONV_CH)[:, :, seq - (CONV_WIDTH - 1):]
    return (y_p.reshape(bsz, seq, D_MODEL), y_s.reshape(db, t_new, D_MODEL),
            kv_prompt[0], kv_prompt[1], kv_prompt[2], conv_prompt,
            kv_sample[0], kv_sample[1], kv_sample[2], conv_state_s[None])
```

```python
import functools

import jax
import jax.numpy as jnp
from jax import lax
from jax.experimental import pallas as pl
from jax.experimental.pallas import tpu as pltpu

F32 = jnp.float32
BF16 = jnp.bfloat16
I32 = jnp.int32

D_MODEL = 1024
HEAD_DIM = 64
HEADS_PER_GROUP = 4
GROUP_W = HEADS_PER_GROUP * HEAD_DIM
ATTN_PATTERNS = ((128, 1), (512, 4), (2048, 16))
N_GROUPS = len(ATTN_PATTERNS)
ATTN_WIDTH = N_GROUPS * GROUP_W
QKV_W = 3 * GROUP_W
STEPS = 128
ROT_HALF = 8
ROPE_THETA = 500000.0
CONV_CH = 512
CONV_WIDTH = 31
HALO = 32
N_EXPERTS = 32
N_EXPERT_GROUPS = 4
D_EXPERT = 512
DEPTH = 1
DN_ALPHA = (2 * DEPTH) ** 0.25
LN_EPS = 1e-5
SCALE = HEAD_DIM ** -0.5
NEG_INF = -1e30
LANES = 128
ROW_SUB = D_MODEL // LANES
KV_ROWS = 2 * GROUP_W
ROW_TILE = 256
CHUNK = 256
VMEM_LIMIT = 56 * 1024 * 1024

_Q0, _K0, _V0 = 0, ATTN_WIDTH, 2 * ATTN_WIDTH
_UA0 = 3 * ATTN_WIDTH
_UB0 = _UA0 + CONV_CH
_G0 = _UB0 + CONV_CH
IN_WIDTH = _G0 + 2 * D_MODEL


def _cparams(sem):
    return pltpu.CompilerParams(dimension_semantics=sem, vmem_limit_bytes=VMEM_LIMIT)


def _layer_norm(x, g, b):
    mu = jnp.mean(x, -1, keepdims=True)
    xc = x - mu
    var = jnp.mean(xc * xc, -1, keepdims=True)
    return xc * lax.rsqrt(var + LN_EPS) * g + b


def _head_masks(shape, lane_axis):
    lane = lax.broadcasted_iota(I32, shape, lane_axis)
    return [(lane >= h * HEAD_DIM) & (lane < (h + 1) * HEAD_DIM) for h in range(HEADS_PER_GROUP)]


def _store_rows_as_tiles(ref3, val2d, tmp):
    sub = 8
    groups = val2d.shape[0] // sub
    for c in range(ROW_SUB):
        tmp[:, c * sub:(c + 1) * sub, :] = val2d[:, c * LANES:(c + 1) * LANES].reshape(groups, sub, LANES)
    for s in range(sub):
        ref3[pl.ds(s, groups, stride=sub), :, :] = tmp[:, pl.ds(s, ROW_SUB, stride=sub), :]


def _load_tiles_as_rows(ref3, tmp):
    sub = 8
    groups = ref3.shape[0] // sub
    for s in range(sub):
        tmp[:, pl.ds(s, ROW_SUB, stride=sub), :] = ref3[pl.ds(s, groups, stride=sub), :, :]
    return jnp.concatenate([tmp[:, c * sub:(c + 1) * sub, :].reshape(groups * sub, LANES) for c in range(ROW_SUB)],
                           axis=1)


def _rotary_tables(pos):
    inv_freq = ROPE_THETA ** (-jnp.arange(ROT_HALF, dtype=F32) / ROT_HALF)
    ang = pos.astype(F32)[:, None] * inv_freq[None, :]
    cos, sin = jnp.cos(ang), jnp.sin(ang)
    zero = jnp.zeros_like(cos)
    p = pos.shape[0]
    rest = HEAD_DIM - 2 * ROT_HALF
    c_head = jnp.concatenate([cos, cos, jnp.ones((p, rest), F32)], -1)
    sa_head = jnp.concatenate([zero, sin, jnp.zeros((p, rest), F32)], -1)
    sb_head = jnp.concatenate([-sin, zero, jnp.zeros((p, rest), F32)], -1)
    rep = LANES // HEAD_DIM
    return (jnp.tile(c_head, (1, rep)), jnp.tile(sa_head, (1, rep)), jnp.tile(sb_head, (1, rep)))


def _in_proj_kernel(*refs, prompt, tiles_per_seq):
    if prompt:
        (x_ref, w_ref, b_ref, c_ref, sa_ref, sb_ref, wdw_ref, bdw_ref,
         q0_ref, q1_ref, q2_ref, kvf_ref, u_ref, conv_ref, gate_ref, ext_scr, sh_scr, rm_scr) = refs
    else:
        (x_ref, w_ref, b_ref, c_ref, sa_ref, sb_ref, qs_ref, kvf_ref, u_ref, gate_ref) = refs
    tm = x_ref.shape[0]
    if prompt:
        @pl.when(pl.program_id(0) % tiles_per_seq == 0)
        def _():
            ext_scr[0:HALO, :] = jnp.zeros((HALO, CONV_CH), F32)

    xb = x_ref[...].astype(BF16)

    def mm(c0, c1):
        return jnp.dot(xb, w_ref[:, c0:c1], preferred_element_type=F32) + b_ref[:, c0:c1]

    u = mm(_UA0, _UB0) * jax.nn.sigmoid(mm(_UB0, _G0))
    u_ref[...] = u
    if prompt:
        ext_scr[HALO:HALO + tm, :] = u
        off = HALO - (CONV_WIDTH - 1)
        sub = 8
        conv = jnp.broadcast_to(bdw_ref[...], (tm, CONV_CH))
        for s in range(sub):
            n_a = (CONV_WIDTH - s + sub - 1) // sub
            rows = tm + sub * (n_a - 1)
            sh_scr[0:rows, :] = ext_scr[off + s:off + s + rows, :]
            for a in range(n_a):
                j = sub * a + s
                conv = conv + sh_scr[sub * a:sub * a + tm, :] * wdw_ref[j:j + 1, :]
        conv_ref[...] = conv
        ext_scr[0:HALO, :] = ext_scr[tm:tm + HALO, :]

    c, sa, sb = c_ref[...], sa_ref[...], sb_ref[...]

    def rotary(z):
        parts = []
        for j in range(z.shape[1] // LANES):
            zc = z[:, j * LANES:(j + 1) * LANES]
            parts.append(zc * c + pltpu.roll(zc, ROT_HALF, 1) * sa + pltpu.roll(zc, LANES - ROT_HALF, 1) * sb)
        return jnp.concatenate(parts, axis=1)

    q = rotary(mm(_Q0, _K0)) * SCALE
    k = rotary(mm(_K0, _V0))
    v = mm(_V0, _UA0)
    for g in range(N_GROUPS):
        gs = slice(g * GROUP_W, (g + 1) * GROUP_W)
        kvf_ref[:, 2 * g * GROUP_W:(2 * g + 1) * GROUP_W] = k[:, gs]
        kvf_ref[:, (2 * g + 1) * GROUP_W:(2 * g + 2) * GROUP_W] = v[:, gs]
    gate_ref[...] = jax.nn.sigmoid(mm(_G0, IN_WIDTH))

    if not prompt:
        qs_ref[...] = q
        return

    for g, out_ref in enumerate((q0_ref, q1_ref, q2_ref)):
        dil = ATTN_PATTERNS[g][1]
        gs = slice(g * GROUP_W, (g + 1) * GROUP_W)
        qkv = jnp.concatenate([q[:, gs], k[:, gs], v[:, gs]], axis=1)
        if dil == 1:
            out_ref[...] = qkv.astype(BF16)
        else:
            nck = QKV_W // LANES
            for ck in range(nck):
                rm_scr[ck] = qkv[:, ck * LANES:(ck + 1) * LANES]
            for r in range(dil):
                blk = jnp.concatenate([rm_scr[ck, pl.ds(r, tm // dil, stride=dil), :] for ck in range(nck)], axis=1)
                out_ref[:, r * QKV_W:(r + 1) * QKV_W] = blk.astype(BF16)


def _in_proj(x2d, w_bf, b2d, tables, tm, conv_w=None, tiles_per_seq=1):
    n = x2d.shape[0]
    p = tables[0].shape[0]
    prompt = conv_w is not None
    assert n % tm == 0 and p % tm == 0
    tpb = p // tm
    row = lambda i: (i, 0)
    tab = lambda i: (i % tpb, 0)
    whole = lambda i: (0, 0)
    in_specs = [pl.BlockSpec((tm, D_MODEL), row), pl.BlockSpec((D_MODEL, IN_WIDTH), whole),
                pl.BlockSpec((1, IN_WIDTH), whole)] + [pl.BlockSpec((tm, LANES), tab)] * 3
    common_specs = [pl.BlockSpec((tm, 2 * ATTN_WIDTH), row), pl.BlockSpec((tm, CONV_CH), row)]
    common_shapes = [jax.ShapeDtypeStruct((n, 2 * ATTN_WIDTH), F32),
                     jax.ShapeDtypeStruct((n, CONV_CH), F32)]
    gate_spec, gate_shape = pl.BlockSpec((tm, 2 * D_MODEL), row), jax.ShapeDtypeStruct((n, 2 * D_MODEL), F32)
    if prompt:
        args = (x2d, w_bf, b2d, *tables, *conv_w)
        in_specs += [pl.BlockSpec((CONV_WIDTH, CONV_CH), whole), pl.BlockSpec((1, CONV_CH), whole)]
        q_specs = [pl.BlockSpec((tm // d, d * QKV_W), row) for _, d in ATTN_PATTERNS]
        q_shapes = [jax.ShapeDtypeStruct((n // d, d * QKV_W), BF16) for _, d in ATTN_PATTERNS]
        out_specs = q_specs + common_specs + [pl.BlockSpec((tm, CONV_CH), row), gate_spec]
        out_shape = q_shapes + common_shapes + [jax.ShapeDtypeStruct((n, CONV_CH), F32), gate_shape]
        scratch = [pltpu.VMEM((HALO + tm, CONV_CH), F32), pltpu.VMEM((HALO + tm, CONV_CH), F32),
                   pltpu.VMEM((QKV_W // LANES, tm, LANES), F32)]
    else:
        args = (x2d, w_bf, b2d, *tables)
        out_specs = [pl.BlockSpec((tm, ATTN_WIDTH), row)] + common_specs + [gate_spec]
        out_shape = [jax.ShapeDtypeStruct((n, ATTN_WIDTH), F32)] + common_shapes + [gate_shape]
        scratch = []
    return pl.pallas_call(
        functools.partial(_in_proj_kernel, prompt=prompt, tiles_per_seq=tiles_per_seq),
        grid=(n // tm,),
        in_specs=in_specs,
        out_specs=out_specs,
        out_shape=out_shape,
        scratch_shapes=scratch,
        compiler_params=_cparams(("arbitrary",)),
        name="in_proj_prompt" if prompt else "in_proj_sample",
    )(*args)


def _attn_block(q, keys, vals, first):
    masks = _head_masks(q.shape, 1)
    zero = jnp.zeros_like(q)
    q_stack = jnp.concatenate([jnp.where(m, q, zero) for m in masks], axis=0)
    s = lax.dot_general(q_stack, keys, (((1,), (1,)), ((), ())), preferred_element_type=F32)
    qi = lax.broadcasted_iota(I32, s.shape, 0) % STEPS
    kj = lax.broadcasted_iota(I32, s.shape, 1)
    ok = (kj <= qi) if first else ((kj >= qi) & (kj <= qi + STEPS))
    s = jnp.where(ok, s, NEG_INF)
    m = jnp.max(s, -1, keepdims=True)
    p = jnp.exp(s - m)
    l = jnp.sum(p, -1, keepdims=True)
    pv = jnp.dot(p.astype(BF16), vals, preferred_element_type=F32)
    lse = m + jnp.log(l)
    o = jnp.zeros((STEPS, GROUP_W), F32)
    lse_o = jnp.zeros((STEPS, GROUP_W), F32)
    for h, hm in enumerate(masks):
        rs = slice(h * STEPS, (h + 1) * STEPS)
        o = jnp.where(hm, pv[rs] / l[rs], o)
        lse_o = jnp.where(hm, lse[rs], lse_o)
    return o, lse_o


def _prompt_attn_kernel(q0_ref, q1_ref, q2_ref, attn_ref, o_rm1, l_rm1, o_rm2, l_rm2, o_nat, l_nat):
    halves = GROUP_W // LANES

    def put_nat(g, rows, o, lse):
        for hf in range(halves):
            hs = slice(hf * LANES, (hf + 1) * LANES)
            o_nat[g * halves + hf, rows, :] = o[:, hs]
            l_nat[g * halves + hf, rows, :] = lse[:, hs]

    for g, ref in enumerate((q0_ref, q1_ref, q2_ref)):
        dil = ATTN_PATTERNS[g][1]
        o_rm, l_rm = (None, o_rm1, o_rm2)[g], (None, l_rm1, l_rm2)[g]
        rows = ref.shape[0]
        nb = rows // STEPS
        for r in range(dil):
            base = r * QKV_W
            qc, kc, vc = (slice(base + j * GROUP_W, base + (j + 1) * GROUP_W) for j in range(3))
            oc = slice(r * GROUP_W, (r + 1) * GROUP_W)
            o, lse = _attn_block(ref[0:STEPS, qc], ref[0:STEPS, kc], ref[0:STEPS, vc], True)
            if dil == 1:
                put_nat(g, slice(0, STEPS), o, lse)
            else:
                o_rm[0:STEPS, oc] = o
                l_rm[0:STEPS, oc] = lse

            def body(n, carry, ref=ref, qc=qc, kc=kc, vc=vc, oc=oc, dil=dil, g=g, o_rm=o_rm, l_rm=l_rm):
                q0 = pl.multiple_of(n * STEPS, STEPS)
                k0 = pl.multiple_of((n - 1) * STEPS, STEPS)
                o, lse = _attn_block(ref[pl.ds(q0, STEPS), qc], ref[pl.ds(k0, 2 * STEPS), kc],
                                     ref[pl.ds(k0, 2 * STEPS), vc], False)
                if dil == 1:
                    put_nat(g, pl.ds(q0, STEPS), o, lse)
                else:
                    o_rm[pl.ds(q0, STEPS), oc] = o
                    l_rm[pl.ds(q0, STEPS), oc] = lse
                return carry

            if nb > 1:
                lax.fori_loop(1, nb, body, 0)
        if dil > 1:
            for r in range(dil):
                oc = slice(r * GROUP_W, (r + 1) * GROUP_W)
                put_nat(g, pl.ds(r, rows, stride=dil), o_rm[0:rows, oc], l_rm[0:rows, oc])

    def nat(ref, g):
        return jnp.concatenate([ref[g * halves + hf] for hf in range(halves)], axis=1)

    lses = [nat(l_nat, g) for g in range(N_GROUPS)]
    mx = jnp.maximum(jnp.maximum(lses[0], lses[1]), lses[2])
    ws = [jnp.exp(ls - mx) for ls in lses]
    attn_ref[...] = (nat(o_nat, 0) * ws[0] + nat(o_nat, 1) * ws[1] + nat(o_nat, 2) * ws[2]) / (ws[0] + ws[1] + ws[2])


def _prompt_attn(qkv_groups, bsz, seq):
    row = lambda b: (b, 0)
    in_specs = [pl.BlockSpec((seq // d, d * QKV_W), row) for _, d in ATTN_PATTERNS]
    return pl.pallas_call(
        _prompt_attn_kernel,
        grid=(bsz,),
        in_specs=in_specs,
        out_specs=pl.BlockSpec((seq, GROUP_W), row),
        out_shape=jax.ShapeDtypeStruct((bsz * seq, GROUP_W), F32),
        scratch_shapes=[pltpu.VMEM((seq // d, d * GROUP_W), F32) for _, d in ATTN_PATTERNS[1:] for _ in range(2)]
        + [pltpu.VMEM((N_GROUPS * (GROUP_W // LANES), seq, LANES), F32) for _ in range(2)],
        compiler_params=_cparams(("parallel",)),
        name="prompt_attn",
    )(*qkv_groups)


def _sample_attn_kernel(q_ref, c0_ref, c1_ref, c2_ref, n0_ref, n1_ref, n2_ref, attn_ref, o0_ref, o1_ref, o2_ref,
                        *, t_new):
    q = q_ref[0]
    outs, lses = [], []
    for g, (cref, nref, oref) in enumerate(((c0_ref, n0_ref, o0_ref), (c1_ref, n1_ref, o1_ref),
                                            (c2_ref, n2_ref, o2_ref))):
        dil = ATTN_PATTERNS[g][1]
        wb = cref.shape[2]
        full = jnp.concatenate([cref[0], nref[0]], axis=1)
        oref[0] = full[:, t_new:wb + t_new]
        k_t = full[0:GROUP_W].astype(BF16)
        v_t = full[GROUP_W:KV_ROWS].astype(BF16)
        qg = q[:, g * GROUP_W:(g + 1) * GROUP_W]
        masks = _head_masks(qg.shape, 1)
        q_stack = jnp.concatenate([jnp.where(m, qg, 0.0) for m in masks], axis=0).astype(BF16)
        s = jnp.dot(q_stack, k_t, preferred_element_type=F32)
        r = lax.broadcasted_iota(I32, s.shape, 1)
        t = lax.broadcasted_iota(I32, s.shape, 0) % t_new
        ok = (r >= t) & (r <= wb + t) & (((r - t) & (dil - 1)) == 0)
        s = jnp.where(ok, s, NEG_INF)
        m = jnp.max(s, -1, keepdims=True)
        p = jnp.exp(s - m)
        l = jnp.sum(p, -1, keepdims=True)
        pv = lax.dot_general(p.astype(BF16), v_t, (((1,), (1,)), ((), ())), preferred_element_type=F32)
        lse = m + jnp.log(l)
        o = jnp.zeros((t_new, GROUP_W), F32)
        lse_o = jnp.zeros((t_new, GROUP_W), F32)
        for h, hm in enumerate(masks):
            rs = slice(h * t_new, (h + 1) * t_new)
            o = jnp.where(hm, pv[rs] / l[rs], o)
            lse_o = jnp.where(hm, lse[rs], lse_o)
        outs.append(o)
        lses.append(lse_o)
    mx = jnp.maximum(jnp.maximum(lses[0], lses[1]), lses[2])
    ws = [jnp.exp(ls - mx) for ls in lses]
    attn_ref[0] = (outs[0] * ws[0] + outs[1] * ws[1] + outs[2] * ws[2]) / (ws[0] + ws[1] + ws[2])


def _sample_attn(q3, caches_t, news_t):
    db, t_new, _ = q3.shape
    for c, (_, dil) in zip(caches_t, ATTN_PATTERNS):
        assert c.shape[2] == STEPS * dil and t_new <= LANES and dil & (dil - 1) == 0
    blk = lambda b: (b, 0, 0)
    cache_specs = [pl.BlockSpec((1, KV_ROWS, c.shape[2]), blk) for c in caches_t]
    outs = pl.pallas_call(
        functools.partial(_sample_attn_kernel, t_new=t_new),
        grid=(db,),
        in_specs=[pl.BlockSpec((1, t_new, ATTN_WIDTH), blk)] + cache_specs + [pl.BlockSpec((1, KV_ROWS, LANES), blk)] * 3,
        out_specs=[pl.BlockSpec((1, t_new, GROUP_W), blk)] + cache_specs,
        out_shape=[jax.ShapeDtypeStruct((db, t_new, GROUP_W), F32)]
        + [jax.ShapeDtypeStruct(c.shape, c.dtype) for c in caches_t],
        compiler_params=_cparams(("parallel",)),
        name="sample_attn",
    )(q3, *caches_t, *news_t)
    return outs[0], outs[1:]


def _sample_conv_kernel(state_ref, u_ref, w_ref, b_ref, conv_ref, sout_ref):
    hist = state_ref.shape[1]
    t_new = u_ref.shape[1]
    bb = u_ref.shape[0]
    for t in range(t_new):
        acc = jnp.broadcast_to(b_ref[...], (bb, CONV_CH))
        for j in range(CONV_WIDTH):
            r = t + j
            row = state_ref[:, r, :] if r < hist else u_ref[:, r - hist, :]
            acc = acc + row * w_ref[j:j + 1, :]
        conv_ref[:, t, :] = acc
    sout_ref[:, 0:hist - t_new, :] = state_ref[:, t_new:hist, :]
    sout_ref[:, hist - t_new:hist, :] = u_ref[...]


def _sample_conv(state, u3, w_dw, b_dw, bb=32):
    db, hist, _ = state.shape
    t_new = u3.shape[1]
    assert db % bb == 0 and hist == CONV_WIDTH - 1
    blk = lambda i: (i, 0, 0)
    whole = lambda i: (0, 0)
    return pl.pallas_call(
        _sample_conv_kernel,
        grid=(db // bb,),
        in_specs=[pl.BlockSpec((bb, hist, CONV_CH), blk), pl.BlockSpec((bb, t_new, CONV_CH), blk),
                  pl.BlockSpec((CONV_WIDTH, CONV_CH), whole), pl.BlockSpec((1, CONV_CH), whole)],
        out_specs=[pl.BlockSpec((bb, t_new, CONV_CH), blk), pl.BlockSpec((bb, hist, CONV_CH), blk)],
        out_shape=[jax.ShapeDtypeStruct((db, t_new, CONV_CH), F32), jax.ShapeDtypeStruct((db, hist, CONV_CH), F32)],
        compiler_params=_cparams(("parallel",)),
        name="sample_conv",
    )(state, u3, w_dw, b_dw)


ROUTE_LANES = 2 * LANES


def _merge_kernel(xp_ref, xs_ref, cp_ref, cs_ref, ap_ref, as_ref, gp_ref, gs_ref, clg_ref, clb_ref, wco_ref, bco_ref,
                  wao_ref, wo_ref, g1_ref, b1_ref, wr_ref, br_ref, h_ref, ri_ref, rg_ref, cnt_ref, base_scr, tile_scr,
                  *, prompt_tiles):
    i = pl.program_id(0)
    tm = xp_ref.shape[0]
    is_p = i < prompt_tiles

    @pl.when(i == 0)
    def _():
        base_scr[...] = jnp.zeros_like(base_scr)

    x = jnp.where(is_p, xp_ref[...], xs_ref[...])
    conv = jnp.where(is_p, cp_ref[...], cs_ref[...])
    attn = jnp.where(is_p, ap_ref[...], as_ref[...])
    gate = jnp.where(is_p, gp_ref[...], gs_ref[...])

    cn = _layer_norm(conv, clg_ref[...], clb_ref[...])
    branch_c = jnp.dot(jax.nn.silu(cn).astype(BF16), wco_ref[...], preferred_element_type=F32) + bco_ref[...]
    branch_a = jnp.dot(attn.astype(BF16), wao_ref[...], preferred_element_type=F32)
    mixed = gate[:, 0:D_MODEL] * branch_a + gate[:, D_MODEL:2 * D_MODEL] * branch_c
    y = jnp.dot(mixed.astype(BF16), wo_ref[...], preferred_element_type=F32)
    h = _layer_norm(DN_ALPHA * x + y, g1_ref[...], b1_ref[...])
    _store_rows_as_tiles(h_ref, h, tile_scr)

    logits = jnp.dot(h.astype(BF16), wr_ref[...], preferred_element_type=F32) + br_ref[...]
    lane = lax.broadcasted_iota(I32, (tm, LANES), 1)
    big = jnp.int32(LANES)
    gl = jnp.where(lane < N_EXPERT_GROUPS, logits[:, 0:LANES], NEG_INF)
    gmax = jnp.max(gl, -1, keepdims=True)
    g_sel = jnp.min(jnp.where(gl == gmax, lane, big), -1, keepdims=True)
    p_g = 1.0 / jnp.sum(jnp.exp(gl - gmax), -1, keepdims=True)
    in_group = (lane < N_EXPERTS) & (lax.shift_right_logical(lane, 3) == g_sel)
    el = jnp.where(in_group, logits[:, LANES:2 * LANES], NEG_INF)
    v1 = jnp.max(el, -1, keepdims=True)
    i1 = jnp.min(jnp.where(el == v1, lane, big), -1, keepdims=True)
    el2 = jnp.where(lane == i1, NEG_INF, el)
    v2 = jnp.max(el2, -1, keepdims=True)
    i2 = jnp.min(jnp.where(el2 == v2, lane, big), -1, keepdims=True)
    e21 = jnp.exp(v2 - v1)
    gate1 = p_g / (1.0 + e21)
    gate2 = p_g * e21 / (1.0 + e21)

    oh = ((lane == i1) | (lane == i2)).astype(BF16)
    ri_ = lax.broadcasted_iota(I32, (tm, tm), 0)
    ci_ = lax.broadcasted_iota(I32, (tm, tm), 1)
    tri = (ci_ < ri_).astype(BF16)
    before = jnp.dot(tri, oh, preferred_element_type=F32) + base_scr[...]
    rank1 = jnp.sum(jnp.where(lane == i1, before, 0.0), -1, keepdims=True).astype(I32)
    rank2 = jnp.sum(jnp.where(lane == i2, before, 0.0), -1, keepdims=True).astype(I32)
    base_scr[...] = base_scr[...] + jnp.sum(oh.astype(F32), 0, keepdims=True)
    cnt_ref[...] = base_scr[...]

    ri_ref[...] = jnp.where(lane == 0, i1, jnp.where(lane == 1, i2, jnp.where(lane == 2, rank1,
                            jnp.where(lane == 3, rank2, 0))))
    rg_ref[...] = jnp.where(lane == 0, gate1, jnp.where(lane == 1, gate2, 0.0))


def _merge(acts_p, acts_s, weights, tm):
    n_p, n_s = acts_p[0].shape[0], acts_s[0].shape[0]
    assert n_p % tm == 0 and n_s % tm == 0
    pt = n_p // tm
    n_all = n_p + n_s
    prow = lambda i: (jnp.minimum(i, pt - 1), 0)
    srow = lambda i: (jnp.maximum(i - pt, 0), 0)
    row = lambda i: (i, 0)
    whole = lambda i: (0, 0)
    act_args, act_specs = [], []
    for a_p, a_s in zip(acts_p, acts_s):
        act_args += [a_p, a_s]
        act_specs += [pl.BlockSpec((tm, a_p.shape[1]), prow), pl.BlockSpec((tm, a_s.shape[1]), srow)]
    return pl.pallas_call(
        functools.partial(_merge_kernel, prompt_tiles=pt),
        grid=(n_all // tm,),
        in_specs=act_specs + [pl.BlockSpec(w.shape, whole) for w in weights],
        out_specs=[pl.BlockSpec((tm, ROW_SUB, LANES), lambda i: (i, 0, 0)), pl.BlockSpec((tm, LANES), row),
                   pl.BlockSpec((tm, LANES), row), pl.BlockSpec((1, LANES), whole)],
        out_shape=[jax.ShapeDtypeStruct((n_all, ROW_SUB, LANES), F32), jax.ShapeDtypeStruct((n_all, LANES), I32),
                   jax.ShapeDtypeStruct((n_all, LANES), F32), jax.ShapeDtypeStruct((1, LANES), F32)],
        scratch_shapes=[pltpu.VMEM((1, LANES), F32), pltpu.VMEM((tm // 8, 8 * ROW_SUB, LANES), F32)],
        compiler_params=_cparams(("arbitrary",)),
        name="merge",
    )(*act_args, *weights)


def _route_invert_kernel(bounds_ref, cnt_ref, route_ref, inv_ref, *, n_all):
    i = pl.program_id(0)
    tm = route_ref.shape[2] // 4

    @pl.when(i == 0)
    def _():
        def fill(j, carry):
            inv_ref[j] = jnp.int32(-1)
            return carry

        for e in range(N_EXPERTS):
            lax.fori_loop(bounds_ref[e] + cnt_ref[e], bounds_ref[e + 1], fill, 0)
        lax.fori_loop(bounds_ref[N_EXPERTS], inv_ref.shape[0], fill, 0)

    def body(t, carry):
        tok = i * tm + t
        for k in range(2):
            row = bounds_ref[route_ref[0, 0, 4 * t + k]] + route_ref[0, 0, 4 * t + 2 + k]
            inv_ref[row] = k * n_all + tok
        return carry

    lax.fori_loop(0, tm, body, 0)


def _route_invert(bounds, cnt, route_flat, n_rows, n_all):
    tm = route_flat.shape[2] // 4
    return pl.pallas_call(
        functools.partial(_route_invert_kernel, n_all=n_all),
        grid_spec=pltpu.PrefetchScalarGridSpec(
            num_scalar_prefetch=2,
            grid=(n_all // tm,),
            in_specs=[pl.BlockSpec((1, 1, 4 * tm), lambda i, b, c: (i, 0, 0), memory_space=pltpu.SMEM)],
            out_specs=pl.BlockSpec(memory_space=pltpu.SMEM),
        ),
        out_shape=jax.ShapeDtypeStruct((n_rows,), I32),
        compiler_params=_cparams(("arbitrary",)),
        name="route_invert",
    )(bounds, cnt, route_flat)


def _experts_kernel(ce_ref, nv_ref, gsrc_ref, gsrc_next_ref, sdst_ref, wg_ref, wu_ref, wd_ref, h_any, out_any,
                    wg_scr, wu_scr, wd_scr, xbuf, ybuf, tile_scr, gsem, ssem):
    c = pl.program_id(0)
    nv = nv_ref[0]
    slot = c % 2

    def start_gather(src_ref, sl):
        for r in range(CHUNK):
            pltpu.make_async_copy(h_any.at[src_ref[0, 0, r]], xbuf.at[sl, r], gsem.at[sl]).start()

    def wait_rows(sem):
        for _ in range(CHUNK):
            pltpu.make_async_copy(h_any.at[0], xbuf.at[0, 0], sem).wait()

    @pl.when(c == 0)
    def _():
        ybuf[...] = jnp.zeros_like(ybuf)
        spare0 = out_any.shape[0] - 2 * CHUNK
        fills = [pltpu.make_async_copy(ybuf.at[sl], out_any.at[pl.ds(spare0 + sl * CHUNK, CHUNK)], ssem.at[sl])
                 for sl in range(2)]
        for cp in fills:
            cp.start()
        for cp in fills:
            cp.wait()

    @pl.when((c == 0) & (nv > 0))
    def _():
        start_gather(gsrc_ref, 0)

    @pl.when(c + 1 < nv)
    def _():
        start_gather(gsrc_next_ref, 1 - slot)

    @pl.when((c < nv) & ((c == 0) | (ce_ref[c] != ce_ref[jnp.maximum(c - 1, 0)])))
    def _():
        wg_scr[...] = wg_ref[0].astype(BF16)
        wu_scr[...] = wu_ref[0].astype(BF16)
        wd_scr[...] = wd_ref[0].astype(BF16)

    @pl.when(c < nv)
    def _():
        wait_rows(gsem.at[slot])
        x = _load_tiles_as_rows(xbuf.at[slot], tile_scr).astype(BF16)
        a = jnp.dot(x, wg_scr[...], preferred_element_type=F32)
        b = jnp.dot(x, wu_scr[...], preferred_element_type=F32)
        y = jnp.dot((jax.nn.silu(a) * b).astype(BF16), wd_scr[...], preferred_element_type=F32)

        @pl.when(c >= 2)
        def _():
            wait_rows(ssem.at[slot])

        _store_rows_as_tiles(ybuf.at[slot], y, tile_scr)
        for r in range(CHUNK):
            pltpu.make_async_copy(ybuf.at[slot, r], out_any.at[sdst_ref[0, 0, r]], ssem.at[slot]).start()

    @pl.when(c == nv - 1)
    def _():
        wait_rows(ssem.at[slot])

        @pl.when(c >= 1)
        def _():
            wait_rows(ssem.at[1 - slot])


def _experts(chunk_expert, n_valid, gsrc3, sdst3, h3, n_slot_rows, w_eg, w_eu, w_ed):
    n_chunks = gsrc3.shape[0]
    last = lambda nv: jnp.maximum(nv[0] - 1, 0)
    cur = lambda c, ce, nv: (jnp.minimum(c, last(nv)), 0, 0)
    nxt = lambda c, ce, nv: (jnp.minimum(c + 1, last(nv)), 0, 0)
    wsel = lambda c, ce, nv: (ce[jnp.minimum(c, last(nv))], 0, 0)
    any_spec = pl.BlockSpec(memory_space=pl.ANY)
    return pl.pallas_call(
        _experts_kernel,
        grid_spec=pltpu.PrefetchScalarGridSpec(
            num_scalar_prefetch=2,
            grid=(n_chunks,),
            in_specs=[pl.BlockSpec((1, 1, CHUNK), cur, memory_space=pltpu.SMEM),
                      pl.BlockSpec((1, 1, CHUNK), nxt, memory_space=pltpu.SMEM),
                      pl.BlockSpec((1, 1, CHUNK), cur, memory_space=pltpu.SMEM),
                      pl.BlockSpec((1, D_MODEL, D_EXPERT), wsel), pl.BlockSpec((1, D_MODEL, D_EXPERT), wsel),
                      pl.BlockSpec((1, D_EXPERT, D_MODEL), wsel), any_spec],
            out_specs=any_spec,
            scratch_shapes=[pltpu.VMEM((D_MODEL, D_EXPERT), BF16), pltpu.VMEM((D_MODEL, D_EXPERT), BF16),
                            pltpu.VMEM((D_EXPERT, D_MODEL), BF16),
                            pltpu.VMEM((2, CHUNK, ROW_SUB, LANES), F32), pltpu.VMEM((2, CHUNK, ROW_SUB, LANES), F32),
                            pltpu.VMEM((CHUNK // 8, 8 * ROW_SUB, LANES), F32),
                            pltpu.SemaphoreType.DMA((2,)), pltpu.SemaphoreType.DMA((2,))],
        ),
        out_shape=jax.ShapeDtypeStruct((n_slot_rows, ROW_SUB, LANES), F32),
        compiler_params=_cparams(("arbitrary",)),
        name="moe_experts",
    )(chunk_expert, n_valid, gsrc3, gsrc3, sdst3, w_eg, w_eu, w_ed, h3)


def _combine_kernel(h_ref, y1_ref, y2_ref, rg_ref, g2_ref, b2_ref, yp_ref, ys_ref, t0_scr, t1_scr, t2_scr,
                    *, prompt_tiles):
    i = pl.program_id(0)
    moe = (_load_tiles_as_rows(y1_ref, t1_scr) * rg_ref[:, 0:1]
           + _load_tiles_as_rows(y2_ref, t2_scr) * rg_ref[:, 1:2])
    out = _layer_norm(DN_ALPHA * _load_tiles_as_rows(h_ref, t0_scr) + moe, g2_ref[...], b2_ref[...])

    @pl.when(i < prompt_tiles)
    def _():
        yp_ref[...] = out

    @pl.when(i >= prompt_tiles)
    def _():
        ys_ref[...] = out


def _combine(h3, y_slots, route_g, ln2_g, ln2_b, n_prompt, tm):
    n = h3.shape[0]
    pt, nt = n_prompt // tm, n // tm
    row = lambda i: (i, 0)
    whole = lambda i: (0, 0)
    tile = lambda i: (i, 0, 0)
    return pl.pallas_call(
        functools.partial(_combine_kernel, prompt_tiles=pt),
        grid=(nt,),
        in_specs=[pl.BlockSpec((tm, ROW_SUB, LANES), tile), pl.BlockSpec((tm, ROW_SUB, LANES), tile),
                  pl.BlockSpec((tm, ROW_SUB, LANES), lambda i: (i + nt, 0, 0)), pl.BlockSpec((tm, LANES), row),
                  pl.BlockSpec((1, D_MODEL), whole), pl.BlockSpec((1, D_MODEL), whole)],
        out_specs=[pl.BlockSpec((tm, D_MODEL), lambda i: (jnp.minimum(i, pt - 1), 0)),
                   pl.BlockSpec((tm, D_MODEL), lambda i: (jnp.maximum(i - pt, 0), 0))],
        out_shape=[jax.ShapeDtypeStruct((n_prompt, D_MODEL), F32), jax.ShapeDtypeStruct((n - n_prompt, D_MODEL), F32)],
        scratch_shapes=[pltpu.VMEM((tm // 8, 8 * ROW_SUB, LANES), F32) for _ in range(3)],
        compiler_params=_cparams(("arbitrary",)),
        name="moe_combine",
    )(h3, y_slots, y_slots, route_g, ln2_g, ln2_b)


def kernel(x_prompt, x_sample, cache_kv_w128, cache_kv_w512, cache_kv_w2048, state_conv, w_in, b_in, w_dw, b_dw, conv_ln_g, conv_ln_b, w_conv_out, b_conv_out, w_attn_out, w_o, ln1_g, ln1_b, w_router_group, b_router_group, w_router_expert, b_router_expert, w_expert_gate, w_expert_up, w_expert_down, ln2_g, ln2_b):
    assert w_in.shape[0] == DEPTH == 1
    bsz, seq, _ = x_prompt.shape
    db, t_new, _ = x_sample.shape
    caches = (cache_kv_w128, cache_kv_w512, cache_kv_w2048)
    past = cache_kv_w2048.shape[2]
    n_p, n_s = bsz * seq, db * t_new
    n_all = n_p + n_s
    tm = ROW_TILE
    kvs = (HEADS_PER_GROUP, HEAD_DIM)

    w_bf = w_in[0].astype(BF16)
    xp2, xs2 = x_prompt.reshape(n_p, D_MODEL), x_sample.reshape(n_s, D_MODEL)
    q0, q1, q2, kvf_p, u_p, conv_p, gate_p = _in_proj(xp2, w_bf, b_in, _rotary_tables(jnp.arange(seq)), tm,
                                                     conv_w=(w_dw[0], b_dw), tiles_per_seq=seq // tm)
    pos_s = past + jnp.arange(n_s) % t_new
    q_s, kvf_s, u_s, gate_s = _in_proj(xs2, w_bf, b_in, _rotary_tables(pos_s), tm)

    attn_p = _prompt_attn((q0, q1, q2), bsz, seq)

    kv_new = kvf_s.reshape(db, t_new, N_GROUPS, KV_ROWS)
    caches_t = [jnp.transpose(c[0], (0, 2, 3, 4, 1)).reshape(db, KV_ROWS, c.shape[2]) for c in caches]
    news_t = [jnp.pad(jnp.transpose(kv_new[:, :, g], (0, 2, 1)), ((0, 0), (0, 0), (0, LANES - t_new)))
              for g in range(N_GROUPS)]
    attn_s, kv_sample_t = _sample_attn(q_s.reshape(db, t_new, ATTN_WIDTH), caches_t, news_t)
    kv_sample = [jnp.transpose(o.reshape(db, 2, *kvs, o.shape[2]), (0, 4, 1, 2, 3))[None] for o in kv_sample_t]

    conv_s, conv_state_s = _sample_conv(state_conv[0], u_s.reshape(db, t_new, CONV_CH), w_dw[0], b_dw)

    w_r = jnp.zeros((D_MODEL, ROUTE_LANES), F32)
    w_r = w_r.at[:, 0:N_EXPERT_GROUPS].set(w_router_group[0]).at[:, LANES:LANES + N_EXPERTS].set(w_router_expert[0])
    b_r = jnp.zeros((1, ROUTE_LANES), F32)
    b_r = b_r.at[:, 0:N_EXPERT_GROUPS].set(b_router_group).at[:, LANES:LANES + N_EXPERTS].set(b_router_expert)
    weights = [conv_ln_g, conv_ln_b, w_conv_out[0].astype(BF16), b_conv_out, w_attn_out[0].astype(BF16),
               w_o[0].astype(BF16), ln1_g, ln1_b, w_r.astype(BF16), b_r]
    acts_p = (xp2, conv_p, attn_p, gate_p)
    acts_s = (xs2, conv_s.reshape(n_s, CONV_CH), attn_s.reshape(n_s, GROUP_W), gate_s)
    h3, route_i, route_g, counts = _merge(acts_p, acts_s, weights, tm)

    cnt = counts[0, :N_EXPERTS].astype(I32)
    padded = ((cnt + CHUNK - 1) // CHUNK) * CHUNK
    pad_end = jnp.cumsum(padded)
    bounds = jnp.concatenate([jnp.zeros((1,), I32), pad_end]).astype(I32)
    n_chunks = (2 * n_all) // CHUNK + N_EXPERTS
    chunk_row0 = jnp.arange(n_chunks, dtype=I32) * CHUNK
    chunk_expert = jnp.minimum(jnp.sum((pad_end[None, :] <= chunk_row0[:, None]).astype(I32), axis=1),
                               N_EXPERTS - 1).astype(I32)
    n_valid = (pad_end[-1:] // CHUNK).astype(I32)
    route_flat = route_i[:, 0:4].reshape(n_all // tm, 1, 4 * tm)

    inv = _route_invert(bounds, cnt, route_flat, n_chunks * CHUNK, n_all)
    spare = 2 * n_all + jnp.arange(n_chunks * CHUNK, dtype=I32) % (2 * CHUNK)
    gsrc = jnp.where(inv >= n_all, inv - n_all, jnp.maximum(inv, 0))
    sdst = jnp.where(inv >= 0, inv, spare)
    y_slots = _experts(chunk_expert, n_valid, gsrc.reshape(n_chunks, 1, CHUNK), sdst.reshape(n_chunks, 1, CHUNK), h3,
                       2 * n_all + 2 * CHUNK, w_expert_gate[0], w_expert_up[0], w_expert_down[0])
    y_p, y_s = _combine(h3, y_slots, route_g, ln2_g, ln2_b, n_p, tm)

    kvp = kvf_p.reshape(1, bsz, seq, N_GROUPS, 2, *kvs)
    kv_prompt = [kvp[:, :, seq - min(w, seq):, g] for g, (w, _) in enumerate(ATTN_PATTERNS)]
    conv_prompt = u_p.reshape(1, bsz, seq, CONV_CH)[:, :, seq - (CONV_WIDTH - 1):]
    return (y_p.reshape(bsz, seq, D_MODEL), y_s.reshape(db, t_new, D_MODEL),
            kv_prompt[0], kv_prompt[1], kv_prompt[2], conv_prompt,
            kv_sample[0], kv_sample[1], kv_sample[2], conv_state_s[None])
```

```python
import functools

import jax
import jax.numpy as jnp
from jax import lax
from jax.experimental import pallas as pl
from jax.experimental.pallas import tpu as pltpu

F32 = jnp.float32
BF16 = jnp.bfloat16
I32 = jnp.int32

D_MODEL = 1024
HEAD_DIM = 64
HEADS_PER_GROUP = 4
GROUP_W = HEADS_PER_GROUP * HEAD_DIM
ATTN_PATTERNS = ((128, 1), (512, 4), (2048, 16))
N_GROUPS = len(ATTN_PATTERNS)
ATTN_WIDTH = N_GROUPS * GROUP_W
QKV_W = 3 * GROUP_W
STEPS = 128
ROT_HALF = 8
ROPE_THETA = 500000.0
CONV_CH = 512
CONV_WIDTH = 31
HALO = 32
N_EXPERTS = 32
N_EXPERT_GROUPS = 4
D_EXPERT = 512
DEPTH = 1
DN_ALPHA = (2 * DEPTH) ** 0.25
LN_EPS = 1e-5
SCALE = HEAD_DIM ** -0.5
NEG_INF = -1e30
LANES = 128
ROW_SUB = D_MODEL // LANES
KV_ROWS = 2 * GROUP_W
ROW_TILE = 256
CHUNK = 256
VMEM_LIMIT = 56 * 1024 * 1024

_Q0, _K0, _V0 = 0, ATTN_WIDTH, 2 * ATTN_WIDTH
_UA0 = 3 * ATTN_WIDTH
_UB0 = _UA0 + CONV_CH
_G0 = _UB0 + CONV_CH
IN_WIDTH = _G0 + 2 * D_MODEL


def _cparams(sem):
    return pltpu.CompilerParams(dimension_semantics=sem, vmem_limit_bytes=VMEM_LIMIT)


def _layer_norm(x, g, b):
    mu = jnp.mean(x, -1, keepdims=True)
    xc = x - mu
    var = jnp.mean(xc * xc, -1, keepdims=True)
    return xc * lax.rsqrt(var + LN_EPS) * g + b


def _head_masks(shape, lane_axis):
    lane = lax.broadcasted_iota(I32, shape, lane_axis)
    return [(lane >= h * HEAD_DIM) & (lane < (h + 1) * HEAD_DIM) for h in range(HEADS_PER_GROUP)]


def _store_rows_as_tiles(ref3, val2d, tmp):
    sub = 8
    groups = val2d.shape[0] // sub
    for c in range(ROW_SUB):
        tmp[:, c * sub:(c + 1) * sub, :] = val2d[:, c * LANES:(c + 1) * LANES].reshape(groups, sub, LANES)
    for s in range(sub):
        ref3[pl.ds(s, groups, stride=sub), :, :] = tmp[:, pl.ds(s, ROW_SUB, stride=sub), :]


def _load_tiles_as_rows(ref3, tmp):
    sub = 8
    groups = ref3.shape[0] // sub
    for s in range(sub):
        tmp[:, pl.ds(s, ROW_SUB, stride=sub), :] = ref3[pl.ds(s, groups, stride=sub), :, :]
    return jnp.concatenate([tmp[:, c * sub:(c + 1) * sub, :].reshape(groups * sub, LANES) for c in range(ROW_SUB)],
                           axis=1)


def _rotary_tables(pos):
    inv_freq = ROPE_THETA ** (-jnp.arange(ROT_HALF, dtype=F32) / ROT_HALF)
    ang = pos.astype(F32)[:, None] * inv_freq[None, :]
    cos, sin = jnp.cos(ang), jnp.sin(ang)
    zero = jnp.zeros_like(cos)
    p = pos.shape[0]
    rest = HEAD_DIM - 2 * ROT_HALF
    c_head = jnp.concatenate([cos, cos, jnp.ones((p, rest), F32)], -1)
    sa_head = jnp.concatenate([zero, sin, jnp.zeros((p, rest), F32)], -1)
    sb_head = jnp.concatenate([-sin, zero, jnp.zeros((p, rest), F32)], -1)
    rep = LANES // HEAD_DIM
    return (jnp.tile(c_head, (1, rep)), jnp.tile(sa_head, (1, rep)), jnp.tile(sb_head, (1, rep)))


def _in_proj_kernel(*refs, prompt, tiles_per_seq):
    if prompt:
        (x_ref, w_ref, b_ref, c_ref, sa_ref, sb_ref, wdw_ref, bdw_ref,
         q0_ref, q1_ref, q2_ref, kvf_ref, u_ref, conv_ref, gate_ref, ext_scr, sh_scr, rm_scr) = refs
    else:
        (x_ref, w_ref, b_ref, c_ref, sa_ref, sb_ref, qs_ref, kvf_ref, u_ref, gate_ref) = refs
    tm = x_ref.shape[0]
    if prompt:
        @pl.when(pl.program_id(0) % tiles_per_seq == 0)
        def _():
            ext_scr[0:HALO, :] = jnp.zeros((HALO, CONV_CH), F32)

    xb = x_ref[...].astype(BF16)

    def mm(c0, c1):
        return jnp.dot(xb, w_ref[:, c0:c1], preferred_element_type=F32) + b_ref[:, c0:c1]

    u = mm(_UA0, _UB0) * jax.nn.sigmoid(mm(_UB0, _G0))
    u_ref[...] = u
    if prompt:
        ext_scr[HALO:HALO + tm, :] = u
        off = HALO - (CONV_WIDTH - 1)
        sub = 8
        conv = jnp.broadcast_to(bdw_ref[...], (tm, CONV_CH))
        for s in range(sub):
            n_a = (CONV_WIDTH - s + sub - 1) // sub
            rows = tm + sub * (n_a - 1)
            sh_scr[0:rows, :] = ext_scr[off + s:off + s + rows, :]
            for a in range(n_a):
                j = sub * a + s
                conv = conv + sh_scr[sub * a:sub * a + tm, :] * wdw_ref[j:j + 1, :]
        conv_ref[...] = conv
        ext_scr[0:HALO, :] = ext_scr[tm:tm + HALO, :]

    c, sa, sb = c_ref[...], sa_ref[...], sb_ref[...]

    def rotary(z):
        parts = []
        for j in range(z.shape[1] // LANES):
            zc = z[:, j * LANES:(j + 1) * LANES]
            parts.append(zc * c + pltpu.roll(zc, ROT_HALF, 1) * sa + pltpu.roll(zc, LANES - ROT_HALF, 1) * sb)
        return jnp.concatenate(parts, axis=1)

    q = rotary(mm(_Q0, _K0)) * SCALE
    k = rotary(mm(_K0, _V0))
    v = mm(_V0, _UA0)
    for g in range(N_GROUPS):
        gs = slice(g * GROUP_W, (g + 1) * GROUP_W)
        kvf_ref[:, 2 * g * GROUP_W:(2 * g + 1) * GROUP_W] = k[:, gs]
        kvf_ref[:, (2 * g + 1) * GROUP_W:(2 * g + 2) * GROUP_W] = v[:, gs]
    gate_ref[...] = jax.nn.sigmoid(mm(_G0, IN_WIDTH))

    if not prompt:
        qs_ref[...] = q
        return

    for g, out_ref in enumerate((q0_ref, q1_ref, q2_ref)):
        dil = ATTN_PATTERNS[g][1]
        gs = slice(g * GROUP_W, (g + 1) * GROUP_W)
        qkv = jnp.concatenate([q[:, gs], k[:, gs], v[:, gs]], axis=1)
        if dil == 1:
            out_ref[...] = qkv.astype(BF16)
        else:
            nck = QKV_W // LANES
            for ck in range(nck):
                rm_scr[ck] = qkv[:, ck * LANES:(ck + 1) * LANES]
            for r in range(dil):
                blk = jnp.concatenate([rm_scr[ck, pl.ds(r, tm // dil, stride=dil), :] for ck in range(nck)], axis=1)
                out_ref[:, r * QKV_W:(r + 1) * QKV_W] = blk.astype(BF16)


def _in_proj(x2d, w_bf, b2d, tables, tm, conv_w=None, tiles_per_seq=1):
    n = x2d.shape[0]
    p = tables[0].shape[0]
    prompt = conv_w is not None
    assert n % tm == 0 and p % tm == 0
    tpb = p // tm
    row = lambda i: (i, 0)
    tab = lambda i: (i % tpb, 0)
    whole = lambda i: (0, 0)
    in_specs = [pl.BlockSpec((tm, D_MODEL), row), pl.BlockSpec((D_MODEL, IN_WIDTH), whole),
                pl.BlockSpec((1, IN_WIDTH), whole)] + [pl.BlockSpec((tm, LANES), tab)] * 3
    common_specs = [pl.BlockSpec((tm, 2 * ATTN_WIDTH), row), pl.BlockSpec((tm, CONV_CH), row)]
    common_shapes = [jax.ShapeDtypeStruct((n, 2 * ATTN_WIDTH), F32),
                     jax.ShapeDtypeStruct((n, CONV_CH), F32)]
    gate_spec, gate_shape = pl.BlockSpec((tm, 2 * D_MODEL), row), jax.ShapeDtypeStruct((n, 2 * D_MODEL), F32)
    if prompt:
        args = (x2d, w_bf, b2d, *tables, *conv_w)
        in_specs += [pl.BlockSpec((CONV_WIDTH, CONV_CH), whole), pl.BlockSpec((1, CONV_CH), whole)]
        q_specs = [pl.BlockSpec((tm // d, d * QKV_W), row) for _, d in ATTN_PATTERNS]
        q_shapes = [jax.ShapeDtypeStruct((n // d, d * QKV_W), BF16) for _, d in ATTN_PATTERNS]
        out_specs = q_specs + common_specs + [pl.BlockSpec((tm, CONV_CH), row), gate_spec]
        out_shape = q_shapes + common_shapes + [jax.ShapeDtypeStruct((n, CONV_CH), F32), gate_shape]
        scratch = [pltpu.VMEM((HALO + tm, CONV_CH), F32), pltpu.VMEM((HALO + tm, CONV_CH), F32),
                   pltpu.VMEM((QKV_W // LANES, tm, LANES), F32)]
    else:
        args = (x2d, w_bf, b2d, *tables)
        out_specs = [pl.BlockSpec((tm, ATTN_WIDTH), row)] + common_specs + [gate_spec]
        out_shape = [jax.ShapeDtypeStruct((n, ATTN_WIDTH), F32)] + common_shapes + [gate_shape]
        scratch = []
    return pl.pallas_call(
        functools.partial(_in_proj_kernel, prompt=prompt, tiles_per_seq=tiles_per_seq),
        grid=(n // tm,),
        in_specs=in_specs,
        out_specs=out_specs,
        out_shape=out_shape,
        scratch_shapes=scratch,
        compiler_params=_cparams(("arbitrary",)),
        name="in_proj_prompt" if prompt else "in_proj_sample",
    )(*args)


def _attn_block(q, keys, vals, first):
    masks = _head_masks(q.shape, 1)
    zero = jnp.zeros_like(q)
    q_stack = jnp.concatenate([jnp.where(m, q, zero) for m in masks], axis=0)
    s = lax.dot_general(q_stack, keys, (((1,), (1,)), ((), ())), preferred_element_type=F32)
    qi = lax.broadcasted_iota(I32, s.shape, 0) % STEPS
    kj = lax.broadcasted_iota(I32, s.shape, 1)
    ok = (kj <= qi) if first else ((kj >= qi) & (kj <= qi + STEPS))
    s = jnp.where(ok, s, NEG_INF)
    m = jnp.max(s, -1, keepdims=True)
    p = jnp.exp(s - m)
    l = jnp.sum(p, -1, keepdims=True)
    pv = jnp.dot(p.astype(BF16), vals, preferred_element_type=F32)
    lse = m + jnp.log(l)
    o = jnp.zeros((STEPS, GROUP_W), F32)
    lse_o = jnp.zeros((STEPS, GROUP_W), F32)
    for h, hm in enumerate(masks):
        rs = slice(h * STEPS, (h + 1) * STEPS)
        o = jnp.where(hm, pv[rs] / l[rs], o)
        lse_o = jnp.where(hm, lse[rs], lse_o)
    return o, lse_o


def _prompt_attn_kernel(q0_ref, q1_ref, q2_ref, attn_ref, o_rm1, l_rm1, o_rm2, l_rm2, o_nat, l_nat):
    halves = GROUP_W // LANES

    def put_nat(g, rows, o, lse):
        for hf in range(halves):
            hs = slice(hf * LANES, (hf + 1) * LANES)
            o_nat[g * halves + hf, rows, :] = o[:, hs]
            l_nat[g * halves + hf, rows, :] = lse[:, hs]

    for g, ref in enumerate((q0_ref, q1_ref, q2_ref)):
        dil = ATTN_PATTERNS[g][1]
        o_rm, l_rm = (None, o_rm1, o_rm2)[g], (None, l_rm1, l_rm2)[g]
        rows = ref.shape[0]
        nb = rows // STEPS
        for r in range(dil):
            base = r * QKV_W
            qc, kc, vc = (slice(base + j * GROUP_W, base + (j + 1) * GROUP_W) for j in range(3))
            oc = slice(r * GROUP_W, (r + 1) * GROUP_W)
            o, lse = _attn_block(ref[0:STEPS, qc], ref[0:STEPS, kc], ref[0:STEPS, vc], True)
            if dil == 1:
                put_nat(g, slice(0, STEPS), o, lse)
            else:
                o_rm[0:STEPS, oc] = o
                l_rm[0:STEPS, oc] = lse

            def body(n, carry, ref=ref, qc=qc, kc=kc, vc=vc, oc=oc, dil=dil, g=g, o_rm=o_rm, l_rm=l_rm):
                q0 = pl.multiple_of(n * STEPS, STEPS)
                k0 = pl.multiple_of((n - 1) * STEPS, STEPS)
                o, lse = _attn_block(ref[pl.ds(q0, STEPS), qc], ref[pl.ds(k0, 2 * STEPS), kc],
                                     ref[pl.ds(k0, 2 * STEPS), vc], False)
                if dil == 1:
                    put_nat(g, pl.ds(q0, STEPS), o, lse)
                else:
                    o_rm[pl.ds(q0, STEPS), oc] = o
                    l_rm[pl.ds(q0, STEPS), oc] = lse
                return carry

            if nb > 1:
                lax.fori_loop(1, nb, body, 0)
        if dil > 1:
            for r in range(dil):
                oc = slice(r * GROUP_W, (r + 1) * GROUP_W)
                put_nat(g, pl.ds(r, rows, stride=dil), o_rm[0:rows, oc], l_rm[0:rows, oc])

    def nat(ref, g):
        return jnp.concatenate([ref[g * halves + hf] for hf in range(halves)], axis=1)

    lses = [nat(l_nat, g) for g in range(N_GROUPS)]
    mx = jnp.maximum(jnp.maximum(lses[0], lses[1]), lses[2])
    ws = [jnp.exp(ls - mx) for ls in lses]
    attn_ref[...] = (nat(o_nat, 0) * ws[0] + nat(o_nat, 1) * ws[1] + nat(o_nat, 2) * ws[2]) / (ws[0] + ws[1] + ws[2])


def _prompt_attn(qkv_groups, bsz, seq):
    row = lambda b: (b, 0)
    in_specs = [pl.BlockSpec((seq // d, d * QKV_W), row) for _, d in ATTN_PATTERNS]
    return pl.pallas_call(
        _prompt_attn_kernel,
        grid=(bsz,),
        in_specs=in_specs,
        out_specs=pl.BlockSpec((seq, GROUP_W), row),
        out_shape=jax.ShapeDtypeStruct((bsz * seq, GROUP_W), F32),
        scratch_shapes=[pltpu.VMEM((seq // d, d * GROUP_W), F32) for _, d in ATTN_PATTERNS[1:] for _ in range(2)]
        + [pltpu.VMEM((N_GROUPS * (GROUP_W // LANES), seq, LANES), F32) for _ in range(2)],
        compiler_params=_cparams(("parallel",)),
        name="prompt_attn",
    )(*qkv_groups)


def _sample_attn_kernel(q_ref, c0_ref, c1_ref, c2_ref, n0_ref, n1_ref, n2_ref, attn_ref, o0_ref, o1_ref, o2_ref,
                        *, t_new):
    q = q_ref[0]
    outs, lses = [], []
    for g, (cref, nref, oref) in enumerate(((c0_ref, n0_ref, o0_ref), (c1_ref, n1_ref, o1_ref),
                                            (c2_ref, n2_ref, o2_ref))):
        dil = ATTN_PATTERNS[g][1]
        wb = cref.shape[2]
        new_t = jnp.concatenate([nref[0], jnp.zeros((LANES - nref.shape[1], KV_ROWS), F32)], axis=0).T
        full = jnp.concatenate([cref[0], new_t], axis=1)
        oref[0] = full[:, t_new:wb + t_new]
        k_t = full[0:GROUP_W].astype(BF16)
        v_t = full[GROUP_W:KV_ROWS].astype(BF16)
        qg = q[:, g * GROUP_W:(g + 1) * GROUP_W]
        masks = _head_masks(qg.shape, 1)
        q_stack = jnp.concatenate([jnp.where(m, qg, 0.0) for m in masks], axis=0).astype(BF16)
        s = jnp.dot(q_stack, k_t, preferred_element_type=F32)
        r = lax.broadcasted_iota(I32, s.shape, 1)
        t = lax.broadcasted_iota(I32, s.shape, 0) % t_new
        ok = (r >= t) & (r <= wb + t) & (((r - t) & (dil - 1)) == 0)
        s = jnp.where(ok, s, NEG_INF)
        m = jnp.max(s, -1, keepdims=True)
        p = jnp.exp(s - m)
        l = jnp.sum(p, -1, keepdims=True)
        pv = lax.dot_general(p.astype(BF16), v_t, (((1,), (1,)), ((), ())), preferred_element_type=F32)
        lse = m + jnp.log(l)
        o = jnp.zeros((t_new, GROUP_W), F32)
        lse_o = jnp.zeros((t_new, GROUP_W), F32)
        for h, hm in enumerate(masks):
            rs = slice(h * t_new, (h + 1) * t_new)
            o = jnp.where(hm, pv[rs] / l[rs], o)
            lse_o = jnp.where(hm, lse[rs], lse_o)
        outs.append(o)
        lses.append(lse_o)
    mx = jnp.maximum(jnp.maximum(lses[0], lses[1]), lses[2])
    ws = [jnp.exp(ls - mx) for ls in lses]
    attn_ref[0] = (outs[0] * ws[0] + outs[1] * ws[1] + outs[2] * ws[2]) / (ws[0] + ws[1] + ws[2])


def _sample_attn(q3, caches_t, news):
    db, t_new, _ = q3.shape
    for c, (_, dil) in zip(caches_t, ATTN_PATTERNS):
        assert c.shape[2] == STEPS * dil and t_new <= LANES and dil & (dil - 1) == 0
    blk = lambda b: (b, 0, 0)
    cache_specs = [pl.BlockSpec((1, KV_ROWS, c.shape[2]), blk) for c in caches_t]
    outs = pl.pallas_call(
        functools.partial(_sample_attn_kernel, t_new=t_new),
        grid=(db,),
        in_specs=[pl.BlockSpec((1, t_new, ATTN_WIDTH), blk)] + cache_specs
        + [pl.BlockSpec((1, news[0].shape[1], KV_ROWS), blk)] * 3,
        out_specs=[pl.BlockSpec((1, t_new, GROUP_W), blk)] + cache_specs,
        out_shape=[jax.ShapeDtypeStruct((db, t_new, GROUP_W), F32)]
        + [jax.ShapeDtypeStruct(c.shape, c.dtype) for c in caches_t],
        compiler_params=_cparams(("parallel",)),
        name="sample_attn",
    )(q3, *caches_t, *news)
    return outs[0], outs[1:]


def _kv_window_kernel(x_ref, o_ref):
    o_ref[0] = x_ref[0].T


def _kv_window(kvf3, g, window):
    bsz, seq, _ = kvf3.shape
    w = min(window, seq)
    rows = min(w, 512)
    assert w % rows == 0 and (seq - w) % rows == 0
    first = (seq - w) // rows
    return pl.pallas_call(
        _kv_window_kernel,
        grid=(bsz, w // rows),
        in_specs=[pl.BlockSpec((1, rows, KV_ROWS), lambda b, n: (b, first + n, g))],
        out_specs=pl.BlockSpec((1, KV_ROWS, rows), lambda b, n: (b, 0, n)),
        out_shape=jax.ShapeDtypeStruct((bsz, KV_ROWS, w), F32),
        compiler_params=_cparams(("parallel", "parallel")),
        name=f"kv_window_g{g}",
    )(kvf3)


def _sample_conv_kernel(state_ref, u_ref, w_ref, b_ref, conv_ref, sout_ref):
    hist = state_ref.shape[1]
    t_new = u_ref.shape[1]
    bb = u_ref.shape[0]
    for t in range(t_new):
        acc = jnp.broadcast_to(b_ref[...], (bb, CONV_CH))
        for j in range(CONV_WIDTH):
            r = t + j
            row = state_ref[:, r, :] if r < hist else u_ref[:, r - hist, :]
            acc = acc + row * w_ref[j:j + 1, :]
        conv_ref[:, t, :] = acc
    sout_ref[:, 0:hist - t_new, :] = state_ref[:, t_new:hist, :]
    sout_ref[:, hist - t_new:hist, :] = u_ref[...]


def _sample_conv(state, u3, w_dw, b_dw, bb=32):
    db, hist, _ = state.shape
    t_new = u3.shape[1]
    assert db % bb == 0 and hist == CONV_WIDTH - 1
    blk = lambda i: (i, 0, 0)
    whole = lambda i: (0, 0)
    return pl.pallas_call(
        _sample_conv_kernel,
        grid=(db // bb,),
        in_specs=[pl.BlockSpec((bb, hist, CONV_CH), blk), pl.BlockSpec((bb, t_new, CONV_CH), blk),
                  pl.BlockSpec((CONV_WIDTH, CONV_CH), whole), pl.BlockSpec((1, CONV_CH), whole)],
        out_specs=[pl.BlockSpec((bb, t_new, CONV_CH), blk), pl.BlockSpec((bb, hist, CONV_CH), blk)],
        out_shape=[jax.ShapeDtypeStruct((db, t_new, CONV_CH), F32), jax.ShapeDtypeStruct((db, hist, CONV_CH), F32)],
        compiler_params=_cparams(("parallel",)),
        name="sample_conv",
    )(state, u3, w_dw, b_dw)


ROUTE_LANES = 2 * LANES


def _merge_kernel(xp_ref, xs_ref, cp_ref, cs_ref, ap_ref, as_ref, gp_ref, gs_ref, clg_ref, clb_ref, wco_ref, bco_ref,
                  wao_ref, wo_ref, g1_ref, b1_ref, wr_ref, br_ref, h_ref, ri_ref, rg_ref, cnt_ref, base_scr, tile_scr,
                  *, prompt_tiles):
    i = pl.program_id(0)
    tm = xp_ref.shape[0]
    is_p = i < prompt_tiles

    @pl.when(i == 0)
    def _():
        base_scr[...] = jnp.zeros_like(base_scr)

    x = jnp.where(is_p, xp_ref[...], xs_ref[...])
    conv = jnp.where(is_p, cp_ref[...], cs_ref[...])
    attn = jnp.where(is_p, ap_ref[...], as_ref[...])
    gate = jnp.where(is_p, gp_ref[...], gs_ref[...])

    cn = _layer_norm(conv, clg_ref[...], clb_ref[...])
    branch_c = jnp.dot(jax.nn.silu(cn).astype(BF16), wco_ref[...], preferred_element_type=F32) + bco_ref[...]
    branch_a = jnp.dot(attn.astype(BF16), wao_ref[...], preferred_element_type=F32)
    mixed = gate[:, 0:D_MODEL] * branch_a + gate[:, D_MODEL:2 * D_MODEL] * branch_c
    y = jnp.dot(mixed.astype(BF16), wo_ref[...], preferred_element_type=F32)
    h = _layer_norm(DN_ALPHA * x + y, g1_ref[...], b1_ref[...])
    _store_rows_as_tiles(h_ref, h, tile_scr)

    logits = jnp.dot(h.astype(BF16), wr_ref[...], preferred_element_type=F32) + br_ref[...]
    lane = lax.broadcasted_iota(I32, (tm, LANES), 1)
    big = jnp.int32(LANES)
    gl = jnp.where(lane < N_EXPERT_GROUPS, logits[:, 0:LANES], NEG_INF)
    gmax = jnp.max(gl, -1, keepdims=True)
    g_sel = jnp.min(jnp.where(gl == gmax, lane, big), -1, keepdims=True)
    p_g = 1.0 / jnp.sum(jnp.exp(gl - gmax), -1, keepdims=True)
    in_group = (lane < N_EXPERTS) & (lax.shift_right_logical(lane, 3) == g_sel)
    el = jnp.where(in_group, logits[:, LANES:2 * LANES], NEG_INF)
    v1 = jnp.max(el, -1, keepdims=True)
    i1 = jnp.min(jnp.where(el == v1, lane, big), -1, keepdims=True)
    el2 = jnp.where(lane == i1, NEG_INF, el)
    v2 = jnp.max(el2, -1, keepdims=True)
    i2 = jnp.min(jnp.where(el2 == v2, lane, big), -1, keepdims=True)
    e21 = jnp.exp(v2 - v1)
    gate1 = p_g / (1.0 + e21)
    gate2 = p_g * e21 / (1.0 + e21)

    oh = ((lane == i1) | (lane == i2)).astype(BF16)
    ri_ = lax.broadcasted_iota(I32, (tm, tm), 0)
    ci_ = lax.broadcasted_iota(I32, (tm, tm), 1)
    tri = (ci_ < ri_).astype(BF16)
    before = jnp.dot(tri, oh, preferred_element_type=F32) + base_scr[...]
    rank1 = jnp.sum(jnp.where(lane == i1, before, 0.0), -1, keepdims=True).astype(I32)
    rank2 = jnp.sum(jnp.where(lane == i2, before, 0.0), -1, keepdims=True).astype(I32)
    base_scr[...] = base_scr[...] + jnp.sum(oh.astype(F32), 0, keepdims=True)
    cnt_ref[...] = base_scr[...]

    ri_ref[...] = jnp.where(lane == 0, i1, jnp.where(lane == 1, i2, jnp.where(lane == 2, rank1,
                            jnp.where(lane == 3, rank2, 0))))
    rg_ref[...] = jnp.where(lane == 0, gate1, jnp.where(lane == 1, gate2, 0.0))


def _merge(acts_p, acts_s, weights, tm):
    n_p, n_s = acts_p[0].shape[0], acts_s[0].shape[0]
    assert n_p % tm == 0 and n_s % tm == 0
    pt = n_p // tm
    n_all = n_p + n_s
    prow = lambda i: (jnp.minimum(i, pt - 1), 0)
    srow = lambda i: (jnp.maximum(i - pt, 0), 0)
    row = lambda i: (i, 0)
    whole = lambda i: (0, 0)
    act_args, act_specs = [], []
    for a_p, a_s in zip(acts_p, acts_s):
        act_args += [a_p, a_s]
        act_specs += [pl.BlockSpec((tm, a_p.shape[1]), prow), pl.BlockSpec((tm, a_s.shape[1]), srow)]
    return pl.pallas_call(
        functools.partial(_merge_kernel, prompt_tiles=pt),
        grid=(n_all // tm,),
        in_specs=act_specs + [pl.BlockSpec(w.shape, whole) for w in weights],
        out_specs=[pl.BlockSpec((tm, ROW_SUB, LANES), lambda i: (i, 0, 0)), pl.BlockSpec((tm, LANES), row),
                   pl.BlockSpec((tm, LANES), row), pl.BlockSpec((1, LANES), whole)],
        out_shape=[jax.ShapeDtypeStruct((n_all, ROW_SUB, LANES), F32), jax.ShapeDtypeStruct((n_all, LANES), I32),
                   jax.ShapeDtypeStruct((n_all, LANES), F32), jax.ShapeDtypeStruct((1, LANES), F32)],
        scratch_shapes=[pltpu.VMEM((1, LANES), F32), pltpu.VMEM((tm // 8, 8 * ROW_SUB, LANES), F32)],
        compiler_params=_cparams(("arbitrary",)),
        name="merge",
    )(*act_args, *weights)


def _route_invert_kernel(bounds_ref, cnt_ref, route_ref, inv_ref, *, n_all):
    i = pl.program_id(0)
    tm = route_ref.shape[2] // 4

    @pl.when(i == 0)
    def _():
        def fill(j, carry):
            inv_ref[j] = jnp.int32(-1)
            return carry

        for e in range(N_EXPERTS):
            lax.fori_loop(bounds_ref[e] + cnt_ref[e], bounds_ref[e + 1], fill, 0)
        lax.fori_loop(bounds_ref[N_EXPERTS], inv_ref.shape[0], fill, 0)

    def body(t, carry):
        tok = i * tm + t
        for k in range(2):
            row = bounds_ref[route_ref[0, 0, 4 * t + k]] + route_ref[0, 0, 4 * t + 2 + k]
            inv_ref[row] = k * n_all + tok
        return carry

    lax.fori_loop(0, tm, body, 0, unroll=8)


def _route_invert(bounds, cnt, route_flat, n_rows, n_all):
    tm = route_flat.shape[2] // 4
    return pl.pallas_call(
        functools.partial(_route_invert_kernel, n_all=n_all),
        grid_spec=pltpu.PrefetchScalarGridSpec(
            num_scalar_prefetch=2,
            grid=(n_all // tm,),
            in_specs=[pl.BlockSpec((1, 1, 4 * tm), lambda i, b, c: (i, 0, 0), memory_space=pltpu.SMEM)],
            out_specs=pl.BlockSpec(memory_space=pltpu.SMEM),
        ),
        out_shape=jax.ShapeDtypeStruct((n_rows,), I32),
        compiler_params=_cparams(("arbitrary",)),
        name="route_invert",
    )(bounds, cnt, route_flat)


def _experts_kernel(ce_ref, nv_ref, gsrc_ref, gsrc_next_ref, wg_ref, wu_ref, wd_ref, h_any, y_ref,
                    wg_scr, wu_scr, wd_scr, xbuf, tile_scr, gsem):
    c = pl.program_id(0)
    nv = nv_ref[0]
    slot = c % 2

    def start_gather(src_ref, sl):
        for r in range(CHUNK):
            pltpu.make_async_copy(h_any.at[src_ref[0, 0, r]], xbuf.at[sl, r], gsem.at[sl]).start()

    @pl.when((c == 0) & (nv > 0))
    def _():
        start_gather(gsrc_ref, 0)

    @pl.when(c + 1 < nv)
    def _():
        start_gather(gsrc_next_ref, 1 - slot)

    @pl.when((c < nv) & ((c == 0) | (ce_ref[c] != ce_ref[jnp.maximum(c - 1, 0)])))
    def _():
        wg_scr[...] = wg_ref[0].astype(BF16)
        wu_scr[...] = wu_ref[0].astype(BF16)
        wd_scr[...] = wd_ref[0].astype(BF16)

    @pl.when(c < nv)
    def _():
        for _ in range(CHUNK):
            pltpu.make_async_copy(h_any.at[0], xbuf.at[0, 0], gsem.at[slot]).wait()
        x = _load_tiles_as_rows(xbuf.at[slot], tile_scr).astype(BF16)
        a = jnp.dot(x, wg_scr[...], preferred_element_type=F32)
        b = jnp.dot(x, wu_scr[...], preferred_element_type=F32)
        y = jnp.dot((jax.nn.silu(a) * b).astype(BF16), wd_scr[...], preferred_element_type=F32)
        _store_rows_as_tiles(y_ref, y, tile_scr)

    @pl.when(c >= nv)
    def _():
        y_ref[...] = jnp.zeros_like(y_ref)


def _experts(chunk_expert, n_valid, gsrc3, h3, w_eg, w_eu, w_ed):
    n_chunks = gsrc3.shape[0]
    last = lambda nv: jnp.maximum(nv[0] - 1, 0)
    cur = lambda c, ce, nv: (jnp.minimum(c, last(nv)), 0, 0)
    nxt = lambda c, ce, nv: (jnp.minimum(c + 1, last(nv)), 0, 0)
    wsel = lambda c, ce, nv: (ce[jnp.minimum(c, last(nv))], 0, 0)
    return pl.pallas_call(
        _experts_kernel,
        grid_spec=pltpu.PrefetchScalarGridSpec(
            num_scalar_prefetch=2,
            grid=(n_chunks,),
            in_specs=[pl.BlockSpec((1, 1, CHUNK), cur, memory_space=pltpu.SMEM),
                      pl.BlockSpec((1, 1, CHUNK), nxt, memory_space=pltpu.SMEM),
                      pl.BlockSpec((1, D_MODEL, D_EXPERT), wsel), pl.BlockSpec((1, D_MODEL, D_EXPERT), wsel),
                      pl.BlockSpec((1, D_EXPERT, D_MODEL), wsel), pl.BlockSpec(memory_space=pl.ANY)],
            out_specs=pl.BlockSpec((CHUNK, ROW_SUB, LANES), lambda c, ce, nv: (c, 0, 0)),
            scratch_shapes=[pltpu.VMEM((D_MODEL, D_EXPERT), BF16), pltpu.VMEM((D_MODEL, D_EXPERT), BF16),
                            pltpu.VMEM((D_EXPERT, D_MODEL), BF16),
                            pltpu.VMEM((2, CHUNK, ROW_SUB, LANES), F32),
                            pltpu.VMEM((CHUNK // 8, 8 * ROW_SUB, LANES), F32),
                            pltpu.SemaphoreType.DMA((2,))],
        ),
        out_shape=jax.ShapeDtypeStruct((n_chunks * CHUNK, ROW_SUB, LANES), F32),
        compiler_params=_cparams(("arbitrary",)),
        name="moe_experts",
    )(chunk_expert, n_valid, gsrc3, gsrc3, w_eg, w_eu, w_ed, h3)


def _combine_kernel(dest_ref, dest_next_ref, h_ref, rg_ref, g2_ref, b2_ref, yg_any, yp_ref, ys_ref,
                    ybuf, t0_scr, t1_scr, t2_scr, sem, *, prompt_tiles):
    i = pl.program_id(0)
    tm = h_ref.shape[0]
    slot = i % 2

    def start_gather(src_ref, sl):
        for r in range(2 * tm):
            pltpu.make_async_copy(yg_any.at[src_ref[0, 0, r]], ybuf.at[sl, r], sem.at[sl]).start()

    @pl.when(i == 0)
    def _():
        start_gather(dest_ref, 0)

    @pl.when(i + 1 < pl.num_programs(0))
    def _():
        start_gather(dest_next_ref, 1 - slot)

    for _ in range(2 * tm):
        pltpu.make_async_copy(yg_any.at[0], ybuf.at[0, 0], sem.at[slot]).wait()
    moe = (_load_tiles_as_rows(ybuf.at[slot, pl.ds(0, tm)], t1_scr) * rg_ref[:, 0:1]
           + _load_tiles_as_rows(ybuf.at[slot, pl.ds(tm, tm)], t2_scr) * rg_ref[:, 1:2])
    out = _layer_norm(DN_ALPHA * _load_tiles_as_rows(h_ref, t0_scr) + moe, g2_ref[...], b2_ref[...])

    @pl.when(i < prompt_tiles)
    def _():
        yp_ref[...] = out

    @pl.when(i >= prompt_tiles)
    def _():
        ys_ref[...] = out


def _combine(dest3, h3, yg, route_g, ln2_g, ln2_b, n_prompt, tm):
    n = h3.shape[0]
    pt, nt = n_prompt // tm, n // tm
    row = lambda i: (i, 0)
    whole = lambda i: (0, 0)
    return pl.pallas_call(
        functools.partial(_combine_kernel, prompt_tiles=pt),
        grid=(nt,),
        in_specs=[pl.BlockSpec((1, 1, 2 * tm), lambda i: (i, 0, 0), memory_space=pltpu.SMEM),
                  pl.BlockSpec((1, 1, 2 * tm), lambda i: (jnp.minimum(i + 1, nt - 1), 0, 0), memory_space=pltpu.SMEM),
                  pl.BlockSpec((tm, ROW_SUB, LANES), lambda i: (i, 0, 0)), pl.BlockSpec((tm, LANES), row),
                  pl.BlockSpec((1, D_MODEL), whole), pl.BlockSpec((1, D_MODEL), whole),
                  pl.BlockSpec(memory_space=pl.ANY)],
        out_specs=[pl.BlockSpec((tm, D_MODEL), lambda i: (jnp.minimum(i, pt - 1), 0)),
                   pl.BlockSpec((tm, D_MODEL), lambda i: (jnp.maximum(i - pt, 0), 0))],
        out_shape=[jax.ShapeDtypeStruct((n_prompt, D_MODEL), F32), jax.ShapeDtypeStruct((n - n_prompt, D_MODEL), F32)],
        scratch_shapes=[pltpu.VMEM((2, 2 * tm, ROW_SUB, LANES), F32)]
        + [pltpu.VMEM((tm // 8, 8 * ROW_SUB, LANES), F32) for _ in range(3)] + [pltpu.SemaphoreType.DMA((2,))],
        compiler_params=_cparams(("arbitrary",)),
        name="moe_combine",
    )(dest3, dest3, h3, route_g, ln2_g, ln2_b, yg)


def kernel(x_prompt, x_sample, cache_kv_w128, cache_kv_w512, cache_kv_w2048, state_conv, w_in, b_in, w_dw, b_dw, conv_ln_g, conv_ln_b, w_conv_out, b_conv_out, w_attn_out, w_o, ln1_g, ln1_b, w_router_group, b_router_group, w_router_expert, b_router_expert, w_expert_gate, w_expert_up, w_expert_down, ln2_g, ln2_b):
    assert w_in.shape[0] == DEPTH == 1
    bsz, seq, _ = x_prompt.shape
    db, t_new, _ = x_sample.shape
    caches = (cache_kv_w128, cache_kv_w512, cache_kv_w2048)
    past = cache_kv_w2048.shape[2]
    n_p, n_s = bsz * seq, db * t_new
    n_all = n_p + n_s
    tm = ROW_TILE
    kvs = (HEADS_PER_GROUP, HEAD_DIM)

    w_bf = w_in[0].astype(BF16)
    xp2, xs2 = x_prompt.reshape(n_p, D_MODEL), x_sample.reshape(n_s, D_MODEL)
    q0, q1, q2, kvf_p, u_p, conv_p, gate_p = _in_proj(xp2, w_bf, b_in, _rotary_tables(jnp.arange(seq)), tm,
                                                     conv_w=(w_dw[0], b_dw), tiles_per_seq=seq // tm)
    pos_s = past + jnp.arange(n_s) % t_new
    q_s, kvf_s, u_s, gate_s = _in_proj(xs2, w_bf, b_in, _rotary_tables(pos_s), tm)

    attn_p = _prompt_attn((q0, q1, q2), bsz, seq)

    kv_new = kvf_s.reshape(db, t_new, N_GROUPS, KV_ROWS)
    caches_t = [jnp.transpose(c[0], (0, 2, 3, 4, 1)).reshape(db, KV_ROWS, c.shape[2]) for c in caches]
    news = [jnp.pad(kv_new[:, :, g], ((0, 0), (0, -t_new % 8), (0, 0))) for g in range(N_GROUPS)]
    attn_s, kv_sample_t = _sample_attn(q_s.reshape(db, t_new, ATTN_WIDTH), caches_t, news)
    kv_sample = [jnp.transpose(o.reshape(db, 2, *kvs, o.shape[2]), (0, 4, 1, 2, 3))[None] for o in kv_sample_t]

    conv_s, conv_state_s = _sample_conv(state_conv[0], u_s.reshape(db, t_new, CONV_CH), w_dw[0], b_dw)

    w_r = jnp.zeros((D_MODEL, ROUTE_LANES), F32)
    w_r = w_r.at[:, 0:N_EXPERT_GROUPS].set(w_router_group[0]).at[:, LANES:LANES + N_EXPERTS].set(w_router_expert[0])
    b_r = jnp.zeros((1, ROUTE_LANES), F32)
    b_r = b_r.at[:, 0:N_EXPERT_GROUPS].set(b_router_group).at[:, LANES:LANES + N_EXPERTS].set(b_router_expert)
    weights = [conv_ln_g, conv_ln_b, w_conv_out[0].astype(BF16), b_conv_out, w_attn_out[0].astype(BF16),
               w_o[0].astype(BF16), ln1_g, ln1_b, w_r.astype(BF16), b_r]
    acts_p = (xp2, conv_p, attn_p, gate_p)
    acts_s = (xs2, conv_s.reshape(n_s, CONV_CH), attn_s.reshape(n_s, GROUP_W), gate_s)
    h3, route_i, route_g, counts = _merge(acts_p, acts_s, weights, tm)

    cnt = counts[0, :N_EXPERTS].astype(I32)
    padded = ((cnt + CHUNK - 1) // CHUNK) * CHUNK
    pad_end = jnp.cumsum(padded)
    bounds = jnp.concatenate([jnp.zeros((1,), I32), pad_end]).astype(I32)
    n_chunks = (2 * n_all) // CHUNK + N_EXPERTS
    chunk_row0 = jnp.arange(n_chunks, dtype=I32) * CHUNK
    chunk_expert = jnp.minimum(jnp.sum((pad_end[None, :] <= chunk_row0[:, None]).astype(I32), axis=1),
                               N_EXPERTS - 1).astype(I32)
    n_valid = (pad_end[-1:] // CHUNK).astype(I32)
    route_flat = route_i[:, 0:4].reshape(n_all // tm, 1, 4 * tm)

    inv = _route_invert(bounds, cnt, route_flat, n_chunks * CHUNK, n_all)
    gsrc = jnp.where(inv >= n_all, inv - n_all, jnp.maximum(inv, 0))
    yg = _experts(chunk_expert, n_valid, gsrc.reshape(n_chunks, 1, CHUNK), h3,
                  w_expert_gate[0], w_expert_up[0], w_expert_down[0])
    e_onehot = route_i[:, 0:2, None] == jnp.arange(N_EXPERTS, dtype=I32)[None, None, :]
    dest = jnp.sum(jnp.where(e_onehot, bounds[None, None, :N_EXPERTS], 0), axis=-1) + route_i[:, 2:4]
    dest3 = jnp.transpose(dest.reshape(n_all // tm, tm, 2), (0, 2, 1)).reshape(n_all // tm, 1, 2 * tm)
    y_p, y_s = _combine(dest3, h3, yg, route_g, ln2_g, ln2_b, n_p, tm)

    kvf3 = kvf_p.reshape(bsz, seq, N_GROUPS * KV_ROWS)
    kv_prompt = []
    for g, (w, _) in enumerate(ATTN_PATTERNS):
        win_t = _kv_window(kvf3, g, w)
        kv_prompt.append(jnp.transpose(win_t.reshape(bsz, 2, *kvs, win_t.shape[2]), (0, 4, 1, 2, 3))[None])
    conv_prompt = u_p.reshape(1, bsz, seq, CONV_CH)[:, :, seq - (CONV_WIDTH - 1):]
    return (y_p.reshape(bsz, seq, D_MODEL), y_s.reshape(db, t_new, D_MODEL),
            kv_prompt[0], kv_prompt[1], kv_prompt[2], conv_prompt,
            kv_sample[0], kv_sample[1], kv_sample[2], conv_state_s[None])
```

```python
import functools

import jax
import jax.numpy as jnp
from jax import lax
from jax.experimental import pallas as pl
from jax.experimental.pallas import tpu as pltpu

F32 = jnp.float32
BF16 = jnp.bfloat16
I32 = jnp.int32

D_MODEL = 1024
HEAD_DIM = 64
HEADS_PER_GROUP = 4
GROUP_W = HEADS_PER_GROUP * HEAD_DIM
ATTN_PATTERNS = ((128, 1), (512, 4), (2048, 16))
N_GROUPS = len(ATTN_PATTERNS)
ATTN_WIDTH = N_GROUPS * GROUP_W
QKV_W = 3 * GROUP_W
STEPS = 128
ROT_HALF = 8
ROPE_THETA = 500000.0
CONV_CH = 512
CONV_WIDTH = 31
HALO = 32
N_EXPERTS = 32
N_EXPERT_GROUPS = 4
D_EXPERT = 512
DEPTH = 1
DN_ALPHA = (2 * DEPTH) ** 0.25
LN_EPS = 1e-5
SCALE = HEAD_DIM ** -0.5
NEG_INF = -1e30
LANES = 128
ROW_SUB = D_MODEL // LANES
KV_ROWS = 2 * GROUP_W
ROW_TILE = 256
CHUNK = 256
VMEM_LIMIT = 56 * 1024 * 1024

_Q0, _K0, _V0 = 0, ATTN_WIDTH, 2 * ATTN_WIDTH
_UA0 = 3 * ATTN_WIDTH
_UB0 = _UA0 + CONV_CH
_G0 = _UB0 + CONV_CH
IN_WIDTH = _G0 + 2 * D_MODEL


def _cparams(sem):
    return pltpu.CompilerParams(dimension_semantics=sem, vmem_limit_bytes=VMEM_LIMIT)


def _layer_norm(x, g, b):
    mu = jnp.mean(x, -1, keepdims=True)
    xc = x - mu
    var = jnp.mean(xc * xc, -1, keepdims=True)
    return xc * lax.rsqrt(var + LN_EPS) * g + b


def _head_masks(shape, lane_axis):
    lane = lax.broadcasted_iota(I32, shape, lane_axis)
    return [(lane >= h * HEAD_DIM) & (lane < (h + 1) * HEAD_DIM) for h in range(HEADS_PER_GROUP)]


def _store_rows_as_tiles(ref3, val2d, tmp):
    sub = 8
    groups = val2d.shape[0] // sub
    for c in range(ROW_SUB):
        tmp[:, c * sub:(c + 1) * sub, :] = val2d[:, c * LANES:(c + 1) * LANES].reshape(groups, sub, LANES)
    for s in range(sub):
        ref3[pl.ds(s, groups, stride=sub), :, :] = tmp[:, pl.ds(s, ROW_SUB, stride=sub), :]


def _load_tiles_as_rows(ref3, tmp):
    sub = 8
    groups = ref3.shape[0] // sub
    for s in range(sub):
        tmp[:, pl.ds(s, ROW_SUB, stride=sub), :] = ref3[pl.ds(s, groups, stride=sub), :, :]
    return jnp.concatenate([tmp[:, c * sub:(c + 1) * sub, :].reshape(groups * sub, LANES) for c in range(ROW_SUB)],
                           axis=1)


def _rotary_tables(pos):
    inv_freq = ROPE_THETA ** (-jnp.arange(ROT_HALF, dtype=F32) / ROT_HALF)
    ang = pos.astype(F32)[:, None] * inv_freq[None, :]
    cos, sin = jnp.cos(ang), jnp.sin(ang)
    zero = jnp.zeros_like(cos)
    p = pos.shape[0]
    rest = HEAD_DIM - 2 * ROT_HALF
    c_head = jnp.concatenate([cos, cos, jnp.ones((p, rest), F32)], -1)
    sa_head = jnp.concatenate([zero, sin, jnp.zeros((p, rest), F32)], -1)
    sb_head = jnp.concatenate([-sin, zero, jnp.zeros((p, rest), F32)], -1)
    rep = LANES // HEAD_DIM
    return (jnp.tile(c_head, (1, rep)), jnp.tile(sa_head, (1, rep)), jnp.tile(sb_head, (1, rep)))


def _in_proj_kernel(*refs, prompt, tiles_per_seq):
    if prompt:
        (x_ref, w_ref, b_ref, c_ref, sa_ref, sb_ref, wdw_ref, bdw_ref,
         q0_ref, q1_ref, q2_ref, kvf_ref, u_ref, conv_ref, gate_ref, ext_scr, sh_scr, rm_scr) = refs
    else:
        (x_ref, w_ref, b_ref, c_ref, sa_ref, sb_ref, qs_ref, kvf_ref, u_ref, gate_ref) = refs
    tm = x_ref.shape[0]
    if prompt:
        @pl.when(pl.program_id(0) % tiles_per_seq == 0)
        def _():
            ext_scr[0:HALO, :] = jnp.zeros((HALO, CONV_CH), F32)

    xb = x_ref[...].astype(BF16)

    def mm(c0, c1):
        return jnp.dot(xb, w_ref[:, c0:c1], preferred_element_type=F32) + b_ref[:, c0:c1]

    u = mm(_UA0, _UB0) * jax.nn.sigmoid(mm(_UB0, _G0))
    u_ref[...] = u
    if prompt:
        ext_scr[HALO:HALO + tm, :] = u
        off = HALO - (CONV_WIDTH - 1)
        sub = 8
        conv = jnp.broadcast_to(bdw_ref[...], (tm, CONV_CH))
        for s in range(sub):
            n_a = (CONV_WIDTH - s + sub - 1) // sub
            rows = tm + sub * (n_a - 1)
            sh_scr[0:rows, :] = ext_scr[off + s:off + s + rows, :]
            for a in range(n_a):
                j = sub * a + s
                conv = conv + sh_scr[sub * a:sub * a + tm, :] * wdw_ref[j:j + 1, :]
        conv_ref[...] = conv
        ext_scr[0:HALO, :] = ext_scr[tm:tm + HALO, :]

    c, sa, sb = c_ref[...], sa_ref[...], sb_ref[...]

    def rotary(z):
        parts = []
        for j in range(z.shape[1] // LANES):
            zc = z[:, j * LANES:(j + 1) * LANES]
            parts.append(zc * c + pltpu.roll(zc, ROT_HALF, 1) * sa + pltpu.roll(zc, LANES - ROT_HALF, 1) * sb)
        return jnp.concatenate(parts, axis=1)

    q = rotary(mm(_Q0, _K0)) * SCALE
    k = rotary(mm(_K0, _V0))
    v = mm(_V0, _UA0)
    for g in range(N_GROUPS):
        gs = slice(g * GROUP_W, (g + 1) * GROUP_W)
        kvf_ref[:, 2 * g * GROUP_W:(2 * g + 1) * GROUP_W] = k[:, gs]
        kvf_ref[:, (2 * g + 1) * GROUP_W:(2 * g + 2) * GROUP_W] = v[:, gs]
    gate_ref[...] = jax.nn.sigmoid(mm(_G0, IN_WIDTH))

    if not prompt:
        qs_ref[...] = q
        return

    for g, out_ref in enumerate((q0_ref, q1_ref, q2_ref)):
        dil = ATTN_PATTERNS[g][1]
        gs = slice(g * GROUP_W, (g + 1) * GROUP_W)
        qkv = jnp.concatenate([q[:, gs], k[:, gs], v[:, gs]], axis=1)
        if dil == 1:
            out_ref[...] = qkv.astype(BF16)
        else:
            nck = QKV_W // LANES
            for ck in range(nck):
                rm_scr[ck] = qkv[:, ck * LANES:(ck + 1) * LANES]
            for r in range(dil):
                blk = jnp.concatenate([rm_scr[ck, pl.ds(r, tm // dil, stride=dil), :] for ck in range(nck)], axis=1)
                out_ref[:, r * QKV_W:(r + 1) * QKV_W] = blk.astype(BF16)


def _in_proj(x2d, w_bf, b2d, tables, tm, conv_w=None, tiles_per_seq=1):
    n = x2d.shape[0]
    p = tables[0].shape[0]
    prompt = conv_w is not None
    assert n % tm == 0 and p % tm == 0
    tpb = p // tm
    row = lambda i: (i, 0)
    tab = lambda i: (i % tpb, 0)
    whole = lambda i: (0, 0)
    in_specs = [pl.BlockSpec((tm, D_MODEL), row), pl.BlockSpec((D_MODEL, IN_WIDTH), whole),
                pl.BlockSpec((1, IN_WIDTH), whole)] + [pl.BlockSpec((tm, LANES), tab)] * 3
    common_specs = [pl.BlockSpec((tm, 2 * ATTN_WIDTH), row), pl.BlockSpec((tm, CONV_CH), row)]
    common_shapes = [jax.ShapeDtypeStruct((n, 2 * ATTN_WIDTH), F32),
                     jax.ShapeDtypeStruct((n, CONV_CH), F32)]
    gate_spec, gate_shape = pl.BlockSpec((tm, 2 * D_MODEL), row), jax.ShapeDtypeStruct((n, 2 * D_MODEL), F32)
    if prompt:
        args = (x2d, w_bf, b2d, *tables, *conv_w)
        in_specs += [pl.BlockSpec((CONV_WIDTH, CONV_CH), whole), pl.BlockSpec((1, CONV_CH), whole)]
        q_specs = [pl.BlockSpec((tm // d, d * QKV_W), row) for _, d in ATTN_PATTERNS]
        q_shapes = [jax.ShapeDtypeStruct((n // d, d * QKV_W), BF16) for _, d in ATTN_PATTERNS]
        out_specs = q_specs + common_specs + [pl.BlockSpec((tm, CONV_CH), row), gate_spec]
        out_shape = q_shapes + common_shapes + [jax.ShapeDtypeStruct((n, CONV_CH), F32), gate_shape]
        scratch = [pltpu.VMEM((HALO + tm, CONV_CH), F32), pltpu.VMEM((HALO + tm, CONV_CH), F32),
                   pltpu.VMEM((QKV_W // LANES, tm, LANES), F32)]
    else:
        args = (x2d, w_bf, b2d, *tables)
        out_specs = [pl.BlockSpec((tm, ATTN_WIDTH), row)] + common_specs + [gate_spec]
        out_shape = [jax.ShapeDtypeStruct((n, ATTN_WIDTH), F32)] + common_shapes + [gate_shape]
        scratch = []
    return pl.pallas_call(
        functools.partial(_in_proj_kernel, prompt=prompt, tiles_per_seq=tiles_per_seq),
        grid=(n // tm,),
        in_specs=in_specs,
        out_specs=out_specs,
        out_shape=out_shape,
        scratch_shapes=scratch,
        compiler_params=_cparams(("arbitrary",)),
        name="in_proj_prompt" if prompt else "in_proj_sample",
    )(*args)


def _attn_block(q, keys, vals, first):
    masks = _head_masks(q.shape, 1)
    zero = jnp.zeros_like(q)
    q_stack = jnp.concatenate([jnp.where(m, q, zero) for m in masks], axis=0)
    s = lax.dot_general(q_stack, keys, (((1,), (1,)), ((), ())), preferred_element_type=F32)
    qi = lax.broadcasted_iota(I32, s.shape, 0) % STEPS
    kj = lax.broadcasted_iota(I32, s.shape, 1)
    ok = (kj <= qi) if first else ((kj >= qi) & (kj <= qi + STEPS))
    s = jnp.where(ok, s, NEG_INF)
    m = jnp.max(s, -1, keepdims=True)
    p = jnp.exp(s - m)
    l = jnp.sum(p, -1, keepdims=True)
    pv = jnp.dot(p.astype(BF16), vals, preferred_element_type=F32)
    lse = m + jnp.log(l)
    o = jnp.zeros((STEPS, GROUP_W), F32)
    lse_o = jnp.zeros((STEPS, GROUP_W), F32)
    for h, hm in enumerate(masks):
        rs = slice(h * STEPS, (h + 1) * STEPS)
        o = jnp.where(hm, pv[rs] / l[rs], o)
        lse_o = jnp.where(hm, lse[rs], lse_o)
    return o, lse_o


def _prompt_attn_kernel(q0_ref, q1_ref, q2_ref, attn_ref, o_rm1, l_rm1, o_rm2, l_rm2, o_nat, l_nat):
    halves = GROUP_W // LANES

    def put_nat(g, rows, o, lse):
        for hf in range(halves):
            hs = slice(hf * LANES, (hf + 1) * LANES)
            o_nat[g * halves + hf, rows, :] = o[:, hs]
            l_nat[g * halves + hf, rows, :] = lse[:, hs]

    for g, ref in enumerate((q0_ref, q1_ref, q2_ref)):
        dil = ATTN_PATTERNS[g][1]
        o_rm, l_rm = (None, o_rm1, o_rm2)[g], (None, l_rm1, l_rm2)[g]
        rows = ref.shape[0]
        nb = rows // STEPS
        for r in range(dil):
            base = r * QKV_W
            qc, kc, vc = (slice(base + j * GROUP_W, base + (j + 1) * GROUP_W) for j in range(3))
            oc = slice(r * GROUP_W, (r + 1) * GROUP_W)
            o, lse = _attn_block(ref[0:STEPS, qc], ref[0:STEPS, kc], ref[0:STEPS, vc], True)
            if dil == 1:
                put_nat(g, slice(0, STEPS), o, lse)
            else:
                o_rm[0:STEPS, oc] = o
                l_rm[0:STEPS, oc] = lse

            def body(n, carry, ref=ref, qc=qc, kc=kc, vc=vc, oc=oc, dil=dil, g=g, o_rm=o_rm, l_rm=l_rm):
                q0 = pl.multiple_of(n * STEPS, STEPS)
                k0 = pl.multiple_of((n - 1) * STEPS, STEPS)
                o, lse = _attn_block(ref[pl.ds(q0, STEPS), qc], ref[pl.ds(k0, 2 * STEPS), kc],
                                     ref[pl.ds(k0, 2 * STEPS), vc], False)
                if dil == 1:
                    put_nat(g, pl.ds(q0, STEPS), o, lse)
                else:
                    o_rm[pl.ds(q0, STEPS), oc] = o
                    l_rm[pl.ds(q0, STEPS), oc] = lse
                return carry

            if nb > 1:
                lax.fori_loop(1, nb, body, 0)
        if dil > 1:
            for r in range(dil):
                oc = slice(r * GROUP_W, (r + 1) * GROUP_W)
                put_nat(g, pl.ds(r, rows, stride=dil), o_rm[0:rows, oc], l_rm[0:rows, oc])

    def nat(ref, g):
        return jnp.concatenate([ref[g * halves + hf] for hf in range(halves)], axis=1)

    lses = [nat(l_nat, g) for g in range(N_GROUPS)]
    mx = jnp.maximum(jnp.maximum(lses[0], lses[1]), lses[2])
    ws = [jnp.exp(ls - mx) for ls in lses]
    attn_ref[...] = (nat(o_nat, 0) * ws[0] + nat(o_nat, 1) * ws[1] + nat(o_nat, 2) * ws[2]) / (ws[0] + ws[1] + ws[2])


def _prompt_attn(qkv_groups, bsz, seq):
    row = lambda b: (b, 0)
    in_specs = [pl.BlockSpec((seq // d, d * QKV_W), row) for _, d in ATTN_PATTERNS]
    return pl.pallas_call(
        _prompt_attn_kernel,
        grid=(bsz,),
        in_specs=in_specs,
        out_specs=pl.BlockSpec((seq, GROUP_W), row),
        out_shape=jax.ShapeDtypeStruct((bsz * seq, GROUP_W), F32),
        scratch_shapes=[pltpu.VMEM((seq // d, d * GROUP_W), F32) for _, d in ATTN_PATTERNS[1:] for _ in range(2)]
        + [pltpu.VMEM((N_GROUPS * (GROUP_W // LANES), seq, LANES), F32) for _ in range(2)],
        compiler_params=_cparams(("parallel",)),
        name="prompt_attn",
    )(*qkv_groups)


def _sample_attn_kernel(q_ref, c0_ref, c1_ref, c2_ref, n0_ref, n1_ref, n2_ref, attn_ref, o0_ref, o1_ref, o2_ref,
                        *, t_new):
    q = q_ref[0]
    outs, lses = [], []
    for g, (cref, nref, oref) in enumerate(((c0_ref, n0_ref, o0_ref), (c1_ref, n1_ref, o1_ref),
                                            (c2_ref, n2_ref, o2_ref))):
        dil = ATTN_PATTERNS[g][1]
        wb = cref.shape[2]
        new_t = jnp.concatenate([nref[0], jnp.zeros((LANES - nref.shape[1], KV_ROWS), F32)], axis=0).T
        full = jnp.concatenate([cref[0], new_t], axis=1)
        oref[0] = full[:, t_new:wb + t_new]
        k_t = full[0:GROUP_W].astype(BF16)
        v_t = full[GROUP_W:KV_ROWS].astype(BF16)
        qg = q[:, g * GROUP_W:(g + 1) * GROUP_W]
        masks = _head_masks(qg.shape, 1)
        q_stack = jnp.concatenate([jnp.where(m, qg, 0.0) for m in masks], axis=0).astype(BF16)
        s = jnp.dot(q_stack, k_t, preferred_element_type=F32)
        r = lax.broadcasted_iota(I32, s.shape, 1)
        t = lax.broadcasted_iota(I32, s.shape, 0) % t_new
        ok = (r >= t) & (r <= wb + t) & (((r - t) & (dil - 1)) == 0)
        s = jnp.where(ok, s, NEG_INF)
        m = jnp.max(s, -1, keepdims=True)
        p = jnp.exp(s - m)
        l = jnp.sum(p, -1, keepdims=True)
        pv = lax.dot_general(p.astype(BF16), v_t, (((1,), (1,)), ((), ())), preferred_element_type=F32)
        lse = m + jnp.log(l)
        o = jnp.zeros((t_new, GROUP_W), F32)
        lse_o = jnp.zeros((t_new, GROUP_W), F32)
        for h, hm in enumerate(masks):
            rs = slice(h * t_new, (h + 1) * t_new)
            o = jnp.where(hm, pv[rs] / l[rs], o)
            lse_o = jnp.where(hm, lse[rs], lse_o)
        outs.append(o)
        lses.append(lse_o)
    mx = jnp.maximum(jnp.maximum(lses[0], lses[1]), lses[2])
    ws = [jnp.exp(ls - mx) for ls in lses]
    attn_ref[0] = (outs[0] * ws[0] + outs[1] * ws[1] + outs[2] * ws[2]) / (ws[0] + ws[1] + ws[2])


def _sample_attn(q3, caches_t, news):
    db, t_new, _ = q3.shape
    for c, (_, dil) in zip(caches_t, ATTN_PATTERNS):
        assert c.shape[2] == STEPS * dil and t_new <= LANES and dil & (dil - 1) == 0
    blk = lambda b: (b, 0, 0)
    cache_specs = [pl.BlockSpec((1, KV_ROWS, c.shape[2]), blk) for c in caches_t]
    outs = pl.pallas_call(
        functools.partial(_sample_attn_kernel, t_new=t_new),
        grid=(db,),
        in_specs=[pl.BlockSpec((1, t_new, ATTN_WIDTH), blk)] + cache_specs
        + [pl.BlockSpec((1, news[0].shape[1], KV_ROWS), blk)] * 3,
        out_specs=[pl.BlockSpec((1, t_new, GROUP_W), blk)] + cache_specs,
        out_shape=[jax.ShapeDtypeStruct((db, t_new, GROUP_W), F32)]
        + [jax.ShapeDtypeStruct(c.shape, c.dtype) for c in caches_t],
        compiler_params=_cparams(("parallel",)),
        name="sample_attn",
    )(q3, *caches_t, *news)
    return outs[0], outs[1:]


def _kv_window_kernel(x_ref, o_ref):
    o_ref[0] = x_ref[0].T


def _kv_window(kvf3, g, window):
    bsz, seq, _ = kvf3.shape
    w = min(window, seq)
    rows = min(w, 512)
    assert w % rows == 0 and (seq - w) % rows == 0
    first = (seq - w) // rows
    return pl.pallas_call(
        _kv_window_kernel,
        grid=(bsz, w // rows),
        in_specs=[pl.BlockSpec((1, rows, KV_ROWS), lambda b, n: (b, first + n, g))],
        out_specs=pl.BlockSpec((1, KV_ROWS, rows), lambda b, n: (b, 0, n)),
        out_shape=jax.ShapeDtypeStruct((bsz, KV_ROWS, w), F32),
        compiler_params=_cparams(("parallel", "parallel")),
        name=f"kv_window_g{g}",
    )(kvf3)


def _sample_conv_kernel(state_ref, u_ref, w_ref, b_ref, conv_ref, sout_ref):
    hist = state_ref.shape[1]
    t_new = u_ref.shape[1]
    bb = u_ref.shape[0]
    for t in range(t_new):
        acc = jnp.broadcast_to(b_ref[...], (bb, CONV_CH))
        for j in range(CONV_WIDTH):
            r = t + j
            row = state_ref[:, r, :] if r < hist else u_ref[:, r - hist, :]
            acc = acc + row * w_ref[j:j + 1, :]
        conv_ref[:, t, :] = acc
    sout_ref[:, 0:hist - t_new, :] = state_ref[:, t_new:hist, :]
    sout_ref[:, hist - t_new:hist, :] = u_ref[...]


def _sample_conv(state, u3, w_dw, b_dw, bb=32):
    db, hist, _ = state.shape
    t_new = u3.shape[1]
    assert db % bb == 0 and hist == CONV_WIDTH - 1
    blk = lambda i: (i, 0, 0)
    whole = lambda i: (0, 0)
    return pl.pallas_call(
        _sample_conv_kernel,
        grid=(db // bb,),
        in_specs=[pl.BlockSpec((bb, hist, CONV_CH), blk), pl.BlockSpec((bb, t_new, CONV_CH), blk),
                  pl.BlockSpec((CONV_WIDTH, CONV_CH), whole), pl.BlockSpec((1, CONV_CH), whole)],
        out_specs=[pl.BlockSpec((bb, t_new, CONV_CH), blk), pl.BlockSpec((bb, hist, CONV_CH), blk)],
        out_shape=[jax.ShapeDtypeStruct((db, t_new, CONV_CH), F32), jax.ShapeDtypeStruct((db, hist, CONV_CH), F32)],
        compiler_params=_cparams(("parallel",)),
        name="sample_conv",
    )(state, u3, w_dw, b_dw)


ROUTE_LANES = 2 * LANES


def _merge_kernel(xp_ref, xs_ref, cp_ref, cs_ref, ap_ref, as_ref, gp_ref, gs_ref, clg_ref, clb_ref, wco_ref, bco_ref,
                  wao_ref, wo_ref, g1_ref, b1_ref, wr_ref, br_ref, h_ref, ri_ref, rg_ref, cnt_ref, base_scr, tile_scr,
                  *, prompt_tiles):
    i = pl.program_id(0)
    tm = xp_ref.shape[0]
    is_p = i < prompt_tiles

    @pl.when(i == 0)
    def _():
        base_scr[...] = jnp.zeros_like(base_scr)

    x = jnp.where(is_p, xp_ref[...], xs_ref[...])
    conv = jnp.where(is_p, cp_ref[...], cs_ref[...])
    attn = jnp.where(is_p, ap_ref[...], as_ref[...])
    gate = jnp.where(is_p, gp_ref[...], gs_ref[...])

    cn = _layer_norm(conv, clg_ref[...], clb_ref[...])
    branch_c = jnp.dot(jax.nn.silu(cn).astype(BF16), wco_ref[...], preferred_element_type=F32) + bco_ref[...]
    branch_a = jnp.dot(attn.astype(BF16), wao_ref[...], preferred_element_type=F32)
    mixed = gate[:, 0:D_MODEL] * branch_a + gate[:, D_MODEL:2 * D_MODEL] * branch_c
    y = jnp.dot(mixed.astype(BF16), wo_ref[...], preferred_element_type=F32)
    h = _layer_norm(DN_ALPHA * x + y, g1_ref[...], b1_ref[...])
    _store_rows_as_tiles(h_ref, h, tile_scr)

    logits = jnp.dot(h.astype(BF16), wr_ref[...], preferred_element_type=F32) + br_ref[...]
    lane = lax.broadcasted_iota(I32, (tm, LANES), 1)
    big = jnp.int32(LANES)
    gl = jnp.where(lane < N_EXPERT_GROUPS, logits[:, 0:LANES], NEG_INF)
    gmax = jnp.max(gl, -1, keepdims=True)
    g_sel = jnp.min(jnp.where(gl == gmax, lane, big), -1, keepdims=True)
    p_g = 1.0 / jnp.sum(jnp.exp(gl - gmax), -1, keepdims=True)
    in_group = (lane < N_EXPERTS) & (lax.shift_right_logical(lane, 3) == g_sel)
    el = jnp.where(in_group, logits[:, LANES:2 * LANES], NEG_INF)
    v1 = jnp.max(el, -1, keepdims=True)
    i1 = jnp.min(jnp.where(el == v1, lane, big), -1, keepdims=True)
    el2 = jnp.where(lane == i1, NEG_INF, el)
    v2 = jnp.max(el2, -1, keepdims=True)
    i2 = jnp.min(jnp.where(el2 == v2, lane, big), -1, keepdims=True)
    e21 = jnp.exp(v2 - v1)
    gate1 = p_g / (1.0 + e21)
    gate2 = p_g * e21 / (1.0 + e21)

    oh = ((lane == i1) | (lane == i2)).astype(BF16)
    ri_ = lax.broadcasted_iota(I32, (tm, tm), 0)
    ci_ = lax.broadcasted_iota(I32, (tm, tm), 1)
    tri = (ci_ < ri_).astype(BF16)
    before = jnp.dot(tri, oh, preferred_element_type=F32) + base_scr[...]
    rank1 = jnp.sum(jnp.where(lane == i1, before, 0.0), -1, keepdims=True).astype(I32)
    rank2 = jnp.sum(jnp.where(lane == i2, before, 0.0), -1, keepdims=True).astype(I32)
    base_scr[...] = base_scr[...] + jnp.sum(oh.astype(F32), 0, keepdims=True)
    cnt_ref[...] = base_scr[...]

    ri_ref[...] = jnp.where(lane == 0, i1, jnp.where(lane == 1, i2, jnp.where(lane == 2, rank1,
                            jnp.where(lane == 3, rank2, 0))))
    rg_ref[...] = jnp.where(lane == 0, gate1, jnp.where(lane == 1, gate2, 0.0))


def _merge(acts_p, acts_s, weights, tm):
    n_p, n_s = acts_p[0].shape[0], acts_s[0].shape[0]
    assert n_p % tm == 0 and n_s % tm == 0
    pt = n_p // tm
    n_all = n_p + n_s
    prow = lambda i: (jnp.minimum(i, pt - 1), 0)
    srow = lambda i: (jnp.maximum(i - pt, 0), 0)
    row = lambda i: (i, 0)
    whole = lambda i: (0, 0)
    act_args, act_specs = [], []
    for a_p, a_s in zip(acts_p, acts_s):
        act_args += [a_p, a_s]
        act_specs += [pl.BlockSpec((tm, a_p.shape[1]), prow), pl.BlockSpec((tm, a_s.shape[1]), srow)]
    return pl.pallas_call(
        functools.partial(_merge_kernel, prompt_tiles=pt),
        grid=(n_all // tm,),
        in_specs=act_specs + [pl.BlockSpec(w.shape, whole) for w in weights],
        out_specs=[pl.BlockSpec((tm, ROW_SUB, LANES), lambda i: (i, 0, 0)), pl.BlockSpec((tm, LANES), row),
                   pl.BlockSpec((tm, LANES), row), pl.BlockSpec((1, LANES), whole)],
        out_shape=[jax.ShapeDtypeStruct((n_all, ROW_SUB, LANES), F32), jax.ShapeDtypeStruct((n_all, LANES), I32),
                   jax.ShapeDtypeStruct((n_all, LANES), F32), jax.ShapeDtypeStruct((1, LANES), F32)],
        scratch_shapes=[pltpu.VMEM((1, LANES), F32), pltpu.VMEM((tm // 8, 8 * ROW_SUB, LANES), F32)],
        compiler_params=_cparams(("arbitrary",)),
        name="merge",
    )(*act_args, *weights)


def _dispatch_kernel(bounds_ref, nv_ref, dest_ref, h_ref, xg_any, hbuf, zero_scr, sem_z, sem_r):
    i = pl.program_id(0)
    tm = h_ref.shape[0]
    n_chunks = xg_any.shape[0] // CHUNK
    slot = i % 2

    def zero_chunk(row0):
        return pltpu.make_async_copy(zero_scr, xg_any.at[pl.ds(pl.multiple_of(row0, CHUNK), CHUNK)], sem_z)

    @pl.when(i == 0)
    def _():
        zero_scr[...] = jnp.zeros_like(zero_scr)
        for e in range(N_EXPERTS):
            @pl.when(bounds_ref[e + 1] > bounds_ref[e])
            def _():
                zero_chunk(bounds_ref[e + 1] - CHUNK).start()

        def start_tail(c, carry):
            zero_chunk(c * CHUNK).start()
            return carry

        lax.fori_loop(nv_ref[0], n_chunks, start_tail, 0)
        for e in range(N_EXPERTS):
            @pl.when(bounds_ref[e + 1] > bounds_ref[e])
            def _():
                zero_chunk(0).wait()

        def wait_tail(c, carry):
            zero_chunk(0).wait()
            return carry

        lax.fori_loop(nv_ref[0], n_chunks, wait_tail, 0)

    def wait_rows(sl):
        for _ in range(2 * tm):
            pltpu.make_async_copy(hbuf.at[0, 0], xg_any.at[0], sem_r.at[sl]).wait()

    @pl.when(i >= 2)
    def _():
        wait_rows(slot)

    hbuf[slot] = h_ref[...]
    for r in range(2 * tm):
        pltpu.make_async_copy(hbuf.at[slot, r % tm], xg_any.at[dest_ref[0, 0, r]], sem_r.at[slot]).start()

    @pl.when(i == pl.num_programs(0) - 1)
    def _():
        wait_rows(slot)

        @pl.when(i >= 1)
        def _():
            wait_rows(1 - slot)


def _dispatch(bounds, n_valid, dest3, h3, n_rows, tm):
    n = h3.shape[0]
    return pl.pallas_call(
        _dispatch_kernel,
        grid_spec=pltpu.PrefetchScalarGridSpec(
            num_scalar_prefetch=2,
            grid=(n // tm,),
            in_specs=[pl.BlockSpec((1, 1, 2 * tm), lambda i, b, nv: (i, 0, 0), memory_space=pltpu.SMEM),
                      pl.BlockSpec((tm, ROW_SUB, LANES), lambda i, b, nv: (i, 0, 0))],
            out_specs=pl.BlockSpec(memory_space=pl.ANY),
            scratch_shapes=[pltpu.VMEM((2, tm, ROW_SUB, LANES), F32), pltpu.VMEM((CHUNK, ROW_SUB, LANES), F32),
                            pltpu.SemaphoreType.DMA(()), pltpu.SemaphoreType.DMA((2,))],
        ),
        out_shape=jax.ShapeDtypeStruct((n_rows, ROW_SUB, LANES), F32),
        compiler_params=_cparams(("arbitrary",)),
        name="moe_dispatch",
    )(bounds, n_valid, dest3, h3)


def _experts_kernel(ce_ref, nv_ref, x_ref, wg_ref, wu_ref, wd_ref, y_ref, wg_scr, wu_scr, wd_scr, tile_scr):
    c = pl.program_id(0)
    nv = nv_ref[0]

    @pl.when((c < nv) & ((c == 0) | (ce_ref[c] != ce_ref[jnp.maximum(c - 1, 0)])))
    def _():
        wg_scr[...] = wg_ref[0].astype(BF16)
        wu_scr[...] = wu_ref[0].astype(BF16)
        wd_scr[...] = wd_ref[0].astype(BF16)

    @pl.when(c < nv)
    def _():
        x = _load_tiles_as_rows(x_ref, tile_scr).astype(BF16)
        a = jnp.dot(x, wg_scr[...], preferred_element_type=F32)
        b = jnp.dot(x, wu_scr[...], preferred_element_type=F32)
        y = jnp.dot((jax.nn.silu(a) * b).astype(BF16), wd_scr[...], preferred_element_type=F32)
        _store_rows_as_tiles(y_ref, y, tile_scr)

    @pl.when(c >= nv)
    def _():
        y_ref[...] = jnp.zeros_like(y_ref)


def _experts(chunk_expert, n_valid, xg, w_eg, w_eu, w_ed):
    n_chunks = xg.shape[0] // CHUNK
    last = lambda nv: jnp.maximum(nv[0] - 1, 0)
    wsel = lambda c, ce, nv: (ce[jnp.minimum(c, last(nv))], 0, 0)
    return pl.pallas_call(
        _experts_kernel,
        grid_spec=pltpu.PrefetchScalarGridSpec(
            num_scalar_prefetch=2,
            grid=(n_chunks,),
            in_specs=[pl.BlockSpec((CHUNK, ROW_SUB, LANES), lambda c, ce, nv: (jnp.minimum(c, last(nv)), 0, 0)),
                      pl.BlockSpec((1, D_MODEL, D_EXPERT), wsel), pl.BlockSpec((1, D_MODEL, D_EXPERT), wsel),
                      pl.BlockSpec((1, D_EXPERT, D_MODEL), wsel)],
            out_specs=pl.BlockSpec((CHUNK, ROW_SUB, LANES), lambda c, ce, nv: (c, 0, 0)),
            scratch_shapes=[pltpu.VMEM((D_MODEL, D_EXPERT), BF16), pltpu.VMEM((D_MODEL, D_EXPERT), BF16),
                            pltpu.VMEM((D_EXPERT, D_MODEL), BF16),
                            pltpu.VMEM((CHUNK // 8, 8 * ROW_SUB, LANES), F32)],
        ),
        out_shape=jax.ShapeDtypeStruct(xg.shape, F32),
        compiler_params=_cparams(("arbitrary",)),
        name="moe_experts",
    )(chunk_expert, n_valid, xg, w_eg, w_eu, w_ed)


def _combine_kernel(dest_ref, dest_next_ref, h_ref, rg_ref, g2_ref, b2_ref, yg_any, yp_ref, ys_ref,
                    ybuf, t0_scr, t1_scr, t2_scr, sem, *, prompt_tiles):
    i = pl.program_id(0)
    tm = h_ref.shape[0]
    slot = i % 2

    def start_gather(src_ref, sl):
        for r in range(2 * tm):
            pltpu.make_async_copy(yg_any.at[src_ref[0, 0, r]], ybuf.at[sl, r], sem.at[sl]).start()

    @pl.when(i == 0)
    def _():
        start_gather(dest_ref, 0)

    @pl.when(i + 1 < pl.num_programs(0))
    def _():
        start_gather(dest_next_ref, 1 - slot)

    for _ in range(2 * tm):
        pltpu.make_async_copy(yg_any.at[0], ybuf.at[0, 0], sem.at[slot]).wait()
    moe = (_load_tiles_as_rows(ybuf.at[slot, pl.ds(0, tm)], t1_scr) * rg_ref[:, 0:1]
           + _load_tiles_as_rows(ybuf.at[slot, pl.ds(tm, tm)], t2_scr) * rg_ref[:, 1:2])
    out = _layer_norm(DN_ALPHA * _load_tiles_as_rows(h_ref, t0_scr) + moe, g2_ref[...], b2_ref[...])

    @pl.when(i < prompt_tiles)
    def _():
        yp_ref[...] = out

    @pl.when(i >= prompt_tiles)
    def _():
        ys_ref[...] = out


def _combine(dest3, h3, yg, route_g, ln2_g, ln2_b, n_prompt, tm):
    n = h3.shape[0]
    pt, nt = n_prompt // tm, n // tm
    row = lambda i: (i, 0)
    whole = lambda i: (0, 0)
    return pl.pallas_call(
        functools.partial(_combine_kernel, prompt_tiles=pt),
        grid=(nt,),
        in_specs=[pl.BlockSpec((1, 1, 2 * tm), lambda i: (i, 0, 0), memory_space=pltpu.SMEM),
                  pl.BlockSpec((1, 1, 2 * tm), lambda i: (jnp.minimum(i + 1, nt - 1), 0, 0), memory_space=pltpu.SMEM),
                  pl.BlockSpec((tm, ROW_SUB, LANES), lambda i: (i, 0, 0)), pl.BlockSpec((tm, LANES), row),
                  pl.BlockSpec((1, D_MODEL), whole), pl.BlockSpec((1, D_MODEL), whole),
                  pl.BlockSpec(memory_space=pl.ANY)],
        out_specs=[pl.BlockSpec((tm, D_MODEL), lambda i: (jnp.minimum(i, pt - 1), 0)),
                   pl.BlockSpec((tm, D_MODEL), lambda i: (jnp.maximum(i - pt, 0), 0))],
        out_shape=[jax.ShapeDtypeStruct((n_prompt, D_MODEL), F32), jax.ShapeDtypeStruct((n - n_prompt, D_MODEL), F32)],
        scratch_shapes=[pltpu.VMEM((2, 2 * tm, ROW_SUB, LANES), F32)]
        + [pltpu.VMEM((tm // 8, 8 * ROW_SUB, LANES), F32) for _ in range(3)] + [pltpu.SemaphoreType.DMA((2,))],
        compiler_params=_cparams(("arbitrary",)),
        name="moe_combine",
    )(dest3, dest3, h3, route_g, ln2_g, ln2_b, yg)


def kernel(x_prompt, x_sample, cache_kv_w128, cache_kv_w512, cache_kv_w2048, state_conv, w_in, b_in, w_dw, b_dw, conv_ln_g, conv_ln_b, w_conv_out, b_conv_out, w_attn_out, w_o, ln1_g, ln1_b, w_router_group, b_router_group, w_router_expert, b_router_expert, w_expert_gate, w_expert_up, w_expert_down, ln2_g, ln2_b):
    assert w_in.shape[0] == DEPTH == 1
    bsz, seq, _ = x_prompt.shape
    db, t_new, _ = x_sample.shape
    caches = (cache_kv_w128, cache_kv_w512, cache_kv_w2048)
    past = cache_kv_w2048.shape[2]
    n_p, n_s = bsz * seq, db * t_new
    n_all = n_p + n_s
    tm = ROW_TILE
    kvs = (HEADS_PER_GROUP, HEAD_DIM)

    w_bf = w_in[0].astype(BF16)
    xp2, xs2 = x_prompt.reshape(n_p, D_MODEL), x_sample.reshape(n_s, D_MODEL)
    q0, q1, q2, kvf_p, u_p, conv_p, gate_p = _in_proj(xp2, w_bf, b_in, _rotary_tables(jnp.arange(seq)), tm,
                                                     conv_w=(w_dw[0], b_dw), tiles_per_seq=seq // tm)
    pos_s = past + jnp.arange(n_s) % t_new
    q_s, kvf_s, u_s, gate_s = _in_proj(xs2, w_bf, b_in, _rotary_tables(pos_s), tm)

    attn_p = _prompt_attn((q0, q1, q2), bsz, seq)

    kv_new = kvf_s.reshape(db, t_new, N_GROUPS, KV_ROWS)
    caches_t = [jnp.transpose(c[0], (0, 2, 3, 4, 1)).reshape(db, KV_ROWS, c.shape[2]) for c in caches]
    news = [jnp.pad(kv_new[:, :, g], ((0, 0), (0, -t_new % 8), (0, 0))) for g in range(N_GROUPS)]
    attn_s, kv_sample_t = _sample_attn(q_s.reshape(db, t_new, ATTN_WIDTH), caches_t, news)
    kv_sample = [jnp.transpose(o.reshape(db, 2, *kvs, o.shape[2]), (0, 4, 1, 2, 3))[None] for o in kv_sample_t]

    conv_s, conv_state_s = _sample_conv(state_conv[0], u_s.reshape(db, t_new, CONV_CH), w_dw[0], b_dw)

    w_r = jnp.zeros((D_MODEL, ROUTE_LANES), F32)
    w_r = w_r.at[:, 0:N_EXPERT_GROUPS].set(w_router_group[0]).at[:, LANES:LANES + N_EXPERTS].set(w_router_expert[0])
    b_r = jnp.zeros((1, ROUTE_LANES), F32)
    b_r = b_r.at[:, 0:N_EXPERT_GROUPS].set(b_router_group).at[:, LANES:LANES + N_EXPERTS].set(b_router_expert)
    weights = [conv_ln_g, conv_ln_b, w_conv_out[0].astype(BF16), b_conv_out, w_attn_out[0].astype(BF16),
               w_o[0].astype(BF16), ln1_g, ln1_b, w_r.astype(BF16), b_r]
    acts_p = (xp2, conv_p, attn_p, gate_p)
    acts_s = (xs2, conv_s.reshape(n_s, CONV_CH), attn_s.reshape(n_s, GROUP_W), gate_s)
    h3, route_i, route_g, counts = _merge(acts_p, acts_s, weights, tm)

    cnt = counts[0, :N_EXPERTS].astype(I32)
    padded = ((cnt + CHUNK - 1) // CHUNK) * CHUNK
    pad_end = jnp.cumsum(padded)
    bounds = jnp.concatenate([jnp.zeros((1,), I32), pad_end]).astype(I32)
    n_chunks = (2 * n_all) // CHUNK + N_EXPERTS
    chunk_row0 = jnp.arange(n_chunks, dtype=I32) * CHUNK
    chunk_expert = jnp.minimum(jnp.sum((pad_end[None, :] <= chunk_row0[:, None]).astype(I32), axis=1),
                               N_EXPERTS - 1).astype(I32)
    n_valid = (pad_end[-1:] // CHUNK).astype(I32)

    e_onehot = route_i[:, 0:2, None] == jnp.arange(N_EXPERTS, dtype=I32)[None, None, :]
    dest = jnp.sum(jnp.where(e_onehot, bounds[None, None, :N_EXPERTS], 0), axis=-1) + route_i[:, 2:4]
    dest3 = jnp.transpose(dest.reshape(n_all // tm, tm, 2), (0, 2, 1)).reshape(n_all // tm, 1, 2 * tm)
    xg = _dispatch(bounds, n_valid, dest3, h3, n_chunks * CHUNK, tm)
    yg = _experts(chunk_expert, n_valid, xg, w_expert_gate[0], w_expert_up[0], w_expert_down[0])
    y_p, y_s = _combine(dest3, h3, yg, route_g, ln2_g, ln2_b, n_p, tm)

    kvf3 = kvf_p.reshape(bsz, seq, N_GROUPS * KV_ROWS)
    kv_prompt = []
    for g, (w, _) in enumerate(ATTN_PATTERNS):
        win_t = _kv_window(kvf3, g, w)
        kv_prompt.append(jnp.transpose(win_t.reshape(bsz, 2, *kvs, win_t.shape[2]), (0, 4, 1, 2, 3))[None])
    conv_prompt = u_p.reshape(1, bsz, seq, CONV_CH)[:, :, seq - (CONV_WIDTH - 1):]
    return (y_p.reshape(bsz, seq, D_MODEL), y_s.reshape(db, t_new, D_MODEL),
            kv_prompt[0], kv_prompt[1], kv_prompt[2], conv_prompt,
            kv_sample[0], kv_sample[1], kv_sample[2], conv_state_s[None])
```

```python
import functools

import jax
import jax.numpy as jnp
from jax import lax
from jax.experimental import pallas as pl
from jax.experimental.pallas import tpu as pltpu

F32 = jnp.float32
BF16 = jnp.bfloat16
I32 = jnp.int32

D_MODEL = 1024
HEAD_DIM = 64
HEADS_PER_GROUP = 4
GROUP_W = HEADS_PER_GROUP * HEAD_DIM
ATTN_PATTERNS = ((128, 1), (512, 4), (2048, 16))
N_GROUPS = len(ATTN_PATTERNS)
ATTN_WIDTH = N_GROUPS * GROUP_W
QKV_W = 3 * GROUP_W
STEPS = 128
ROT_HALF = 8
ROPE_THETA = 500000.0
CONV_CH = 512
CONV_WIDTH = 31
HALO = 32
N_EXPERTS = 32
N_EXPERT_GROUPS = 4
D_EXPERT = 512
DEPTH = 1
DN_ALPHA = (2 * DEPTH) ** 0.25
LN_EPS = 1e-5
SCALE = HEAD_DIM ** -0.5
NEG_INF = -1e30
LANES = 128
ROW_SUB = D_MODEL // LANES
KV_ROWS = 2 * GROUP_W
ROW_TILE = 256
CHUNK = 256
VMEM_LIMIT = 56 * 1024 * 1024

_Q0, _K0, _V0 = 0, ATTN_WIDTH, 2 * ATTN_WIDTH
_UA0 = 3 * ATTN_WIDTH
_UB0 = _UA0 + CONV_CH
_G0 = _UB0 + CONV_CH
IN_WIDTH = _G0 + 2 * D_MODEL


def _cparams(sem):
    return pltpu.CompilerParams(dimension_semantics=sem, vmem_limit_bytes=VMEM_LIMIT)


def _layer_norm(x, g, b):
    mu = jnp.mean(x, -1, keepdims=True)
    xc = x - mu
    var = jnp.mean(xc * xc, -1, keepdims=True)
    return xc * lax.rsqrt(var + LN_EPS) * g + b


def _head_masks(shape, lane_axis):
    lane = lax.broadcasted_iota(I32, shape, lane_axis)
    return [(lane >= h * HEAD_DIM) & (lane < (h + 1) * HEAD_DIM) for h in range(HEADS_PER_GROUP)]


def _store_rows_as_tiles(ref3, val2d, tmp):
    sub = 8
    groups = val2d.shape[0] // sub
    for c in range(ROW_SUB):
        tmp[:, c * sub:(c + 1) * sub, :] = val2d[:, c * LANES:(c + 1) * LANES].reshape(groups, sub, LANES)
    for s in range(sub):
        ref3[pl.ds(s, groups, stride=sub), :, :] = tmp[:, pl.ds(s, ROW_SUB, stride=sub), :]


def _load_tiles_as_rows(ref3, tmp):
    sub = 8
    groups = ref3.shape[0] // sub
    for s in range(sub):
        tmp[:, pl.ds(s, ROW_SUB, stride=sub), :] = ref3[pl.ds(s, groups, stride=sub), :, :]
    return jnp.concatenate([tmp[:, c * sub:(c + 1) * sub, :].reshape(groups * sub, LANES) for c in range(ROW_SUB)],
                           axis=1)


def _rotary_tables(pos):
    inv_freq = ROPE_THETA ** (-jnp.arange(ROT_HALF, dtype=F32) / ROT_HALF)
    ang = pos.astype(F32)[:, None] * inv_freq[None, :]
    cos, sin = jnp.cos(ang), jnp.sin(ang)
    zero = jnp.zeros_like(cos)
    p = pos.shape[0]
    rest = HEAD_DIM - 2 * ROT_HALF
    c_head = jnp.concatenate([cos, cos, jnp.ones((p, rest), F32)], -1)
    sa_head = jnp.concatenate([zero, sin, jnp.zeros((p, rest), F32)], -1)
    sb_head = jnp.concatenate([-sin, zero, jnp.zeros((p, rest), F32)], -1)
    rep = LANES // HEAD_DIM
    return (jnp.tile(c_head, (1, rep)), jnp.tile(sa_head, (1, rep)), jnp.tile(sb_head, (1, rep)))


def _in_proj_kernel(*refs, prompt, tiles_per_seq):
    if prompt:
        (x_ref, w_ref, b_ref, c_ref, sa_ref, sb_ref, wdw_ref, bdw_ref,
         q0_ref, q1_ref, q2_ref, kvf_ref, u_ref, conv_ref, gate_ref, ext_scr, sh_scr, rm_scr) = refs
    else:
        (x_ref, w_ref, b_ref, c_ref, sa_ref, sb_ref, qs_ref, kvf_ref, u_ref, gate_ref) = refs
    tm = x_ref.shape[0]
    if prompt:
        @pl.when(pl.program_id(0) % tiles_per_seq == 0)
        def _():
            ext_scr[0:HALO, :] = jnp.zeros((HALO, CONV_CH), F32)

    xb = x_ref[...].astype(BF16)

    def mm(c0, c1):
        return jnp.dot(xb, w_ref[:, c0:c1], preferred_element_type=F32) + b_ref[:, c0:c1]

    u = mm(_UA0, _UB0) * jax.nn.sigmoid(mm(_UB0, _G0))
    u_ref[...] = u
    if prompt:
        ext_scr[HALO:HALO + tm, :] = u
        off = HALO - (CONV_WIDTH - 1)
        sub = 8
        conv = jnp.broadcast_to(bdw_ref[...], (tm, CONV_CH))
        for s in range(sub):
            n_a = (CONV_WIDTH - s + sub - 1) // sub
            rows = tm + sub * (n_a - 1)
            sh_scr[0:rows, :] = ext_scr[off + s:off + s + rows, :]
            for a in range(n_a):
                j = sub * a + s
                conv = conv + sh_scr[sub * a:sub * a + tm, :] * wdw_ref[j:j + 1, :]
        conv_ref[...] = conv
        ext_scr[0:HALO, :] = ext_scr[tm:tm + HALO, :]

    c, sa, sb = c_ref[...], sa_ref[...], sb_ref[...]

    def rotary(z):
        parts = []
        for j in range(z.shape[1] // LANES):
            zc = z[:, j * LANES:(j + 1) * LANES]
            parts.append(zc * c + pltpu.roll(zc, ROT_HALF, 1) * sa + pltpu.roll(zc, LANES - ROT_HALF, 1) * sb)
        return jnp.concatenate(parts, axis=1)

    q = rotary(mm(_Q0, _K0)) * SCALE
    k = rotary(mm(_K0, _V0))
    v = mm(_V0, _UA0)
    for g in range(N_GROUPS):
        gs = slice(g * GROUP_W, (g + 1) * GROUP_W)
        kvf_ref[:, 2 * g * GROUP_W:(2 * g + 1) * GROUP_W] = k[:, gs]
        kvf_ref[:, (2 * g + 1) * GROUP_W:(2 * g + 2) * GROUP_W] = v[:, gs]
    gate_ref[...] = jax.nn.sigmoid(mm(_G0, IN_WIDTH))

    if not prompt:
        qs_ref[...] = q
        return

    for g, out_ref in enumerate((q0_ref, q1_ref, q2_ref)):
        dil = ATTN_PATTERNS[g][1]
        gs = slice(g * GROUP_W, (g + 1) * GROUP_W)
        qkv = jnp.concatenate([q[:, gs], k[:, gs], v[:, gs]], axis=1)
        if dil == 1:
            out_ref[...] = qkv.astype(BF16)
        else:
            nck = QKV_W // LANES
            for ck in range(nck):
                rm_scr[ck] = qkv[:, ck * LANES:(ck + 1) * LANES]
            for r in range(dil):
                blk = jnp.concatenate([rm_scr[ck, pl.ds(r, tm // dil, stride=dil), :] for ck in range(nck)], axis=1)
                out_ref[:, r * QKV_W:(r + 1) * QKV_W] = blk.astype(BF16)


def _in_proj(x2d, w_bf, b2d, tables, tm, conv_w=None, tiles_per_seq=1):
    n = x2d.shape[0]
    p = tables[0].shape[0]
    prompt = conv_w is not None
    assert n % tm == 0 and p % tm == 0
    tpb = p // tm
    row = lambda i: (i, 0)
    tab = lambda i: (i % tpb, 0)
    whole = lambda i: (0, 0)
    in_specs = [pl.BlockSpec((tm, D_MODEL), row), pl.BlockSpec((D_MODEL, IN_WIDTH), whole),
                pl.BlockSpec((1, IN_WIDTH), whole)] + [pl.BlockSpec((tm, LANES), tab)] * 3
    common_specs = [pl.BlockSpec((tm, 2 * ATTN_WIDTH), row), pl.BlockSpec((tm, CONV_CH), row)]
    common_shapes = [jax.ShapeDtypeStruct((n, 2 * ATTN_WIDTH), F32),
                     jax.ShapeDtypeStruct((n, CONV_CH), F32)]
    gate_spec, gate_shape = pl.BlockSpec((tm, 2 * D_MODEL), row), jax.ShapeDtypeStruct((n, 2 * D_MODEL), F32)
    if prompt:
        args = (x2d, w_bf, b2d, *tables, *conv_w)
        in_specs += [pl.BlockSpec((CONV_WIDTH, CONV_CH), whole), pl.BlockSpec((1, CONV_CH), whole)]
        q_specs = [pl.BlockSpec((tm // d, d * QKV_W), row) for _, d in ATTN_PATTERNS]
        q_shapes = [jax.ShapeDtypeStruct((n // d, d * QKV_W), BF16) for _, d in ATTN_PATTERNS]
        out_specs = q_specs + common_specs + [pl.BlockSpec((tm, CONV_CH), row), gate_spec]
        out_shape = q_shapes + common_shapes + [jax.ShapeDtypeStruct((n, CONV_CH), F32), gate_shape]
        scratch = [pltpu.VMEM((HALO + tm, CONV_CH), F32), pltpu.VMEM((HALO + tm, CONV_CH), F32),
                   pltpu.VMEM((QKV_W // LANES, tm, LANES), F32)]
    else:
        args = (x2d, w_bf, b2d, *tables)
        out_specs = [pl.BlockSpec((tm, ATTN_WIDTH), row)] + common_specs + [gate_spec]
        out_shape = [jax.ShapeDtypeStruct((n, ATTN_WIDTH), F32)] + common_shapes + [gate_shape]
        scratch = []
    return pl.pallas_call(
        functools.partial(_in_proj_kernel, prompt=prompt, tiles_per_seq=tiles_per_seq),
        grid=(n // tm,),
        in_specs=in_specs,
        out_specs=out_specs,
        out_shape=out_shape,
        scratch_shapes=scratch,
        compiler_params=_cparams(("arbitrary",)),
        name="in_proj_prompt" if prompt else "in_proj_sample",
    )(*args)


def _attn_block(q, keys, vals, first):
    masks = _head_masks(q.shape, 1)
    zero = jnp.zeros_like(q)
    q_stack = jnp.concatenate([jnp.where(m, q, zero) for m in masks], axis=0)
    s = lax.dot_general(q_stack, keys, (((1,), (1,)), ((), ())), preferred_element_type=F32)
    qi = lax.broadcasted_iota(I32, s.shape, 0) % STEPS
    kj = lax.broadcasted_iota(I32, s.shape, 1)
    ok = (kj <= qi) if first else ((kj >= qi) & (kj <= qi + STEPS))
    s = jnp.where(ok, s, NEG_INF)
    m = jnp.max(s, -1, keepdims=True)
    p = jnp.exp(s - m)
    l = jnp.sum(p, -1, keepdims=True)
    pv = jnp.dot(p.astype(BF16), vals, preferred_element_type=F32)
    lse = m + jnp.log(l)
    o = jnp.zeros((STEPS, GROUP_W), F32)
    lse_o = jnp.zeros((STEPS, GROUP_W), F32)
    for h, hm in enumerate(masks):
        rs = slice(h * STEPS, (h + 1) * STEPS)
        o = jnp.where(hm, pv[rs] / l[rs], o)
        lse_o = jnp.where(hm, lse[rs], lse_o)
    return o, lse_o


def _prompt_attn_kernel(q0_ref, q1_ref, q2_ref, attn_ref, o_rm1, l_rm1, o_rm2, l_rm2, o_nat, l_nat):
    halves = GROUP_W // LANES

    def put_nat(g, rows, o, lse):
        for hf in range(halves):
            hs = slice(hf * LANES, (hf + 1) * LANES)
            o_nat[g * halves + hf, rows, :] = o[:, hs]
            l_nat[g * halves + hf, rows, :] = lse[:, hs]

    for g, ref in enumerate((q0_ref, q1_ref, q2_ref)):
        dil = ATTN_PATTERNS[g][1]
        o_rm, l_rm = (None, o_rm1, o_rm2)[g], (None, l_rm1, l_rm2)[g]
        rows = ref.shape[0]
        nb = rows // STEPS
        for r in range(dil):
            base = r * QKV_W
            qc, kc, vc = (slice(base + j * GROUP_W, base + (j + 1) * GROUP_W) for j in range(3))
            oc = slice(r * GROUP_W, (r + 1) * GROUP_W)
            o, lse = _attn_block(ref[0:STEPS, qc], ref[0:STEPS, kc], ref[0:STEPS, vc], True)
            if dil == 1:
                put_nat(g, slice(0, STEPS), o, lse)
            else:
                o_rm[0:STEPS, oc] = o
                l_rm[0:STEPS, oc] = lse

            def body(n, carry, ref=ref, qc=qc, kc=kc, vc=vc, oc=oc, dil=dil, g=g, o_rm=o_rm, l_rm=l_rm):
                q0 = pl.multiple_of(n * STEPS, STEPS)
                k0 = pl.multiple_of((n - 1) * STEPS, STEPS)
                o, lse = _attn_block(ref[pl.ds(q0, STEPS), qc], ref[pl.ds(k0, 2 * STEPS), kc],
                                     ref[pl.ds(k0, 2 * STEPS), vc], False)
                if dil == 1:
                    put_nat(g, pl.ds(q0, STEPS), o, lse)
                else:
                    o_rm[pl.ds(q0, STEPS), oc] = o
                    l_rm[pl.ds(q0, STEPS), oc] = lse
                return carry

            if nb > 1:
                lax.fori_loop(1, nb, body, 0, unroll=3)
        if dil > 1:
            for r in range(dil):
                oc = slice(r * GROUP_W, (r + 1) * GROUP_W)
                put_nat(g, pl.ds(r, rows, stride=dil), o_rm[0:rows, oc], l_rm[0:rows, oc])

    def nat(ref, g):
        return jnp.concatenate([ref[g * halves + hf] for hf in range(halves)], axis=1)

    lses = [nat(l_nat, g) for g in range(N_GROUPS)]
    mx = jnp.maximum(jnp.maximum(lses[0], lses[1]), lses[2])
    ws = [jnp.exp(ls - mx) for ls in lses]
    attn_ref[...] = (nat(o_nat, 0) * ws[0] + nat(o_nat, 1) * ws[1] + nat(o_nat, 2) * ws[2]) / (ws[0] + ws[1] + ws[2])


def _prompt_attn(qkv_groups, bsz, seq):
    row = lambda b: (b, 0)
    in_specs = [pl.BlockSpec((seq // d, d * QKV_W), row) for _, d in ATTN_PATTERNS]
    return pl.pallas_call(
        _prompt_attn_kernel,
        grid=(bsz,),
        in_specs=in_specs,
        out_specs=pl.BlockSpec((seq, GROUP_W), row),
        out_shape=jax.ShapeDtypeStruct((bsz * seq, GROUP_W), F32),
        scratch_shapes=[pltpu.VMEM((seq // d, d * GROUP_W), F32) for _, d in ATTN_PATTERNS[1:] for _ in range(2)]
        + [pltpu.VMEM((N_GROUPS * (GROUP_W // LANES), seq, LANES), F32) for _ in range(2)],
        compiler_params=_cparams(("parallel",)),
        name="prompt_attn",
    )(*qkv_groups)


def _sample_attn_kernel(q_ref, c0_ref, c1_ref, c2_ref, n0_ref, n1_ref, n2_ref, attn_ref, o0_ref, o1_ref, o2_ref,
                        *, t_new):
    q = q_ref[0]
    outs, lses = [], []
    for g, (cref, nref, oref) in enumerate(((c0_ref, n0_ref, o0_ref), (c1_ref, n1_ref, o1_ref),
                                            (c2_ref, n2_ref, o2_ref))):
        dil = ATTN_PATTERNS[g][1]
        wb = cref.shape[2]
        new_t = jnp.concatenate([nref[0], jnp.zeros((LANES - nref.shape[1], KV_ROWS), F32)], axis=0).T
        full = jnp.concatenate([cref[0], new_t], axis=1)
        oref[0] = full[:, t_new:wb + t_new]
        k_t = full[0:GROUP_W].astype(BF16)
        v_t = full[GROUP_W:KV_ROWS].astype(BF16)
        qg = q[:, g * GROUP_W:(g + 1) * GROUP_W]
        masks = _head_masks(qg.shape, 1)
        q_stack = jnp.concatenate([jnp.where(m, qg, 0.0) for m in masks], axis=0).astype(BF16)
        s = jnp.dot(q_stack, k_t, preferred_element_type=F32)
        r = lax.broadcasted_iota(I32, s.shape, 1)
        t = lax.broadcasted_iota(I32, s.shape, 0) % t_new
        ok = (r >= t) & (r <= wb + t) & (((r - t) & (dil - 1)) == 0)
        s = jnp.where(ok, s, NEG_INF)
        m = jnp.max(s, -1, keepdims=True)
        p = jnp.exp(s - m)
        l = jnp.sum(p, -1, keepdims=True)
        pv = lax.dot_general(p.astype(BF16), v_t, (((1,), (1,)), ((), ())), preferred_element_type=F32)
        lse = m + jnp.log(l)
        o = jnp.zeros((t_new, GROUP_W), F32)
        lse_o = jnp.zeros((t_new, GROUP_W), F32)
        for h, hm in enumerate(masks):
            rs = slice(h * t_new, (h + 1) * t_new)
            o = jnp.where(hm, pv[rs] / l[rs], o)
            lse_o = jnp.where(hm, lse[rs], lse_o)
        outs.append(o)
        lses.append(lse_o)
    mx = jnp.maximum(jnp.maximum(lses[0], lses[1]), lses[2])
    ws = [jnp.exp(ls - mx) for ls in lses]
    attn_ref[0] = (outs[0] * ws[0] + outs[1] * ws[1] + outs[2] * ws[2]) / (ws[0] + ws[1] + ws[2])


def _sample_attn(q3, caches_t, news):
    db, t_new, _ = q3.shape
    for c, (_, dil) in zip(caches_t, ATTN_PATTERNS):
        assert c.shape[2] == STEPS * dil and t_new <= LANES and dil & (dil - 1) == 0
    blk = lambda b: (b, 0, 0)
    cache_specs = [pl.BlockSpec((1, KV_ROWS, c.shape[2]), blk) for c in caches_t]
    outs = pl.pallas_call(
        functools.partial(_sample_attn_kernel, t_new=t_new),
        grid=(db,),
        in_specs=[pl.BlockSpec((1, t_new, ATTN_WIDTH), blk)] + cache_specs
        + [pl.BlockSpec((1, news[0].shape[1], KV_ROWS), blk)] * 3,
        out_specs=[pl.BlockSpec((1, t_new, GROUP_W), blk)] + cache_specs,
        out_shape=[jax.ShapeDtypeStruct((db, t_new, GROUP_W), F32)]
        + [jax.ShapeDtypeStruct(c.shape, c.dtype) for c in caches_t],
        compiler_params=_cparams(("parallel",)),
        name="sample_attn",
    )(q3, *caches_t, *news)
    return outs[0], outs[1:]


def _kv_window_kernel(x_ref, o_ref):
    o_ref[0] = x_ref[0].T


def _kv_window(kvf3, g, window):
    bsz, seq, _ = kvf3.shape
    w = min(window, seq)
    rows = min(w, 512)
    assert w % rows == 0 and (seq - w) % rows == 0
    first = (seq - w) // rows
    return pl.pallas_call(
        _kv_window_kernel,
        grid=(bsz, w // rows),
        in_specs=[pl.BlockSpec((1, rows, KV_ROWS), lambda b, n: (b, first + n, g))],
        out_specs=pl.BlockSpec((1, KV_ROWS, rows), lambda b, n: (b, 0, n)),
        out_shape=jax.ShapeDtypeStruct((bsz, KV_ROWS, w), F32),
        compiler_params=_cparams(("parallel", "parallel")),
        name=f"kv_window_g{g}",
    )(kvf3)


def _sample_conv_kernel(state_ref, u_ref, w_ref, b_ref, conv_ref, sout_ref):
    hist = state_ref.shape[1]
    t_new = u_ref.shape[1]
    bb = u_ref.shape[0]
    for t in range(t_new):
        acc = jnp.broadcast_to(b_ref[...], (bb, CONV_CH))
        for j in range(CONV_WIDTH):
            r = t + j
            row = state_ref[:, r, :] if r < hist else u_ref[:, r - hist, :]
            acc = acc + row * w_ref[j:j + 1, :]
        conv_ref[:, t, :] = acc
    sout_ref[:, 0:hist - t_new, :] = state_ref[:, t_new:hist, :]
    sout_ref[:, hist - t_new:hist, :] = u_ref[...]


def _sample_conv(state, u3, w_dw, b_dw, bb=32):
    db, hist, _ = state.shape
    t_new = u3.shape[1]
    assert db % bb == 0 and hist == CONV_WIDTH - 1
    blk = lambda i: (i, 0, 0)
    whole = lambda i: (0, 0)
    return pl.pallas_call(
        _sample_conv_kernel,
        grid=(db // bb,),
        in_specs=[pl.BlockSpec((bb, hist, CONV_CH), blk), pl.BlockSpec((bb, t_new, CONV_CH), blk),
                  pl.BlockSpec((CONV_WIDTH, CONV_CH), whole), pl.BlockSpec((1, CONV_CH), whole)],
        out_specs=[pl.BlockSpec((bb, t_new, CONV_CH), blk), pl.BlockSpec((bb, hist, CONV_CH), blk)],
        out_shape=[jax.ShapeDtypeStruct((db, t_new, CONV_CH), F32), jax.ShapeDtypeStruct((db, hist, CONV_CH), F32)],
        compiler_params=_cparams(("parallel",)),
        name="sample_conv",
    )(state, u3, w_dw, b_dw)


ROUTE_LANES = 2 * LANES


def _merge_kernel(xp_ref, xs_ref, cp_ref, cs_ref, ap_ref, as_ref, gp_ref, gs_ref, clg_ref, clb_ref, wco_ref, bco_ref,
                  wao_ref, wo_ref, g1_ref, b1_ref, wr_ref, br_ref, h_ref, ri_ref, rg_ref, cnt_ref, base_scr, tile_scr,
                  *, prompt_tiles):
    i = pl.program_id(0)
    tm = xp_ref.shape[0]
    is_p = i < prompt_tiles

    @pl.when(i == 0)
    def _():
        base_scr[...] = jnp.zeros_like(base_scr)

    x = jnp.where(is_p, xp_ref[...], xs_ref[...])
    conv = jnp.where(is_p, cp_ref[...], cs_ref[...])
    attn = jnp.where(is_p, ap_ref[...], as_ref[...])
    gate = jnp.where(is_p, gp_ref[...], gs_ref[...])

    cn = _layer_norm(conv, clg_ref[...], clb_ref[...])
    branch_c = jnp.dot(jax.nn.silu(cn).astype(BF16), wco_ref[...], preferred_element_type=F32) + bco_ref[...]
    branch_a = jnp.dot(attn.astype(BF16), wao_ref[...], preferred_element_type=F32)
    mixed = gate[:, 0:D_MODEL] * branch_a + gate[:, D_MODEL:2 * D_MODEL] * branch_c
    y = jnp.dot(mixed.astype(BF16), wo_ref[...], preferred_element_type=F32)
    h = _layer_norm(DN_ALPHA * x + y, g1_ref[...], b1_ref[...])
    _store_rows_as_tiles(h_ref, h, tile_scr)

    logits = jnp.dot(h.astype(BF16), wr_ref[...], preferred_element_type=F32) + br_ref[...]
    lane = lax.broadcasted_iota(I32, (tm, LANES), 1)
    big = jnp.int32(LANES)
    gl = jnp.where(lane < N_EXPERT_GROUPS, logits[:, 0:LANES], NEG_INF)
    gmax = jnp.max(gl, -1, keepdims=True)
    g_sel = jnp.min(jnp.where(gl == gmax, lane, big), -1, keepdims=True)
    p_g = 1.0 / jnp.sum(jnp.exp(gl - gmax), -1, keepdims=True)
    in_group = (lane < N_EXPERTS) & (lax.shift_right_logical(lane, 3) == g_sel)
    el = jnp.where(in_group, logits[:, LANES:2 * LANES], NEG_INF)
    v1 = jnp.max(el, -1, keepdims=True)
    i1 = jnp.min(jnp.where(el == v1, lane, big), -1, keepdims=True)
    el2 = jnp.where(lane == i1, NEG_INF, el)
    v2 = jnp.max(el2, -1, keepdims=True)
    i2 = jnp.min(jnp.where(el2 == v2, lane, big), -1, keepdims=True)
    e21 = jnp.exp(v2 - v1)
    gate1 = p_g / (1.0 + e21)
    gate2 = p_g * e21 / (1.0 + e21)

    oh = ((lane == i1) | (lane == i2)).astype(BF16)
    ri_ = lax.broadcasted_iota(I32, (tm, tm), 0)
    ci_ = lax.broadcasted_iota(I32, (tm, tm), 1)
    tri = (ci_ < ri_).astype(BF16)
    before = jnp.dot(tri, oh, preferred_element_type=F32) + base_scr[...]
    rank1 = jnp.sum(jnp.where(lane == i1, before, 0.0), -1, keepdims=True).astype(I32)
    rank2 = jnp.sum(jnp.where(lane == i2, before, 0.0), -1, keepdims=True).astype(I32)
    base_scr[...] = base_scr[...] + jnp.sum(oh.astype(F32), 0, keepdims=True)
    cnt_ref[...] = base_scr[...]

    ri_ref[...] = jnp.where(lane == 0, i1, jnp.where(lane == 1, i2, jnp.where(lane == 2, rank1,
                            jnp.where(lane == 3, rank2, 0))))
    rg_ref[...] = jnp.where(lane == 0, gate1, jnp.where(lane == 1, gate2, 0.0))


def _merge(acts_p, acts_s, weights, tm):
    n_p, n_s = acts_p[0].shape[0], acts_s[0].shape[0]
    assert n_p % tm == 0 and n_s % tm == 0
    pt = n_p // tm
    n_all = n_p + n_s
    prow = lambda i: (jnp.minimum(i, pt - 1), 0)
    srow = lambda i: (jnp.maximum(i - pt, 0), 0)
    row = lambda i: (i, 0)
    whole = lambda i: (0, 0)
    act_args, act_specs = [], []
    for a_p, a_s in zip(acts_p, acts_s):
        act_args += [a_p, a_s]
        act_specs += [pl.BlockSpec((tm, a_p.shape[1]), prow), pl.BlockSpec((tm, a_s.shape[1]), srow)]
    return pl.pallas_call(
        functools.partial(_merge_kernel, prompt_tiles=pt),
        grid=(n_all // tm,),
        in_specs=act_specs + [pl.BlockSpec(w.shape, whole) for w in weights],
        out_specs=[pl.BlockSpec((tm, ROW_SUB, LANES), lambda i: (i, 0, 0)), pl.BlockSpec((tm, LANES), row),
                   pl.BlockSpec((tm, LANES), row), pl.BlockSpec((1, LANES), whole)],
        out_shape=[jax.ShapeDtypeStruct((n_all, ROW_SUB, LANES), F32), jax.ShapeDtypeStruct((n_all, LANES), I32),
                   jax.ShapeDtypeStruct((n_all, LANES), F32), jax.ShapeDtypeStruct((1, LANES), F32)],
        scratch_shapes=[pltpu.VMEM((1, LANES), F32), pltpu.VMEM((tm // 8, 8 * ROW_SUB, LANES), F32)],
        compiler_params=_cparams(("arbitrary",)),
        name="merge",
    )(*act_args, *weights)


def _dispatch_kernel(bounds_ref, nv_ref, dest_ref, h_ref, xg_any, hbuf, zero_scr, sem_z, sem_r):
    i = pl.program_id(0)
    tm = h_ref.shape[0]
    n_chunks = xg_any.shape[0] // CHUNK
    slot = i % 2

    def zero_chunk(row0):
        return pltpu.make_async_copy(zero_scr, xg_any.at[pl.ds(pl.multiple_of(row0, CHUNK), CHUNK)], sem_z)

    @pl.when(i == 0)
    def _():
        zero_scr[...] = jnp.zeros_like(zero_scr)
        for e in range(N_EXPERTS):
            @pl.when(bounds_ref[e + 1] > bounds_ref[e])
            def _():
                zero_chunk(bounds_ref[e + 1] - CHUNK).start()

        def start_tail(c, carry):
            zero_chunk(c * CHUNK).start()
            return carry

        lax.fori_loop(nv_ref[0], n_chunks, start_tail, 0)
        for e in range(N_EXPERTS):
            @pl.when(bounds_ref[e + 1] > bounds_ref[e])
            def _():
                zero_chunk(0).wait()

        def wait_tail(c, carry):
            zero_chunk(0).wait()
            return carry

        lax.fori_loop(nv_ref[0], n_chunks, wait_tail, 0)

    def wait_rows(sl):
        for _ in range(2 * tm):
            pltpu.make_async_copy(hbuf.at[0, 0], xg_any.at[0], sem_r.at[sl]).wait()

    @pl.when(i >= 2)
    def _():
        wait_rows(slot)

    hbuf[slot] = h_ref[...]
    for r in range(2 * tm):
        pltpu.make_async_copy(hbuf.at[slot, r % tm], xg_any.at[dest_ref[0, 0, r]],
                              sem_r.at[slot]).start(priority=r % 2)

    @pl.when(i == pl.num_programs(0) - 1)
    def _():
        wait_rows(slot)

        @pl.when(i >= 1)
        def _():
            wait_rows(1 - slot)


def _dispatch(bounds, n_valid, dest3, h3, n_rows, tm):
    n = h3.shape[0]
    return pl.pallas_call(
        _dispatch_kernel,
        grid_spec=pltpu.PrefetchScalarGridSpec(
            num_scalar_prefetch=2,
            grid=(n // tm,),
            in_specs=[pl.BlockSpec((1, 1, 2 * tm), lambda i, b, nv: (i, 0, 0), memory_space=pltpu.SMEM),
                      pl.BlockSpec((tm, ROW_SUB, LANES), lambda i, b, nv: (i, 0, 0))],
            out_specs=pl.BlockSpec(memory_space=pl.ANY),
            scratch_shapes=[pltpu.VMEM((2, tm, ROW_SUB, LANES), F32), pltpu.VMEM((CHUNK, ROW_SUB, LANES), F32),
                            pltpu.SemaphoreType.DMA(()), pltpu.SemaphoreType.DMA((2,))],
        ),
        out_shape=jax.ShapeDtypeStruct((n_rows, ROW_SUB, LANES), F32),
        compiler_params=_cparams(("arbitrary",)),
        name="moe_dispatch",
    )(bounds, n_valid, dest3, h3)


def _experts_kernel(ce_ref, nv_ref, x_ref, wg_ref, wu_ref, wd_ref, y_ref, wg_scr, wu_scr, wd_scr, tile_scr):
    c = pl.program_id(0)
    nv = nv_ref[0]

    @pl.when((c < nv) & ((c == 0) | (ce_ref[c] != ce_ref[jnp.maximum(c - 1, 0)])))
    def _():
        wg_scr[...] = wg_ref[0].astype(BF16)
        wu_scr[...] = wu_ref[0].astype(BF16)
        wd_scr[...] = wd_ref[0].astype(BF16)

    @pl.when(c < nv)
    def _():
        x = _load_tiles_as_rows(x_ref, tile_scr).astype(BF16)
        a = jnp.dot(x, wg_scr[...], preferred_element_type=F32)
        b = jnp.dot(x, wu_scr[...], preferred_element_type=F32)
        y = jnp.dot((jax.nn.silu(a) * b).astype(BF16), wd_scr[...], preferred_element_type=F32)
        _store_rows_as_tiles(y_ref, y, tile_scr)

    @pl.when(c >= nv)
    def _():
        y_ref[...] = jnp.zeros_like(y_ref)


def _experts(chunk_expert, n_valid, xg, w_eg, w_eu, w_ed):
    n_chunks = xg.shape[0] // CHUNK
    last = lambda nv: jnp.maximum(nv[0] - 1, 0)
    wsel = lambda c, ce, nv: (ce[jnp.minimum(c, last(nv))], 0, 0)
    return pl.pallas_call(
        _experts_kernel,
        grid_spec=pltpu.PrefetchScalarGridSpec(
            num_scalar_prefetch=2,
            grid=(n_chunks,),
            in_specs=[pl.BlockSpec((CHUNK, ROW_SUB, LANES), lambda c, ce, nv: (jnp.minimum(c, last(nv)), 0, 0)),
                      pl.BlockSpec((1, D_MODEL, D_EXPERT), wsel), pl.BlockSpec((1, D_MODEL, D_EXPERT), wsel),
                      pl.BlockSpec((1, D_EXPERT, D_MODEL), wsel)],
            out_specs=pl.BlockSpec((CHUNK, ROW_SUB, LANES), lambda c, ce, nv: (c, 0, 0)),
            scratch_shapes=[pltpu.VMEM((D_MODEL, D_EXPERT), BF16), pltpu.VMEM((D_MODEL, D_EXPERT), BF16),
                            pltpu.VMEM((D_EXPERT, D_MODEL), BF16),
                            pltpu.VMEM((CHUNK // 8, 8 * ROW_SUB, LANES), F32)],
        ),
        out_shape=jax.ShapeDtypeStruct(xg.shape, F32),
        compiler_params=_cparams(("arbitrary",)),
        name="moe_experts",
    )(chunk_expert, n_valid, xg, w_eg, w_eu, w_ed)


def _combine_kernel(dest_ref, dest_next_ref, h_ref, rg_ref, g2_ref, b2_ref, yg_any, yp_ref, ys_ref,
                    ybuf, t0_scr, t1_scr, t2_scr, sem, *, prompt_tiles):
    i = pl.program_id(0)
    tm = h_ref.shape[0]
    slot = i % 2

    def start_gather(src_ref, sl):
        for r in range(2 * tm):
            pltpu.make_async_copy(yg_any.at[src_ref[0, 0, r]], ybuf.at[sl, r], sem.at[sl]).start(priority=r % 2)

    @pl.when(i == 0)
    def _():
        start_gather(dest_ref, 0)

    @pl.when(i + 1 < pl.num_programs(0))
    def _():
        start_gather(dest_next_ref, 1 - slot)

    for _ in range(2 * tm):
        pltpu.make_async_copy(yg_any.at[0], ybuf.at[0, 0], sem.at[slot]).wait()
    moe = (_load_tiles_as_rows(ybuf.at[slot, pl.ds(0, tm)], t1_scr) * rg_ref[:, 0:1]
           + _load_tiles_as_rows(ybuf.at[slot, pl.ds(tm, tm)], t2_scr) * rg_ref[:, 1:2])
    out = _layer_norm(DN_ALPHA * _load_tiles_as_rows(h_ref, t0_scr) + moe, g2_ref[...], b2_ref[...])

    @pl.when(i < prompt_tiles)
    def _():
        yp_ref[...] = out

    @pl.when(i >= prompt_tiles)
    def _():
        ys_ref[...] = out


def _combine(dest3, h3, yg, route_g, ln2_g, ln2_b, n_prompt, tm):
    n = h3.shape[0]
    pt, nt = n_prompt // tm, n // tm
    row = lambda i: (i, 0)
    whole = lambda i: (0, 0)
    return pl.pallas_call(
        functools.partial(_combine_kernel, prompt_tiles=pt),
        grid=(nt,),
        in_specs=[pl.BlockSpec((1, 1, 2 * tm), lambda i: (i, 0, 0), memory_space=pltpu.SMEM),
                  pl.BlockSpec((1, 1, 2 * tm), lambda i: (jnp.minimum(i + 1, nt - 1), 0, 0), memory_space=pltpu.SMEM),
                  pl.BlockSpec((tm, ROW_SUB, LANES), lambda i: (i, 0, 0)), pl.BlockSpec((tm, LANES), row),
                  pl.BlockSpec((1, D_MODEL), whole), pl.BlockSpec((1, D_MODEL), whole),
                  pl.BlockSpec(memory_space=pl.ANY)],
        out_specs=[pl.BlockSpec((tm, D_MODEL), lambda i: (jnp.minimum(i, pt - 1), 0)),
                   pl.BlockSpec((tm, D_MODEL), lambda i: (jnp.maximum(i - pt, 0), 0))],
        out_shape=[jax.ShapeDtypeStruct((n_prompt, D_MODEL), F32), jax.ShapeDtypeStruct((n - n_prompt, D_MODEL), F32)],
        scratch_shapes=[pltpu.VMEM((2, 2 * tm, ROW_SUB, LANES), F32)]
        + [pltpu.VMEM((tm // 8, 8 * ROW_SUB, LANES), F32) for _ in range(3)] + [pltpu.SemaphoreType.DMA((2,))],
        compiler_params=_cparams(("arbitrary",)),
        name="moe_combine",
    )(dest3, dest3, h3, route_g, ln2_g, ln2_b, yg)


def kernel(x_prompt, x_sample, cache_kv_w128, cache_kv_w512, cache_kv_w2048, state_conv, w_in, b_in, w_dw, b_dw, conv_ln_g, conv_ln_b, w_conv_out, b_conv_out, w_attn_out, w_o, ln1_g, ln1_b, w_router_group, b_router_group, w_router_expert, b_router_expert, w_expert_gate, w_expert_up, w_expert_down, ln2_g, ln2_b):
    assert w_in.shape[0] == DEPTH == 1
    bsz, seq, _ = x_prompt.shape
    db, t_new, _ = x_sample.shape
    caches = (cache_kv_w128, cache_kv_w512, cache_kv_w2048)
    past = cache_kv_w2048.shape[2]
    n_p, n_s = bsz * seq, db * t_new
    n_all = n_p + n_s
    tm = ROW_TILE
    kvs = (HEADS_PER_GROUP, HEAD_DIM)

    w_bf = w_in[0].astype(BF16)
    xp2, xs2 = x_prompt.reshape(n_p, D_MODEL), x_sample.reshape(n_s, D_MODEL)
    q0, q1, q2, kvf_p, u_p, conv_p, gate_p = _in_proj(xp2, w_bf, b_in, _rotary_tables(jnp.arange(seq)), tm,
                                                     conv_w=(w_dw[0], b_dw), tiles_per_seq=seq // tm)
    pos_s = past + jnp.arange(n_s) % t_new
    q_s, kvf_s, u_s, gate_s = _in_proj(xs2, w_bf, b_in, _rotary_tables(pos_s), tm)

    attn_p = _prompt_attn((q0, q1, q2), bsz, seq)

    kv_new = kvf_s.reshape(db, t_new, N_GROUPS, KV_ROWS)
    caches_t = [jnp.transpose(c[0], (0, 2, 3, 4, 1)).reshape(db, KV_ROWS, c.shape[2]) for c in caches]
    news = [jnp.pad(kv_new[:, :, g], ((0, 0), (0, -t_new % 8), (0, 0))) for g in range(N_GROUPS)]
    attn_s, kv_sample_t = _sample_attn(q_s.reshape(db, t_new, ATTN_WIDTH), caches_t, news)
    kv_sample = [jnp.transpose(o.reshape(db, 2, *kvs, o.shape[2]), (0, 4, 1, 2, 3))[None] for o in kv_sample_t]

    conv_s, conv_state_s = _sample_conv(state_conv[0], u_s.reshape(db, t_new, CONV_CH), w_dw[0], b_dw)

    w_r = jnp.zeros((D_MODEL, ROUTE_LANES), F32)
    w_r = w_r.at[:, 0:N_EXPERT_GROUPS].set(w_router_group[0]).at[:, LANES:LANES + N_EXPERTS].set(w_router_expert[0])
    b_r = jnp.zeros((1, ROUTE_LANES), F32)
    b_r = b_r.at[:, 0:N_EXPERT_GROUPS].set(b_router_group).at[:, LANES:LANES + N_EXPERTS].set(b_router_expert)
    weights = [conv_ln_g, conv_ln_b, w_conv_out[0].astype(BF16), b_conv_out, w_attn_out[0].astype(BF16),
               w_o[0].astype(BF16), ln1_g, ln1_b, w_r.astype(BF16), b_r]
    acts_p = (xp2, conv_p, attn_p, gate_p)
    acts_s = (xs2, conv_s.reshape(n_s, CONV_CH), attn_s.reshape(n_s, GROUP_W), gate_s)
    h3, route_i, route_g, counts = _merge(acts_p, acts_s, weights, tm)

    cnt = counts[0, :N_EXPERTS].astype(I32)
    padded = ((cnt + CHUNK - 1) // CHUNK) * CHUNK
    pad_end = jnp.cumsum(padded)
    bounds = jnp.concatenate([jnp.zeros((1,), I32), pad_end]).astype(I32)
    n_chunks = (2 * n_all) // CHUNK + N_EXPERTS
    chunk_row0 = jnp.arange(n_chunks, dtype=I32) * CHUNK
    chunk_expert = jnp.minimum(jnp.sum((pad_end[None, :] <= chunk_row0[:, None]).astype(I32), axis=1),
                               N_EXPERTS - 1).astype(I32)
    n_valid = (pad_end[-1:] // CHUNK).astype(I32)

    e_onehot = route_i[:, 0:2, None] == jnp.arange(N_EXPERTS, dtype=I32)[None, None, :]
    dest = jnp.sum(jnp.where(e_onehot, bounds[None, None, :N_EXPERTS], 0), axis=-1) + route_i[:, 2:4]
    dest3 = jnp.transpose(dest.reshape(n_all // tm, tm, 2), (0, 2, 1)).reshape(n_all // tm, 1, 2 * tm)
    xg = _dispatch(bounds, n_valid, dest3, h3, n_chunks * CHUNK, tm)
    yg = _experts(chunk_expert, n_valid, xg, w_expert_gate[0], w_expert_up[0], w_expert_down[0])
    y_p, y_s = _combine(dest3, h3, yg, route_g, ln2_g, ln2_b, n_p, tm)

    kvf3 = kvf_p.reshape(bsz, seq, N_GROUPS * KV_ROWS)
    kv_prompt = []
    for g, (w, _) in enumerate(ATTN_PATTERNS):
        win_t = _kv_window(kvf3, g, w)
        kv_prompt.append(jnp.transpose(win_t.reshape(bsz, 2, *kvs, win_t.shape[2]), (0, 4, 1, 2, 3))[None])
    conv_prompt = u_p.reshape(1, bsz, seq, CONV_CH)[:, :, seq - (CONV_WIDTH - 1):]
    return (y_p.reshape(bsz, seq, D_MODEL), y_s.reshape(db, t_new, D_MODEL),
            kv_prompt[0], kv_prompt[1], kv_prompt[2], conv_prompt,
            kv_sample[0], kv_sample[1], kv_sample[2], conv_state_s[None])
```

```python
import functools

import jax
import jax.numpy as jnp
import numpy as np
from jax import lax
from jax.experimental import pallas as pl
from jax.experimental.pallas import tpu as pltpu

F32 = jnp.float32
BF16 = jnp.bfloat16
I32 = jnp.int32

D_MODEL = 1024
HEAD_DIM = 64
HEADS_PER_GROUP = 4
GROUP_W = HEADS_PER_GROUP * HEAD_DIM
ATTN_PATTERNS = ((128, 1), (512, 4), (2048, 16))
N_GROUPS = len(ATTN_PATTERNS)
ATTN_WIDTH = N_GROUPS * GROUP_W
QKV_W = 3 * GROUP_W
STEPS = 128
ROT_HALF = 8
ROPE_THETA = 500000.0
CONV_CH = 512
CONV_WIDTH = 31
HALO = 32
N_EXPERTS = 32
N_EXPERT_GROUPS = 4
D_EXPERT = 512
DEPTH = 1
DN_ALPHA = (2 * DEPTH) ** 0.25
LN_EPS = 1e-5
SCALE = HEAD_DIM ** -0.5
NEG_INF = -1e30
LANES = 128
ROW_SUB = D_MODEL // LANES
KV_ROWS = 2 * GROUP_W
ROW_TILE = 256
CHUNK = 256
VMEM_LIMIT = 56 * 1024 * 1024

_Q0, _K0, _V0 = 0, ATTN_WIDTH, 2 * ATTN_WIDTH
_UA0 = 3 * ATTN_WIDTH
_UB0 = _UA0 + CONV_CH
_G0 = _UB0 + CONV_CH
IN_WIDTH = _G0 + 2 * D_MODEL


def _cparams(sem):
    return pltpu.CompilerParams(dimension_semantics=sem, vmem_limit_bytes=VMEM_LIMIT)


def _layer_norm(x, g, b):
    mu = jnp.mean(x, -1, keepdims=True)
    xc = x - mu
    var = jnp.mean(xc * xc, -1, keepdims=True)
    return xc * lax.rsqrt(var + LN_EPS) * g + b


def _head_masks(shape, lane_axis):
    lane = lax.broadcasted_iota(I32, shape, lane_axis)
    return [(lane >= h * HEAD_DIM) & (lane < (h + 1) * HEAD_DIM) for h in range(HEADS_PER_GROUP)]


def _store_rows_as_tiles(ref3, val2d, tmp):
    sub = 8
    groups = val2d.shape[0] // sub
    for c in range(ROW_SUB):
        tmp[:, c * sub:(c + 1) * sub, :] = val2d[:, c * LANES:(c + 1) * LANES].reshape(groups, sub, LANES)
    for s in range(sub):
        ref3[pl.ds(s, groups, stride=sub), :, :] = tmp[:, pl.ds(s, ROW_SUB, stride=sub), :]


def _load_tiles_as_rows(ref3, tmp):
    sub = 8
    groups = ref3.shape[0] // sub
    for s in range(sub):
        tmp[:, pl.ds(s, ROW_SUB, stride=sub), :] = ref3[pl.ds(s, groups, stride=sub), :, :]
    return jnp.concatenate([tmp[:, c * sub:(c + 1) * sub, :].reshape(groups * sub, LANES) for c in range(ROW_SUB)],
                           axis=1)


def _rotary_tables(pos):
    pos = np.asarray(pos, np.float64)
    inv_freq = ROPE_THETA ** (-np.arange(ROT_HALF, dtype=np.float64) / ROT_HALF)
    ang = pos[:, None] * inv_freq[None, :]
    cos, sin = np.cos(ang), np.sin(ang)
    zero = np.zeros_like(cos)
    p = pos.shape[0]
    rest = HEAD_DIM - 2 * ROT_HALF
    c_head = np.concatenate([cos, cos, np.ones((p, rest))], -1)
    sa_head = np.concatenate([zero, sin, np.zeros((p, rest))], -1)
    sb_head = np.concatenate([-sin, zero, np.zeros((p, rest))], -1)
    rep = LANES // HEAD_DIM
    return tuple(jnp.asarray(np.tile(t, (1, rep)), F32) for t in (c_head, sa_head, sb_head))


def _in_proj_kernel(*refs, prompt, tiles_per_seq):
    if prompt:
        (x_ref, w_ref, b_ref, c_ref, sa_ref, sb_ref, wdw_ref, bdw_ref,
         q0_ref, q1_ref, q2_ref, kvf_ref, u_ref, conv_ref, gate_ref, ext_scr, sh_scr, rm_scr) = refs
    else:
        (x_ref, w_ref, b_ref, c_ref, sa_ref, sb_ref, qs_ref, kvf_ref, u_ref, gate_ref) = refs
    tm = x_ref.shape[0]
    if prompt:
        @pl.when(pl.program_id(0) % tiles_per_seq == 0)
        def _():
            ext_scr[0:HALO, :] = jnp.zeros((HALO, CONV_CH), F32)

    xb = x_ref[...].astype(BF16)

    def mm(c0, c1):
        return jnp.dot(xb, w_ref[:, c0:c1], preferred_element_type=F32) + b_ref[:, c0:c1]

    u = mm(_UA0, _UB0) * jax.nn.sigmoid(mm(_UB0, _G0))
    u_ref[...] = u
    if prompt:
        ext_scr[HALO:HALO + tm, :] = u
        off = HALO - (CONV_WIDTH - 1)
        sub = 8
        for s in range(sub):
            rows = tm + sub * ((CONV_WIDTH - s + sub - 1) // sub - 1)
            sh_scr[s, 0:rows, :] = ext_scr[off + s:off + s + rows, :]
        rb = 64
        for cc in range(CONV_CH // LANES):
            cs = slice(cc * LANES, (cc + 1) * LANES)
            for r0 in range(0, tm, rb):
                acc = jnp.broadcast_to(bdw_ref[:, cs], (rb, LANES))
                for j in range(CONV_WIDTH):
                    a0 = sub * (j // sub) + r0
                    acc = acc + sh_scr[j % sub, a0:a0 + rb, cs] * wdw_ref[j:j + 1, cs]
                conv_ref[r0:r0 + rb, cs] = acc
        ext_scr[0:HALO, :] = ext_scr[tm:tm + HALO, :]

    c, sa, sb = c_ref[...], sa_ref[...], sb_ref[...]

    def rotary(z):
        parts = []
        for j in range(z.shape[1] // LANES):
            zc = z[:, j * LANES:(j + 1) * LANES]
            parts.append(zc * c + pltpu.roll(zc, ROT_HALF, 1) * sa + pltpu.roll(zc, LANES - ROT_HALF, 1) * sb)
        return jnp.concatenate(parts, axis=1)

    q = rotary(mm(_Q0, _K0)) * SCALE
    k = rotary(mm(_K0, _V0))
    v = mm(_V0, _UA0)
    for g in range(N_GROUPS):
        gs = slice(g * GROUP_W, (g + 1) * GROUP_W)
        kvf_ref[:, 2 * g * GROUP_W:(2 * g + 1) * GROUP_W] = k[:, gs]
        kvf_ref[:, (2 * g + 1) * GROUP_W:(2 * g + 2) * GROUP_W] = v[:, gs]
    gate_ref[...] = jax.nn.sigmoid(mm(_G0, IN_WIDTH))

    if not prompt:
        qs_ref[...] = q
        return

    for g, out_ref in enumerate((q0_ref, q1_ref, q2_ref)):
        dil = ATTN_PATTERNS[g][1]
        gs = slice(g * GROUP_W, (g + 1) * GROUP_W)
        qkv = jnp.concatenate([q[:, gs], k[:, gs], v[:, gs]], axis=1)
        if dil == 1:
            out_ref[...] = qkv.astype(BF16)
        else:
            nck = QKV_W // LANES
            for ck in range(nck):
                rm_scr[ck] = qkv[:, ck * LANES:(ck + 1) * LANES]
            for r in range(dil):
                blk = jnp.concatenate([rm_scr[ck, pl.ds(r, tm // dil, stride=dil), :] for ck in range(nck)], axis=1)
                out_ref[:, r * QKV_W:(r + 1) * QKV_W] = blk.astype(BF16)


def _in_proj(x2d, w_bf, b2d, tables, tm, conv_w=None, tiles_per_seq=1):
    n = x2d.shape[0]
    p = tables[0].shape[0]
    prompt = conv_w is not None
    assert n % tm == 0 and p % tm == 0
    tpb = p // tm
    row = lambda i: (i, 0)
    tab = lambda i: (i % tpb, 0)
    whole = lambda i: (0, 0)
    in_specs = [pl.BlockSpec((tm, D_MODEL), row), pl.BlockSpec((D_MODEL, IN_WIDTH), whole),
                pl.BlockSpec((1, IN_WIDTH), whole)] + [pl.BlockSpec((tm, LANES), tab)] * 3
    common_specs = [pl.BlockSpec((tm, 2 * ATTN_WIDTH), row), pl.BlockSpec((tm, CONV_CH), row)]
    common_shapes = [jax.ShapeDtypeStruct((n, 2 * ATTN_WIDTH), F32),
                     jax.ShapeDtypeStruct((n, CONV_CH), F32)]
    gate_spec, gate_shape = pl.BlockSpec((tm, 2 * D_MODEL), row), jax.ShapeDtypeStruct((n, 2 * D_MODEL), F32)
    if prompt:
        args = (x2d, w_bf, b2d, *tables, *conv_w)
        in_specs += [pl.BlockSpec((CONV_WIDTH, CONV_CH), whole), pl.BlockSpec((1, CONV_CH), whole)]
        q_specs = [pl.BlockSpec((tm // d, d * QKV_W), row) for _, d in ATTN_PATTERNS]
        q_shapes = [jax.ShapeDtypeStruct((n // d, d * QKV_W), BF16) for _, d in ATTN_PATTERNS]
        out_specs = q_specs + common_specs + [pl.BlockSpec((tm, CONV_CH), row), gate_spec]
        out_shape = q_shapes + common_shapes + [jax.ShapeDtypeStruct((n, CONV_CH), F32), gate_shape]
        scratch = [pltpu.VMEM((HALO + tm, CONV_CH), F32), pltpu.VMEM((8, HALO + tm, CONV_CH), F32),
                   pltpu.VMEM((QKV_W // LANES, tm, LANES), F32)]
    else:
        args = (x2d, w_bf, b2d, *tables)
        out_specs = [pl.BlockSpec((tm, ATTN_WIDTH), row)] + common_specs + [gate_spec]
        out_shape = [jax.ShapeDtypeStruct((n, ATTN_WIDTH), F32)] + common_shapes + [gate_shape]
        scratch = []
    return pl.pallas_call(
        functools.partial(_in_proj_kernel, prompt=prompt, tiles_per_seq=tiles_per_seq),
        grid=(n // tm,),
        in_specs=in_specs,
        out_specs=out_specs,
        out_shape=out_shape,
        scratch_shapes=scratch,
        compiler_params=_cparams(("arbitrary",)),
        name="in_proj_prompt" if prompt else "in_proj_sample",
    )(*args)


def _attn_block(q, keys, vals, first):
    masks = _head_masks(q.shape, 1)
    zero = jnp.zeros_like(q)
    q_stack = jnp.concatenate([jnp.where(m, q, zero) for m in masks], axis=0)
    s = lax.dot_general(q_stack, keys, (((1,), (1,)), ((), ())), preferred_element_type=F32)
    qi = lax.broadcasted_iota(I32, s.shape, 0) % STEPS
    kj = lax.broadcasted_iota(I32, s.shape, 1)
    ok = (kj <= qi) if first else ((kj >= qi) & (kj <= qi + STEPS))
    s = jnp.where(ok, s, NEG_INF)
    m = jnp.max(s, -1, keepdims=True)
    p = jnp.exp(s - m)
    l = jnp.sum(p, -1, keepdims=True)
    pv = jnp.dot(p.astype(BF16), vals, preferred_element_type=F32)
    lse = m + jnp.log(l)
    o = jnp.zeros((STEPS, GROUP_W), F32)
    lse_o = jnp.zeros((STEPS, GROUP_W), F32)
    for h, hm in enumerate(masks):
        rs = slice(h * STEPS, (h + 1) * STEPS)
        o = jnp.where(hm, pv[rs] / l[rs], o)
        lse_o = jnp.where(hm, lse[rs], lse_o)
    return o, lse_o


def _prompt_attn_kernel(q0_ref, q1_ref, q2_ref, attn_ref, o_rm1, l_rm1, o_rm2, l_rm2, o_nat, l_nat):
    halves = GROUP_W // LANES

    def put_nat(g, rows, o, lse):
        for hf in range(halves):
            hs = slice(hf * LANES, (hf + 1) * LANES)
            o_nat[g * halves + hf, rows, :] = o[:, hs]
            l_nat[g * halves + hf, rows, :] = lse[:, hs]

    for g, ref in enumerate((q0_ref, q1_ref, q2_ref)):
        dil = ATTN_PATTERNS[g][1]
        o_rm, l_rm = (None, o_rm1, o_rm2)[g], (None, l_rm1, l_rm2)[g]
        rows = ref.shape[0]
        nb = rows // STEPS
        for r in range(dil):
            base = r * QKV_W
            qc, kc, vc = (slice(base + j * GROUP_W, base + (j + 1) * GROUP_W) for j in range(3))
            oc = slice(r * GROUP_W, (r + 1) * GROUP_W)
            o, lse = _attn_block(ref[0:STEPS, qc], ref[0:STEPS, kc], ref[0:STEPS, vc], True)
            if dil == 1:
                put_nat(g, slice(0, STEPS), o, lse)
            else:
                o_rm[0:STEPS, oc] = o
                l_rm[0:STEPS, oc] = lse

            def body(n, carry, ref=ref, qc=qc, kc=kc, vc=vc, oc=oc, dil=dil, g=g, o_rm=o_rm, l_rm=l_rm):
                q0 = pl.multiple_of(n * STEPS, STEPS)
                k0 = pl.multiple_of((n - 1) * STEPS, STEPS)
                o, lse = _attn_block(ref[pl.ds(q0, STEPS), qc], ref[pl.ds(k0, 2 * STEPS), kc],
                                     ref[pl.ds(k0, 2 * STEPS), vc], False)
                if dil == 1:
                    put_nat(g, pl.ds(q0, STEPS), o, lse)
                else:
                    o_rm[pl.ds(q0, STEPS), oc] = o
                    l_rm[pl.ds(q0, STEPS), oc] = lse
                return carry

            if nb > 1:
                lax.fori_loop(1, nb, body, 0, unroll=3)
        if dil > 1:
            for r in range(dil):
                oc = slice(r * GROUP_W, (r + 1) * GROUP_W)
                put_nat(g, pl.ds(r, rows, stride=dil), o_rm[0:rows, oc], l_rm[0:rows, oc])

    def nat(ref, g):
        return jnp.concatenate([ref[g * halves + hf] for hf in range(halves)], axis=1)

    lses = [nat(l_nat, g) for g in range(N_GROUPS)]
    mx = jnp.maximum(jnp.maximum(lses[0], lses[1]), lses[2])
    ws = [jnp.exp(ls - mx) for ls in lses]
    attn_ref[...] = (nat(o_nat, 0) * ws[0] + nat(o_nat, 1) * ws[1] + nat(o_nat, 2) * ws[2]) / (ws[0] + ws[1] + ws[2])


def _prompt_attn(qkv_groups, bsz, seq):
    row = lambda b: (b, 0)
    in_specs = [pl.BlockSpec((seq // d, d * QKV_W), row) for _, d in ATTN_PATTERNS]
    return pl.pallas_call(
        _prompt_attn_kernel,
        grid=(bsz,),
        in_specs=in_specs,
        out_specs=pl.BlockSpec((seq, GROUP_W), row),
        out_shape=jax.ShapeDtypeStruct((bsz * seq, GROUP_W), F32),
        scratch_shapes=[pltpu.VMEM((seq // d, d * GROUP_W), F32) for _, d in ATTN_PATTERNS[1:] for _ in range(2)]
        + [pltpu.VMEM((N_GROUPS * (GROUP_W // LANES), seq, LANES), F32) for _ in range(2)],
        compiler_params=_cparams(("parallel",)),
        name="prompt_attn",
    )(*qkv_groups)


def _sample_attn_kernel(q_ref, c0_ref, c1_ref, c2_ref, n0_ref, n1_ref, n2_ref, attn_ref, o0_ref, o1_ref, o2_ref,
                        *, t_new):
    q = q_ref[0]
    outs, lses = [], []
    for g, (cref, nref, oref) in enumerate(((c0_ref, n0_ref, o0_ref), (c1_ref, n1_ref, o1_ref),
                                            (c2_ref, n2_ref, o2_ref))):
        dil = ATTN_PATTERNS[g][1]
        wb = cref.shape[2]
        new_t = jnp.concatenate([nref[0], jnp.zeros((LANES - nref.shape[1], KV_ROWS), F32)], axis=0).T
        full = jnp.concatenate([cref[0], new_t], axis=1)
        oref[0] = full[:, t_new:wb + t_new]
        k_t = full[0:GROUP_W].astype(BF16)
        v_t = full[GROUP_W:KV_ROWS].astype(BF16)
        qg = q[:, g * GROUP_W:(g + 1) * GROUP_W]
        masks = _head_masks(qg.shape, 1)
        q_stack = jnp.concatenate([jnp.where(m, qg, 0.0) for m in masks], axis=0).astype(BF16)
        s = jnp.dot(q_stack, k_t, preferred_element_type=F32)
        r = lax.broadcasted_iota(I32, s.shape, 1)
        t = lax.broadcasted_iota(I32, s.shape, 0) % t_new
        ok = (r >= t) & (r <= wb + t) & (((r - t) & (dil - 1)) == 0)
        s = jnp.where(ok, s, NEG_INF)
        m = jnp.max(s, -1, keepdims=True)
        p = jnp.exp(s - m)
        l = jnp.sum(p, -1, keepdims=True)
        pv = lax.dot_general(p.astype(BF16), v_t, (((1,), (1,)), ((), ())), preferred_element_type=F32)
        lse = m + jnp.log(l)
        o = jnp.zeros((t_new, GROUP_W), F32)
        lse_o = jnp.zeros((t_new, GROUP_W), F32)
        for h, hm in enumerate(masks):
            rs = slice(h * t_new, (h + 1) * t_new)
            o = jnp.where(hm, pv[rs] / l[rs], o)
            lse_o = jnp.where(hm, lse[rs], lse_o)
        outs.append(o)
        lses.append(lse_o)
    mx = jnp.maximum(jnp.maximum(lses[0], lses[1]), lses[2])
    ws = [jnp.exp(ls - mx) for ls in lses]
    attn_ref[0] = (outs[0] * ws[0] + outs[1] * ws[1] + outs[2] * ws[2]) / (ws[0] + ws[1] + ws[2])


def _sample_attn(q3, caches_t, news):
    db, t_new, _ = q3.shape
    for c, (_, dil) in zip(caches_t, ATTN_PATTERNS):
        assert c.shape[2] == STEPS * dil and t_new <= LANES and dil & (dil - 1) == 0
    blk = lambda b: (b, 0, 0)
    cache_specs = [pl.BlockSpec((1, KV_ROWS, c.shape[2]), blk) for c in caches_t]
    outs = pl.pallas_call(
        functools.partial(_sample_attn_kernel, t_new=t_new),
        grid=(db,),
        in_specs=[pl.BlockSpec((1, t_new, ATTN_WIDTH), blk)] + cache_specs
        + [pl.BlockSpec((1, news[0].shape[1], KV_ROWS), blk)] * 3,
        out_specs=[pl.BlockSpec((1, t_new, GROUP_W), blk)] + cache_specs,
        out_shape=[jax.ShapeDtypeStruct((db, t_new, GROUP_W), F32)]
        + [jax.ShapeDtypeStruct(c.shape, c.dtype) for c in caches_t],
        compiler_params=_cparams(("parallel",)),
        name="sample_attn",
    )(q3, *caches_t, *news)
    return outs[0], outs[1:]


def _kv_window_kernel(x_ref, o_ref):
    o_ref[0] = x_ref[0].T


def _kv_window(kvf3, g, window):
    bsz, seq, _ = kvf3.shape
    w = min(window, seq)
    rows = min(w, 512)
    assert w % rows == 0 and (seq - w) % rows == 0
    first = (seq - w) // rows
    return pl.pallas_call(
        _kv_window_kernel,
        grid=(bsz, w // rows),
        in_specs=[pl.BlockSpec((1, rows, KV_ROWS), lambda b, n: (b, first + n, g))],
        out_specs=pl.BlockSpec((1, KV_ROWS, rows), lambda b, n: (b, 0, n)),
        out_shape=jax.ShapeDtypeStruct((bsz, KV_ROWS, w), F32),
        compiler_params=_cparams(("parallel", "parallel")),
        name=f"kv_window_g{g}",
    )(kvf3)


def _sample_conv_kernel(state_ref, u_ref, w_ref, b_ref, conv_ref, sout_ref):
    hist = state_ref.shape[1]
    t_new = u_ref.shape[1]
    bb = u_ref.shape[0]
    for t in range(t_new):
        acc = jnp.broadcast_to(b_ref[...], (bb, CONV_CH))
        for j in range(CONV_WIDTH):
            r = t + j
            row = state_ref[:, r, :] if r < hist else u_ref[:, r - hist, :]
            acc = acc + row * w_ref[j:j + 1, :]
        conv_ref[:, t, :] = acc
    sout_ref[:, 0:hist - t_new, :] = state_ref[:, t_new:hist, :]
    sout_ref[:, hist - t_new:hist, :] = u_ref[...]


def _sample_conv(state, u3, w_dw, b_dw, bb=32):
    db, hist, _ = state.shape
    t_new = u3.shape[1]
    assert db % bb == 0 and hist == CONV_WIDTH - 1
    blk = lambda i: (i, 0, 0)
    whole = lambda i: (0, 0)
    return pl.pallas_call(
        _sample_conv_kernel,
        grid=(db // bb,),
        in_specs=[pl.BlockSpec((bb, hist, CONV_CH), blk), pl.BlockSpec((bb, t_new, CONV_CH), blk),
                  pl.BlockSpec((CONV_WIDTH, CONV_CH), whole), pl.BlockSpec((1, CONV_CH), whole)],
        out_specs=[pl.BlockSpec((bb, t_new, CONV_CH), blk), pl.BlockSpec((bb, hist, CONV_CH), blk)],
        out_shape=[jax.ShapeDtypeStruct((db, t_new, CONV_CH), F32), jax.ShapeDtypeStruct((db, hist, CONV_CH), F32)],
        compiler_params=_cparams(("parallel",)),
        name="sample_conv",
    )(state, u3, w_dw, b_dw)


ROUTE_LANES = 2 * LANES


def _merge_kernel(xp_ref, xs_ref, cp_ref, cs_ref, ap_ref, as_ref, gp_ref, gs_ref, clg_ref, clb_ref, wco_ref, bco_ref,
                  wao_ref, wo_ref, g1_ref, b1_ref, wr_ref, br_ref, h_ref, ri_ref, rg_ref, cnt_ref, base_scr, tile_scr,
                  *, prompt_tiles):
    i = pl.program_id(0)
    tm = xp_ref.shape[0]
    is_p = i < prompt_tiles

    @pl.when(i == 0)
    def _():
        base_scr[...] = jnp.zeros_like(base_scr)

    x = jnp.where(is_p, xp_ref[...], xs_ref[...])
    conv = jnp.where(is_p, cp_ref[...], cs_ref[...])
    attn = jnp.where(is_p, ap_ref[...], as_ref[...])
    gate = jnp.where(is_p, gp_ref[...], gs_ref[...])

    cn = _layer_norm(conv, clg_ref[...], clb_ref[...])
    branch_c = jnp.dot(jax.nn.silu(cn).astype(BF16), wco_ref[...], preferred_element_type=F32) + bco_ref[...]
    branch_a = jnp.dot(attn.astype(BF16), wao_ref[...], preferred_element_type=F32)
    mixed = gate[:, 0:D_MODEL] * branch_a + gate[:, D_MODEL:2 * D_MODEL] * branch_c
    y = jnp.dot(mixed.astype(BF16), wo_ref[...], preferred_element_type=F32)
    h = _layer_norm(DN_ALPHA * x + y, g1_ref[...], b1_ref[...])
    _store_rows_as_tiles(h_ref, h, tile_scr)

    logits = jnp.dot(h.astype(BF16), wr_ref[...], preferred_element_type=F32) + br_ref[...]
    lane = lax.broadcasted_iota(I32, (tm, LANES), 1)
    lane_f = lane.astype(F32)
    big = jnp.float32(LANES)

    def first_lane(hit):
        return jnp.min(jnp.where(hit, lane_f, big), -1, keepdims=True).astype(I32)

    gl = jnp.where(lane < N_EXPERT_GROUPS, logits[:, 0:LANES], NEG_INF)
    gmax = jnp.max(gl, -1, keepdims=True)
    g_sel = first_lane(gl == gmax)
    p_g = 1.0 / jnp.sum(jnp.exp(gl - gmax), -1, keepdims=True)
    in_group = (lane < N_EXPERTS) & (lax.shift_right_logical(lane, 3) == g_sel)
    el = jnp.where(in_group, logits[:, LANES:2 * LANES], NEG_INF)
    v1 = jnp.max(el, -1, keepdims=True)
    i1 = first_lane(el == v1)
    el2 = jnp.where(lane == i1, NEG_INF, el)
    v2 = jnp.max(el2, -1, keepdims=True)
    i2 = first_lane(el2 == v2)
    e21 = jnp.exp(v2 - v1)
    gate1 = p_g / (1.0 + e21)
    gate2 = p_g * e21 / (1.0 + e21)

    oh = ((lane == i1) | (lane == i2)).astype(BF16)
    ri_ = lax.broadcasted_iota(I32, (tm, tm), 0)
    ci_ = lax.broadcasted_iota(I32, (tm, tm), 1)
    tri = (ci_ < ri_).astype(BF16)
    before = jnp.dot(tri, oh, preferred_element_type=F32) + base_scr[...]
    rank1 = jnp.sum(jnp.where(lane == i1, before, 0.0), -1, keepdims=True).astype(I32)
    rank2 = jnp.sum(jnp.where(lane == i2, before, 0.0), -1, keepdims=True).astype(I32)
    base_scr[...] = base_scr[...] + jnp.sum(oh.astype(F32), 0, keepdims=True)
    cnt_ref[...] = base_scr[...]

    ri_ref[...] = jnp.where(lane == 0, i1, jnp.where(lane == 1, i2, jnp.where(lane == 2, rank1,
                            jnp.where(lane == 3, rank2, 0))))
    rg_ref[...] = jnp.where(lane == 0, gate1, jnp.where(lane == 1, gate2, 0.0))


def _merge(acts_p, acts_s, weights, tm):
    n_p, n_s = acts_p[0].shape[0], acts_s[0].shape[0]
    assert n_p % tm == 0 and n_s % tm == 0
    pt = n_p // tm
    n_all = n_p + n_s
    prow = lambda i: (jnp.minimum(i, pt - 1), 0)
    srow = lambda i: (jnp.maximum(i - pt, 0), 0)
    row = lambda i: (i, 0)
    whole = lambda i: (0, 0)
    act_args, act_specs = [], []
    for a_p, a_s in zip(acts_p, acts_s):
        act_args += [a_p, a_s]
        act_specs += [pl.BlockSpec((tm, a_p.shape[1]), prow), pl.BlockSpec((tm, a_s.shape[1]), srow)]
    return pl.pallas_call(
        functools.partial(_merge_kernel, prompt_tiles=pt),
        grid=(n_all // tm,),
        in_specs=act_specs + [pl.BlockSpec(w.shape, whole) for w in weights],
        out_specs=[pl.BlockSpec((tm, ROW_SUB, LANES), lambda i: (i, 0, 0)), pl.BlockSpec((tm, LANES), row),
                   pl.BlockSpec((tm, LANES), row), pl.BlockSpec((1, LANES), whole)],
        out_shape=[jax.ShapeDtypeStruct((n_all, ROW_SUB, LANES), F32), jax.ShapeDtypeStruct((n_all, LANES), I32),
                   jax.ShapeDtypeStruct((n_all, LANES), F32), jax.ShapeDtypeStruct((1, LANES), F32)],
        scratch_shapes=[pltpu.VMEM((1, LANES), F32), pltpu.VMEM((tm // 8, 8 * ROW_SUB, LANES), F32)],
        compiler_params=_cparams(("arbitrary",)),
        name="merge",
    )(*act_args, *weights)


def _dispatch_kernel(bounds_ref, nv_ref, dest_ref, h_ref, xg_any, hbuf, zero_scr, sem_z, sem_r):
    i = pl.program_id(0)
    tm = h_ref.shape[0]
    n_chunks = xg_any.shape[0] // CHUNK
    slot = i % 2

    def zero_chunk(row0):
        return pltpu.make_async_copy(zero_scr, xg_any.at[pl.ds(pl.multiple_of(row0, CHUNK), CHUNK)], sem_z)

    @pl.when(i == 0)
    def _():
        zero_scr[...] = jnp.zeros_like(zero_scr)
        for e in range(N_EXPERTS):
            @pl.when(bounds_ref[e + 1] > bounds_ref[e])
            def _():
                zero_chunk(bounds_ref[e + 1] - CHUNK).start()

        def start_tail(c, carry):
            zero_chunk(c * CHUNK).start()
            return carry

        lax.fori_loop(nv_ref[0], n_chunks, start_tail, 0)
        for e in range(N_EXPERTS):
            @pl.when(bounds_ref[e + 1] > bounds_ref[e])
            def _():
                zero_chunk(0).wait()

        def wait_tail(c, carry):
            zero_chunk(0).wait()
            return carry

        lax.fori_loop(nv_ref[0], n_chunks, wait_tail, 0)

    def wait_rows(sl):
        for _ in range(2 * tm):
            pltpu.make_async_copy(hbuf.at[0, 0], xg_any.at[0], sem_r.at[sl]).wait()

    @pl.when(i >= 2)
    def _():
        wait_rows(slot)

    hbuf[slot] = h_ref[...]
    for r in range(2 * tm):
        pltpu.make_async_copy(hbuf.at[slot, r % tm], xg_any.at[dest_ref[0, 0, r]],
                              sem_r.at[slot]).start(priority=r % 2)

    @pl.when(i == pl.num_programs(0) - 1)
    def _():
        wait_rows(slot)

        @pl.when(i >= 1)
        def _():
            wait_rows(1 - slot)


def _dispatch(bounds, n_valid, dest3, h3, n_rows, tm):
    n = h3.shape[0]
    return pl.pallas_call(
        _dispatch_kernel,
        grid_spec=pltpu.PrefetchScalarGridSpec(
            num_scalar_prefetch=2,
            grid=(n // tm,),
            in_specs=[pl.BlockSpec((1, 1, 2 * tm), lambda i, b, nv: (i, 0, 0), memory_space=pltpu.SMEM),
                      pl.BlockSpec((tm, ROW_SUB, LANES), lambda i, b, nv: (i, 0, 0))],
            out_specs=pl.BlockSpec(memory_space=pl.ANY),
            scratch_shapes=[pltpu.VMEM((2, tm, ROW_SUB, LANES), F32), pltpu.VMEM((CHUNK, ROW_SUB, LANES), F32),
                            pltpu.SemaphoreType.DMA(()), pltpu.SemaphoreType.DMA((2,))],
        ),
        out_shape=jax.ShapeDtypeStruct((n_rows, ROW_SUB, LANES), F32),
        compiler_params=_cparams(("arbitrary",)),
        name="moe_dispatch",
    )(bounds, n_valid, dest3, h3)


def _experts_kernel(ce_ref, nv_ref, x_ref, wg_ref, wu_ref, wd_ref, y_ref, wg_scr, wu_scr, wd_scr, tile_scr):
    c = pl.program_id(0)
    nv = nv_ref[0]

    @pl.when((c < nv) & ((c == 0) | (ce_ref[c] != ce_ref[jnp.maximum(c - 1, 0)])))
    def _():
        wg_scr[...] = wg_ref[0].astype(BF16)
        wu_scr[...] = wu_ref[0].astype(BF16)
        wd_scr[...] = wd_ref[0].astype(BF16)

    @pl.when(c < nv)
    def _():
        x = _load_tiles_as_rows(x_ref, tile_scr).astype(BF16)
        a = jnp.dot(x, wg_scr[...], preferred_element_type=F32)
        b = jnp.dot(x, wu_scr[...], preferred_element_type=F32)
        y = jnp.dot((jax.nn.silu(a) * b).astype(BF16), wd_scr[...], preferred_element_type=F32)
        _store_rows_as_tiles(y_ref, y, tile_scr)

    @pl.when(c >= nv)
    def _():
        y_ref[...] = jnp.zeros_like(y_ref)


def _experts(chunk_expert, n_valid, xg, w_eg, w_eu, w_ed):
    n_chunks = xg.shape[0] // CHUNK
    last = lambda nv: jnp.maximum(nv[0] - 1, 0)
    wsel = lambda c, ce, nv: (ce[jnp.minimum(c, last(nv))], 0, 0)
    return pl.pallas_call(
        _experts_kernel,
        grid_spec=pltpu.PrefetchScalarGridSpec(
            num_scalar_prefetch=2,
            grid=(n_chunks,),
            in_specs=[pl.BlockSpec((CHUNK, ROW_SUB, LANES), lambda c, ce, nv: (jnp.minimum(c, last(nv)), 0, 0)),
                      pl.BlockSpec((1, D_MODEL, D_EXPERT), wsel), pl.BlockSpec((1, D_MODEL, D_EXPERT), wsel),
                      pl.BlockSpec((1, D_EXPERT, D_MODEL), wsel)],
            out_specs=pl.BlockSpec((CHUNK, ROW_SUB, LANES), lambda c, ce, nv: (c, 0, 0)),
            scratch_shapes=[pltpu.VMEM((D_MODEL, D_EXPERT), BF16), pltpu.VMEM((D_MODEL, D_EXPERT), BF16),
                            pltpu.VMEM((D_EXPERT, D_MODEL), BF16),
                            pltpu.VMEM((CHUNK // 8, 8 * ROW_SUB, LANES), F32)],
        ),
        out_shape=jax.ShapeDtypeStruct(xg.shape, F32),
        compiler_params=_cparams(("arbitrary",)),
        name="moe_experts",
    )(chunk_expert, n_valid, xg, w_eg, w_eu, w_ed)


def _combine_kernel(dest_ref, dest_next_ref, h_ref, rg_ref, g2_ref, b2_ref, yg_any, yp_ref, ys_ref,
                    ybuf, t0_scr, t1_scr, t2_scr, sem, *, prompt_tiles):
    i = pl.program_id(0)
    tm = h_ref.shape[0]
    slot = i % 2

    def start_gather(src_ref, sl):
        for r in range(2 * tm):
            pltpu.make_async_copy(yg_any.at[src_ref[0, 0, r]], ybuf.at[sl, r], sem.at[sl]).start(priority=r % 2)

    @pl.when(i == 0)
    def _():
        start_gather(dest_ref, 0)

    @pl.when(i + 1 < pl.num_programs(0))
    def _():
        start_gather(dest_next_ref, 1 - slot)

    for _ in range(2 * tm):
        pltpu.make_async_copy(yg_any.at[0], ybuf.at[0, 0], sem.at[slot]).wait()
    moe = (_load_tiles_as_rows(ybuf.at[slot, pl.ds(0, tm)], t1_scr) * rg_ref[:, 0:1]
           + _load_tiles_as_rows(ybuf.at[slot, pl.ds(tm, tm)], t2_scr) * rg_ref[:, 1:2])
    out = _layer_norm(DN_ALPHA * _load_tiles_as_rows(h_ref, t0_scr) + moe, g2_ref[...], b2_ref[...])

    @pl.when(i < prompt_tiles)
    def _():
        yp_ref[...] = out

    @pl.when(i >= prompt_tiles)
    def _():
        ys_ref[...] = out


def _combine(dest3, h3, yg, route_g, ln2_g, ln2_b, n_prompt, tm):
    n = h3.shape[0]
    pt, nt = n_prompt // tm, n // tm
    row = lambda i: (i, 0)
    whole = lambda i: (0, 0)
    return pl.pallas_call(
        functools.partial(_combine_kernel, prompt_tiles=pt),
        grid=(nt,),
        in_specs=[pl.BlockSpec((1, 1, 2 * tm), lambda i: (i, 0, 0), memory_space=pltpu.SMEM),
                  pl.BlockSpec((1, 1, 2 * tm), lambda i: (jnp.minimum(i + 1, nt - 1), 0, 0), memory_space=pltpu.SMEM),
                  pl.BlockSpec((tm, ROW_SUB, LANES), lambda i: (i, 0, 0)), pl.BlockSpec((tm, LANES), row),
                  pl.BlockSpec((1, D_MODEL), whole), pl.BlockSpec((1, D_MODEL), whole),
                  pl.BlockSpec(memory_space=pl.ANY)],
        out_specs=[pl.BlockSpec((tm, D_MODEL), lambda i: (jnp.minimum(i, pt - 1), 0)),
                   pl.BlockSpec((tm, D_MODEL), lambda i: (jnp.maximum(i - pt, 0), 0))],
        out_shape=[jax.ShapeDtypeStruct((n_prompt, D_MODEL), F32), jax.ShapeDtypeStruct((n - n_prompt, D_MODEL), F32)],
        scratch_shapes=[pltpu.VMEM((2, 2 * tm, ROW_SUB, LANES), F32)]
        + [pltpu.VMEM((tm // 8, 8 * ROW_SUB, LANES), F32) for _ in range(3)] + [pltpu.SemaphoreType.DMA((2,))],
        compiler_params=_cparams(("arbitrary",)),
        name="moe_combine",
    )(dest3, dest3, h3, route_g, ln2_g, ln2_b, yg)


def kernel(x_prompt, x_sample, cache_kv_w128, cache_kv_w512, cache_kv_w2048, state_conv, w_in, b_in, w_dw, b_dw, conv_ln_g, conv_ln_b, w_conv_out, b_conv_out, w_attn_out, w_o, ln1_g, ln1_b, w_router_group, b_router_group, w_router_expert, b_router_expert, w_expert_gate, w_expert_up, w_expert_down, ln2_g, ln2_b):
    assert w_in.shape[0] == DEPTH == 1
    bsz, seq, _ = x_prompt.shape
    db, t_new, _ = x_sample.shape
    caches = (cache_kv_w128, cache_kv_w512, cache_kv_w2048)
    past = cache_kv_w2048.shape[2]
    n_p, n_s = bsz * seq, db * t_new
    n_all = n_p + n_s
    tm = ROW_TILE
    kvs = (HEADS_PER_GROUP, HEAD_DIM)

    w_bf = w_in[0].astype(BF16)
    xp2, xs2 = x_prompt.reshape(n_p, D_MODEL), x_sample.reshape(n_s, D_MODEL)
    q0, q1, q2, kvf_p, u_p, conv_p, gate_p = _in_proj(xp2, w_bf, b_in, _rotary_tables(np.arange(seq)), tm,
                                                     conv_w=(w_dw[0], b_dw), tiles_per_seq=seq // tm)
    pos_s = past + np.arange(tm) % t_new
    q_s, kvf_s, u_s, gate_s = _in_proj(xs2, w_bf, b_in, _rotary_tables(pos_s), tm)

    attn_p = _prompt_attn((q0, q1, q2), bsz, seq)

    kv_new = kvf_s.reshape(db, t_new, N_GROUPS, KV_ROWS)
    caches_t = [jnp.transpose(c[0], (0, 2, 3, 4, 1)).reshape(db, KV_ROWS, c.shape[2]) for c in caches]
    news = [jnp.pad(kv_new[:, :, g], ((0, 0), (0, -t_new % 8), (0, 0))) for g in range(N_GROUPS)]
    attn_s, kv_sample_t = _sample_attn(q_s.reshape(db, t_new, ATTN_WIDTH), caches_t, news)
    kv_sample = [jnp.transpose(o.reshape(db, 2, *kvs, o.shape[2]), (0, 4, 1, 2, 3))[None] for o in kv_sample_t]

    conv_s, conv_state_s = _sample_conv(state_conv[0], u_s.reshape(db, t_new, CONV_CH), w_dw[0], b_dw)

    w_r = jnp.zeros((D_MODEL, ROUTE_LANES), F32)
    w_r = w_r.at[:, 0:N_EXPERT_GROUPS].set(w_router_group[0]).at[:, LANES:LANES + N_EXPERTS].set(w_router_expert[0])
    b_r = jnp.zeros((1, ROUTE_LANES), F32)
    b_r = b_r.at[:, 0:N_EXPERT_GROUPS].set(b_router_group).at[:, LANES:LANES + N_EXPERTS].set(b_router_expert)
    weights = [conv_ln_g, conv_ln_b, w_conv_out[0].astype(BF16), b_conv_out, w_attn_out[0].astype(BF16),
               w_o[0].astype(BF16), ln1_g, ln1_b, w_r.astype(BF16), b_r]
    acts_p = (xp2, conv_p, attn_p, gate_p)
    acts_s = (xs2, conv_s.reshape(n_s, CONV_CH), attn_s.reshape(n_s, GROUP_W), gate_s)
    h3, route_i, route_g, counts = _merge(acts_p, acts_s, weights, tm)

    cnt = counts[0, :N_EXPERTS].astype(I32)
    padded = ((cnt + CHUNK - 1) // CHUNK) * CHUNK
    pad_end = jnp.cumsum(padded)
    bounds = jnp.concatenate([jnp.zeros((1,), I32), pad_end]).astype(I32)
    n_chunks = (2 * n_all) // CHUNK + N_EXPERTS
    chunk_row0 = jnp.arange(n_chunks, dtype=I32) * CHUNK
    chunk_expert = jnp.minimum(jnp.sum((pad_end[None, :] <= chunk_row0[:, None]).astype(I32), axis=1),
                               N_EXPERTS - 1).astype(I32)
    n_valid = (pad_end[-1:] // CHUNK).astype(I32)

    e_onehot = route_i[:, 0:2, None] == jnp.arange(N_EXPERTS, dtype=I32)[None, None, :]
    dest = jnp.sum(jnp.where(e_onehot, bounds[None, None, :N_EXPERTS], 0), axis=-1) + route_i[:, 2:4]
    dest3 = jnp.transpose(dest.reshape(n_all // tm, tm, 2), (0, 2, 1)).reshape(n_all // tm, 1, 2 * tm)
    xg = _dispatch(bounds, n_valid, dest3, h3, n_chunks * CHUNK, tm)
    yg = _experts(chunk_expert, n_valid, xg, w_expert_gate[0], w_expert_up[0], w_expert_down[0])
    y_p, y_s = _combine(dest3, h3, yg, route_g, ln2_g, ln2_b, n_p, tm)

    kvf3 = kvf_p.reshape(bsz, seq, N_GROUPS * KV_ROWS)
    kv_prompt = []
    for g, (w, _) in enumerate(ATTN_PATTERNS):
        win_t = _kv_window(kvf3, g, w)
        kv_prompt.append(jnp.transpose(win_t.reshape(bsz, 2, *kvs, win_t.shape[2]), (0, 4, 1, 2, 3))[None])
    conv_prompt = u_p.reshape(1, bsz, seq, CONV_CH)[:, :, seq - (CONV_WIDTH - 1):]
    return (y_p.reshape(bsz, seq, D_MODEL), y_s.reshape(db, t_new, D_MODEL),
            kv_prompt[0], kv_prompt[1], kv_prompt[2], conv_prompt,
            kv_sample[0], kv_sample[1], kv_sample[2], conv_state_s[None])
```

```python
import functools

import jax
import jax.numpy as jnp
import numpy as np
from jax import lax
from jax.experimental import pallas as pl
from jax.experimental.pallas import tpu as pltpu

F32 = jnp.float32
BF16 = jnp.bfloat16
I32 = jnp.int32

D_MODEL = 1024
HEAD_DIM = 64
HEADS_PER_GROUP = 4
GROUP_W = HEADS_PER_GROUP * HEAD_DIM
ATTN_PATTERNS = ((128, 1), (512, 4), (2048, 16))
N_GROUPS = len(ATTN_PATTERNS)
ATTN_WIDTH = N_GROUPS * GROUP_W
QKV_W = 3 * GROUP_W
STEPS = 128
ROT_HALF = 8
ROPE_THETA = 500000.0
CONV_CH = 512
CONV_WIDTH = 31
HALO = 32
N_EXPERTS = 32
N_EXPERT_GROUPS = 4
D_EXPERT = 512
DEPTH = 1
DN_ALPHA = (2 * DEPTH) ** 0.25
LN_EPS = 1e-5
SCALE = HEAD_DIM ** -0.5
NEG_INF = -1e30
LANES = 128
ROW_SUB = D_MODEL // LANES
KV_ROWS = 2 * GROUP_W
ROW_TILE = 256
CHUNK = 256
VMEM_LIMIT = 56 * 1024 * 1024

_Q0, _K0, _V0 = 0, ATTN_WIDTH, 2 * ATTN_WIDTH
_UA0 = 3 * ATTN_WIDTH
_UB0 = _UA0 + CONV_CH
_G0 = _UB0 + CONV_CH
IN_WIDTH = _G0 + 2 * D_MODEL


def _cparams(sem):
    return pltpu.CompilerParams(dimension_semantics=sem, vmem_limit_bytes=VMEM_LIMIT)


def _layer_norm(x, g, b):
    mu = jnp.mean(x, -1, keepdims=True)
    xc = x - mu
    var = jnp.mean(xc * xc, -1, keepdims=True)
    return xc * lax.rsqrt(var + LN_EPS) * g + b


def _head_masks(shape, lane_axis):
    lane = lax.broadcasted_iota(I32, shape, lane_axis)
    return [(lane >= h * HEAD_DIM) & (lane < (h + 1) * HEAD_DIM) for h in range(HEADS_PER_GROUP)]


def _store_rows_as_tiles(ref3, val2d, tmp):
    sub = 8
    groups = val2d.shape[0] // sub
    for c in range(ROW_SUB):
        tmp[:, c * sub:(c + 1) * sub, :] = val2d[:, c * LANES:(c + 1) * LANES].reshape(groups, sub, LANES)
    for s in range(sub):
        ref3[pl.ds(s, groups, stride=sub), :, :] = tmp[:, pl.ds(s, ROW_SUB, stride=sub), :]


def _load_tiles_as_rows(ref3, tmp):
    sub = 8
    groups = ref3.shape[0] // sub
    for s in range(sub):
        tmp[:, pl.ds(s, ROW_SUB, stride=sub), :] = ref3[pl.ds(s, groups, stride=sub), :, :]
    return jnp.concatenate([tmp[:, c * sub:(c + 1) * sub, :].reshape(groups * sub, LANES) for c in range(ROW_SUB)],
                           axis=1)


def _rotary_tables(pos):
    pos = np.asarray(pos, np.float64)
    inv_freq = ROPE_THETA ** (-np.arange(ROT_HALF, dtype=np.float64) / ROT_HALF)
    ang = pos[:, None] * inv_freq[None, :]
    cos, sin = np.cos(ang), np.sin(ang)
    zero = np.zeros_like(cos)
    p = pos.shape[0]
    rest = HEAD_DIM - 2 * ROT_HALF
    c_head = np.concatenate([cos, cos, np.ones((p, rest))], -1)
    sa_head = np.concatenate([zero, sin, np.zeros((p, rest))], -1)
    sb_head = np.concatenate([-sin, zero, np.zeros((p, rest))], -1)
    rep = LANES // HEAD_DIM
    return tuple(jnp.asarray(np.tile(t, (1, rep)), F32) for t in (c_head, sa_head, sb_head))


def _causal_conv_rows(ext_scr, sh_scr, wdw_ref, bdw_ref, conv_ref):
    n = conv_ref.shape[0]
    off = HALO - (CONV_WIDTH - 1)
    sub = 8
    for s in range(sub):
        rows = n + sub * ((CONV_WIDTH - s + sub - 1) // sub - 1)
        sh_scr[s, 0:rows, :] = ext_scr[off + s:off + s + rows, :]
    rb = 64
    for cc in range(CONV_CH // LANES):
        cs = slice(cc * LANES, (cc + 1) * LANES)
        for r0 in range(0, n, rb):
            acc = jnp.broadcast_to(bdw_ref[:, cs], (rb, LANES))
            for j in range(CONV_WIDTH):
                a0 = sub * (j // sub) + r0
                acc = acc + sh_scr[j % sub, a0:a0 + rb, cs] * wdw_ref[j:j + 1, cs]
            conv_ref[r0:r0 + rb, cs] = acc


def _in_proj_kernel(*refs, prompt):
    if prompt:
        (x_ref, w_ref, b_ref, c_ref, sa_ref, sb_ref,
         q0_ref, q1_ref, q2_ref, kvf_ref, u_ref, gate_ref, rm_scr) = refs
    else:
        (x_ref, w_ref, b_ref, c_ref, sa_ref, sb_ref, qs_ref, kvf_ref, u_ref, gate_ref) = refs
    tm = x_ref.shape[0]
    xb = x_ref[...].astype(BF16)

    def mm(c0, c1):
        return jnp.dot(xb, w_ref[:, c0:c1], preferred_element_type=F32) + b_ref[:, c0:c1]

    u_ref[...] = mm(_UA0, _UB0) * jax.nn.sigmoid(mm(_UB0, _G0))
    c, sa, sb = c_ref[...], sa_ref[...], sb_ref[...]

    def rotary(z):
        parts = []
        for j in range(z.shape[1] // LANES):
            zc = z[:, j * LANES:(j + 1) * LANES]
            parts.append(zc * c + pltpu.roll(zc, ROT_HALF, 1) * sa + pltpu.roll(zc, LANES - ROT_HALF, 1) * sb)
        return jnp.concatenate(parts, axis=1)

    q = rotary(mm(_Q0, _K0)) * SCALE
    k = rotary(mm(_K0, _V0))
    v = mm(_V0, _UA0)
    for g in range(N_GROUPS):
        gs = slice(g * GROUP_W, (g + 1) * GROUP_W)
        kvf_ref[:, 2 * g * GROUP_W:(2 * g + 1) * GROUP_W] = k[:, gs]
        kvf_ref[:, (2 * g + 1) * GROUP_W:(2 * g + 2) * GROUP_W] = v[:, gs]
    gate_ref[...] = jax.nn.sigmoid(mm(_G0, IN_WIDTH))

    if not prompt:
        qs_ref[...] = q
        return

    for g, out_ref in enumerate((q0_ref, q1_ref, q2_ref)):
        dil = ATTN_PATTERNS[g][1]
        gs = slice(g * GROUP_W, (g + 1) * GROUP_W)
        qkv = jnp.concatenate([q[:, gs], k[:, gs], v[:, gs]], axis=1)
        if dil == 1:
            out_ref[...] = qkv.astype(BF16)
        else:
            nck = QKV_W // LANES
            for ck in range(nck):
                rm_scr[ck] = qkv[:, ck * LANES:(ck + 1) * LANES]
            for r in range(dil):
                blk = jnp.concatenate([rm_scr[ck, pl.ds(r, tm // dil, stride=dil), :] for ck in range(nck)], axis=1)
                out_ref[:, r * QKV_W:(r + 1) * QKV_W] = blk.astype(BF16)


def _in_proj(x2d, w_bf, b2d, tables, tm, prompt):
    n = x2d.shape[0]
    p = tables[0].shape[0]
    assert n % tm == 0 and p % tm == 0
    tpb = p // tm
    row = lambda i: (i, 0)
    tab = lambda i: (i % tpb, 0)
    whole = lambda i: (0, 0)
    in_specs = [pl.BlockSpec((tm, D_MODEL), row), pl.BlockSpec((D_MODEL, IN_WIDTH), whole),
                pl.BlockSpec((1, IN_WIDTH), whole)] + [pl.BlockSpec((tm, LANES), tab)] * 3
    common_specs = [pl.BlockSpec((tm, 2 * ATTN_WIDTH), row), pl.BlockSpec((tm, CONV_CH), row)]
    common_shapes = [jax.ShapeDtypeStruct((n, 2 * ATTN_WIDTH), F32),
                     jax.ShapeDtypeStruct((n, CONV_CH), F32)]
    gate_spec, gate_shape = pl.BlockSpec((tm, 2 * D_MODEL), row), jax.ShapeDtypeStruct((n, 2 * D_MODEL), F32)
    if prompt:
        q_specs = [pl.BlockSpec((tm // d, d * QKV_W), row) for _, d in ATTN_PATTERNS]
        q_shapes = [jax.ShapeDtypeStruct((n // d, d * QKV_W), BF16) for _, d in ATTN_PATTERNS]
        out_specs = q_specs + common_specs + [gate_spec]
        out_shape = q_shapes + common_shapes + [gate_shape]
        scratch = [pltpu.VMEM((QKV_W // LANES, tm, LANES), F32)]
    else:
        out_specs = [pl.BlockSpec((tm, ATTN_WIDTH), row)] + common_specs + [gate_spec]
        out_shape = [jax.ShapeDtypeStruct((n, ATTN_WIDTH), F32)] + common_shapes + [gate_shape]
        scratch = []
    return pl.pallas_call(
        functools.partial(_in_proj_kernel, prompt=prompt),
        grid=(n // tm,),
        in_specs=in_specs,
        out_specs=out_specs,
        out_shape=out_shape,
        scratch_shapes=scratch,
        compiler_params=_cparams(("parallel",)),
        name="in_proj_prompt" if prompt else "in_proj_sample",
    )(x2d, w_bf, b2d, *tables)


def _attn_block(q, keys, vals, first):
    masks = _head_masks(q.shape, 1)
    zero = jnp.zeros_like(q)
    q_stack = jnp.concatenate([jnp.where(m, q, zero) for m in masks], axis=0)
    s = lax.dot_general(q_stack, keys, (((1,), (1,)), ((), ())), preferred_element_type=F32)
    qi = lax.broadcasted_iota(I32, s.shape, 0) % STEPS
    kj = lax.broadcasted_iota(I32, s.shape, 1)
    ok = (kj <= qi) if first else ((kj >= qi) & (kj <= qi + STEPS))
    s = jnp.where(ok, s, NEG_INF)
    m = jnp.max(s, -1, keepdims=True)
    p = jnp.exp(s - m)
    l = jnp.sum(p, -1, keepdims=True)
    pv = jnp.dot(p.astype(BF16), vals, preferred_element_type=F32)
    lse = m + jnp.log(l)
    o = jnp.zeros((STEPS, GROUP_W), F32)
    lse_o = jnp.zeros((STEPS, GROUP_W), F32)
    for h, hm in enumerate(masks):
        rs = slice(h * STEPS, (h + 1) * STEPS)
        o = jnp.where(hm, pv[rs] / l[rs], o)
        lse_o = jnp.where(hm, lse[rs], lse_o)
    return o, lse_o


def _prompt_attn_kernel(q0_ref, q1_ref, q2_ref, attn_ref, o_rm1, l_rm1, o_rm2, l_rm2, o_nat, l_nat):
    halves = GROUP_W // LANES

    def put_nat(g, rows, o, lse):
        for hf in range(halves):
            hs = slice(hf * LANES, (hf + 1) * LANES)
            o_nat[g * halves + hf, rows, :] = o[:, hs]
            l_nat[g * halves + hf, rows, :] = lse[:, hs]

    for g, ref in enumerate((q0_ref, q1_ref, q2_ref)):
        dil = ATTN_PATTERNS[g][1]
        o_rm, l_rm = (None, o_rm1, o_rm2)[g], (None, l_rm1, l_rm2)[g]
        rows = ref.shape[0]
        nb = rows // STEPS
        for r in range(dil):
            base = r * QKV_W
            qc, kc, vc = (slice(base + j * GROUP_W, base + (j + 1) * GROUP_W) for j in range(3))
            oc = slice(r * GROUP_W, (r + 1) * GROUP_W)
            o, lse = _attn_block(ref[0:STEPS, qc], ref[0:STEPS, kc], ref[0:STEPS, vc], True)
            if dil == 1:
                put_nat(g, slice(0, STEPS), o, lse)
            else:
                o_rm[0:STEPS, oc] = o
                l_rm[0:STEPS, oc] = lse

            def body(n, carry, ref=ref, qc=qc, kc=kc, vc=vc, oc=oc, dil=dil, g=g, o_rm=o_rm, l_rm=l_rm):
                q0 = pl.multiple_of(n * STEPS, STEPS)
                k0 = pl.multiple_of((n - 1) * STEPS, STEPS)
                o, lse = _attn_block(ref[pl.ds(q0, STEPS), qc], ref[pl.ds(k0, 2 * STEPS), kc],
                                     ref[pl.ds(k0, 2 * STEPS), vc], False)
                if dil == 1:
                    put_nat(g, pl.ds(q0, STEPS), o, lse)
                else:
                    o_rm[pl.ds(q0, STEPS), oc] = o
                    l_rm[pl.ds(q0, STEPS), oc] = lse
                return carry

            if nb > 1:
                lax.fori_loop(1, nb, body, 0, unroll=3)
        if dil > 1:
            for r in range(dil):
                oc = slice(r * GROUP_W, (r + 1) * GROUP_W)
                put_nat(g, pl.ds(r, rows, stride=dil), o_rm[0:rows, oc], l_rm[0:rows, oc])

    def nat(ref, g):
        return jnp.concatenate([ref[g * halves + hf] for hf in range(halves)], axis=1)

    lses = [nat(l_nat, g) for g in range(N_GROUPS)]
    mx = jnp.maximum(jnp.maximum(lses[0], lses[1]), lses[2])
    ws = [jnp.exp(ls - mx) for ls in lses]
    attn_ref[...] = (nat(o_nat, 0) * ws[0] + nat(o_nat, 1) * ws[1] + nat(o_nat, 2) * ws[2]) / (ws[0] + ws[1] + ws[2])


def _prompt_attn(qkv_groups, bsz, seq):
    row = lambda b: (b, 0)
    in_specs = [pl.BlockSpec((seq // d, d * QKV_W), row) for _, d in ATTN_PATTERNS]
    return pl.pallas_call(
        _prompt_attn_kernel,
        grid=(bsz,),
        in_specs=in_specs,
        out_specs=pl.BlockSpec((seq, GROUP_W), row),
        out_shape=jax.ShapeDtypeStruct((bsz * seq, GROUP_W), F32),
        scratch_shapes=[pltpu.VMEM((seq // d, d * GROUP_W), F32) for _, d in ATTN_PATTERNS[1:] for _ in range(2)]
        + [pltpu.VMEM((N_GROUPS * (GROUP_W // LANES), seq, LANES), F32) for _ in range(2)],
        compiler_params=_cparams(("parallel",)),
        name="prompt_attn",
    )(*qkv_groups)


def _sample_attn_kernel(q_ref, c0_ref, c1_ref, c2_ref, n0_ref, n1_ref, n2_ref, u_ref, halo_ref, wdw_ref, bdw_ref,
                        attn_ref, o0_ref, o1_ref, o2_ref, conv_ref, ext_scr, sh_scr, *, t_new, conv_tiles_per_seq):
    n_conv = u_ref.shape[0]
    first = pl.program_id(0) % conv_tiles_per_seq == 0
    ext_scr[0:HALO, :] = jnp.where(first, 0.0, halo_ref[...])
    ext_scr[HALO:HALO + n_conv, :] = u_ref[...]
    _causal_conv_rows(ext_scr, sh_scr, wdw_ref, bdw_ref, conv_ref)

    q = q_ref[0]
    outs, lses = [], []
    for g, (cref, nref, oref) in enumerate(((c0_ref, n0_ref, o0_ref), (c1_ref, n1_ref, o1_ref),
                                            (c2_ref, n2_ref, o2_ref))):
        dil = ATTN_PATTERNS[g][1]
        wb = cref.shape[2]
        new_t = jnp.concatenate([nref[0], jnp.zeros((LANES - nref.shape[1], KV_ROWS), F32)], axis=0).T
        full = jnp.concatenate([cref[0], new_t], axis=1)
        oref[0] = full[:, t_new:wb + t_new]
        k_t = full[0:GROUP_W].astype(BF16)
        v_t = full[GROUP_W:KV_ROWS].astype(BF16)
        qg = q[:, g * GROUP_W:(g + 1) * GROUP_W]
        masks = _head_masks(qg.shape, 1)
        q_stack = jnp.concatenate([jnp.where(m, qg, 0.0) for m in masks], axis=0).astype(BF16)
        s = jnp.dot(q_stack, k_t, preferred_element_type=F32)
        r = lax.broadcasted_iota(I32, s.shape, 1)
        t = lax.broadcasted_iota(I32, s.shape, 0) % t_new
        ok = (r >= t) & (r <= wb + t) & (((r - t) & (dil - 1)) == 0)
        s = jnp.where(ok, s, NEG_INF)
        m = jnp.max(s, -1, keepdims=True)
        p = jnp.exp(s - m)
        l = jnp.sum(p, -1, keepdims=True)
        pv = lax.dot_general(p.astype(BF16), v_t, (((1,), (1,)), ((), ())), preferred_element_type=F32)
        lse = m + jnp.log(l)
        o = jnp.zeros((t_new, GROUP_W), F32)
        lse_o = jnp.zeros((t_new, GROUP_W), F32)
        for h, hm in enumerate(masks):
            rs = slice(h * t_new, (h + 1) * t_new)
            o = jnp.where(hm, pv[rs] / l[rs], o)
            lse_o = jnp.where(hm, lse[rs], lse_o)
        outs.append(o)
        lses.append(lse_o)
    mx = jnp.maximum(jnp.maximum(lses[0], lses[1]), lses[2])
    ws = [jnp.exp(ls - mx) for ls in lses]
    attn_ref[0] = (outs[0] * ws[0] + outs[1] * ws[1] + outs[2] * ws[2]) / (ws[0] + ws[1] + ws[2])


def _sample_attn(q3, caches_t, news, u_prompt, prompt_seq, w_dw, b_dw):
    db, t_new, _ = q3.shape
    for c, (_, dil) in zip(caches_t, ATTN_PATTERNS):
        assert c.shape[2] == STEPS * dil and t_new <= LANES and dil & (dil - 1) == 0
    n_conv = u_prompt.shape[0] // db
    assert u_prompt.shape[0] == n_conv * db and n_conv % 64 == 0 and n_conv % HALO == 0 and prompt_seq % n_conv == 0
    blk = lambda b: (b, 0, 0)
    row = lambda b: (b, 0)
    whole = lambda b: (0, 0)
    cache_specs = [pl.BlockSpec((1, KV_ROWS, c.shape[2]), blk) for c in caches_t]
    halo_spec = pl.BlockSpec((HALO, CONV_CH), lambda b: (jnp.maximum(b * (n_conv // HALO) - 1, 0), 0))
    outs = pl.pallas_call(
        functools.partial(_sample_attn_kernel, t_new=t_new, conv_tiles_per_seq=prompt_seq // n_conv),
        grid=(db,),
        in_specs=[pl.BlockSpec((1, t_new, ATTN_WIDTH), blk)] + cache_specs
        + [pl.BlockSpec((1, news[0].shape[1], KV_ROWS), blk)] * 3
        + [pl.BlockSpec((n_conv, CONV_CH), row), halo_spec,
           pl.BlockSpec((CONV_WIDTH, CONV_CH), whole), pl.BlockSpec((1, CONV_CH), whole)],
        out_specs=[pl.BlockSpec((1, t_new, GROUP_W), blk)] + cache_specs + [pl.BlockSpec((n_conv, CONV_CH), row)],
        out_shape=[jax.ShapeDtypeStruct((db, t_new, GROUP_W), F32)]
        + [jax.ShapeDtypeStruct(c.shape, c.dtype) for c in caches_t]
        + [jax.ShapeDtypeStruct(u_prompt.shape, F32)],
        scratch_shapes=[pltpu.VMEM((HALO + n_conv, CONV_CH), F32), pltpu.VMEM((8, HALO + n_conv, CONV_CH), F32)],
        compiler_params=_cparams(("parallel",)),
        name="sample_attn",
    )(q3, *caches_t, *news, u_prompt, u_prompt, w_dw, b_dw)
    return outs[0], outs[1:4], outs[4]


def _kv_window_kernel(x_ref, o_ref):
    o_ref[0] = x_ref[0].T


def _kv_window(kvf3, g, window):
    bsz, seq, _ = kvf3.shape
    w = min(window, seq)
    rows = min(w, 512)
    assert w % rows == 0 and (seq - w) % rows == 0
    first = (seq - w) // rows
    return pl.pallas_call(
        _kv_window_kernel,
        grid=(bsz, w // rows),
        in_specs=[pl.BlockSpec((1, rows, KV_ROWS), lambda b, n: (b, first + n, g))],
        out_specs=pl.BlockSpec((1, KV_ROWS, rows), lambda b, n: (b, 0, n)),
        out_shape=jax.ShapeDtypeStruct((bsz, KV_ROWS, w), F32),
        compiler_params=_cparams(("parallel", "parallel")),
        name=f"kv_window_g{g}",
    )(kvf3)


def _sample_conv_kernel(state_ref, u_ref, w_ref, b_ref, conv_ref, sout_ref):
    hist = state_ref.shape[1]
    t_new = u_ref.shape[1]
    bb = u_ref.shape[0]
    for t in range(t_new):
        acc = jnp.broadcast_to(b_ref[...], (bb, CONV_CH))
        for j in range(CONV_WIDTH):
            r = t + j
            row = state_ref[:, r, :] if r < hist else u_ref[:, r - hist, :]
            acc = acc + row * w_ref[j:j + 1, :]
        conv_ref[:, t, :] = acc
    sout_ref[:, 0:hist - t_new, :] = state_ref[:, t_new:hist, :]
    sout_ref[:, hist - t_new:hist, :] = u_ref[...]


def _sample_conv(state, u3, w_dw, b_dw, bb=32):
    db, hist, _ = state.shape
    t_new = u3.shape[1]
    assert db % bb == 0 and hist == CONV_WIDTH - 1
    blk = lambda i: (i, 0, 0)
    whole = lambda i: (0, 0)
    return pl.pallas_call(
        _sample_conv_kernel,
        grid=(db // bb,),
        in_specs=[pl.BlockSpec((bb, hist, CONV_CH), blk), pl.BlockSpec((bb, t_new, CONV_CH), blk),
                  pl.BlockSpec((CONV_WIDTH, CONV_CH), whole), pl.BlockSpec((1, CONV_CH), whole)],
        out_specs=[pl.BlockSpec((bb, t_new, CONV_CH), blk), pl.BlockSpec((bb, hist, CONV_CH), blk)],
        out_shape=[jax.ShapeDtypeStruct((db, t_new, CONV_CH), F32), jax.ShapeDtypeStruct((db, hist, CONV_CH), F32)],
        compiler_params=_cparams(("parallel",)),
        name="sample_conv",
    )(state, u3, w_dw, b_dw)


ROUTE_LANES = 2 * LANES


def _merge_kernel(xp_ref, xs_ref, cp_ref, cs_ref, ap_ref, as_ref, gp_ref, gs_ref, clg_ref, clb_ref, wco_ref, bco_ref,
                  wao_ref, wo_ref, g1_ref, b1_ref, wr_ref, br_ref, h_ref, ri_ref, rg_ref, cnt_ref, base_scr, tile_scr,
                  *, prompt_tiles):
    i = pl.program_id(0)
    tm = xp_ref.shape[0]
    is_p = i < prompt_tiles

    @pl.when(i == 0)
    def _():
        base_scr[...] = jnp.zeros_like(base_scr)

    x = jnp.where(is_p, xp_ref[...], xs_ref[...])
    conv = jnp.where(is_p, cp_ref[...], cs_ref[...])
    attn = jnp.where(is_p, ap_ref[...], as_ref[...])
    gate = jnp.where(is_p, gp_ref[...], gs_ref[...])

    cn = _layer_norm(conv, clg_ref[...], clb_ref[...])
    branch_c = jnp.dot(jax.nn.silu(cn).astype(BF16), wco_ref[...], preferred_element_type=F32) + bco_ref[...]
    branch_a = jnp.dot(attn.astype(BF16), wao_ref[...], preferred_element_type=F32)
    mixed = gate[:, 0:D_MODEL] * branch_a + gate[:, D_MODEL:2 * D_MODEL] * branch_c
    y = jnp.dot(mixed.astype(BF16), wo_ref[...], preferred_element_type=F32)
    h = _layer_norm(DN_ALPHA * x + y, g1_ref[...], b1_ref[...])
    _store_rows_as_tiles(h_ref, h, tile_scr)

    logits = jnp.dot(h.astype(BF16), wr_ref[...], preferred_element_type=F32) + br_ref[...]
    lane = lax.broadcasted_iota(I32, (tm, LANES), 1)
    lane_f = lane.astype(F32)
    big = jnp.float32(LANES)

    def first_lane(hit):
        return jnp.min(jnp.where(hit, lane_f, big), -1, keepdims=True).astype(I32)

    gl = jnp.where(lane < N_EXPERT_GROUPS, logits[:, 0:LANES], NEG_INF)
    gmax = jnp.max(gl, -1, keepdims=True)
    g_sel = first_lane(gl == gmax)
    p_g = 1.0 / jnp.sum(jnp.exp(gl - gmax), -1, keepdims=True)
    in_group = (lane < N_EXPERTS) & (lax.shift_right_logical(lane, 3) == g_sel)
    el = jnp.where(in_group, logits[:, LANES:2 * LANES], NEG_INF)
    v1 = jnp.max(el, -1, keepdims=True)
    i1 = first_lane(el == v1)
    el2 = jnp.where(lane == i1, NEG_INF, el)
    v2 = jnp.max(el2, -1, keepdims=True)
    i2 = first_lane(el2 == v2)
    e21 = jnp.exp(v2 - v1)
    gate1 = p_g / (1.0 + e21)
    gate2 = p_g * e21 / (1.0 + e21)

    oh = ((lane == i1) | (lane == i2)).astype(BF16)
    ri_ = lax.broadcasted_iota(I32, (tm, tm), 0)
    ci_ = lax.broadcasted_iota(I32, (tm, tm), 1)
    tri = (ci_ < ri_).astype(BF16)
    before = jnp.dot(tri, oh, preferred_element_type=F32) + base_scr[...]
    rank1 = jnp.sum(jnp.where(lane == i1, before, 0.0), -1, keepdims=True).astype(I32)
    rank2 = jnp.sum(jnp.where(lane == i2, before, 0.0), -1, keepdims=True).astype(I32)
    base_scr[...] = base_scr[...] + jnp.sum(oh.astype(F32), 0, keepdims=True)
    cnt_ref[...] = base_scr[...]

    ri_ref[...] = jnp.where(lane == 0, i1, jnp.where(lane == 1, i2, jnp.where(lane == 2, rank1,
                            jnp.where(lane == 3, rank2, 0))))
    rg_ref[...] = jnp.where(lane == 0, gate1, jnp.where(lane == 1, gate2, 0.0))


def _merge(acts_p, acts_s, weights, tm):
    n_p, n_s = acts_p[0].shape[0], acts_s[0].shape[0]
    assert n_p % tm == 0 and n_s % tm == 0
    pt = n_p // tm
    n_all = n_p + n_s
    prow = lambda i: (jnp.minimum(i, pt - 1), 0)
    srow = lambda i: (jnp.maximum(i - pt, 0), 0)
    row = lambda i: (i, 0)
    whole = lambda i: (0, 0)
    act_args, act_specs = [], []
    for a_p, a_s in zip(acts_p, acts_s):
        act_args += [a_p, a_s]
        act_specs += [pl.BlockSpec((tm, a_p.shape[1]), prow), pl.BlockSpec((tm, a_s.shape[1]), srow)]
    return pl.pallas_call(
        functools.partial(_merge_kernel, prompt_tiles=pt),
        grid=(n_all // tm,),
        in_specs=act_specs + [pl.BlockSpec(w.shape, whole) for w in weights],
        out_specs=[pl.BlockSpec((tm, ROW_SUB, LANES), lambda i: (i, 0, 0)), pl.BlockSpec((tm, LANES), row),
                   pl.BlockSpec((tm, LANES), row), pl.BlockSpec((1, LANES), whole)],
        out_shape=[jax.ShapeDtypeStruct((n_all, ROW_SUB, LANES), F32), jax.ShapeDtypeStruct((n_all, LANES), I32),
                   jax.ShapeDtypeStruct((n_all, LANES), F32), jax.ShapeDtypeStruct((1, LANES), F32)],
        scratch_shapes=[pltpu.VMEM((1, LANES), F32), pltpu.VMEM((tm // 8, 8 * ROW_SUB, LANES), F32)],
        compiler_params=_cparams(("arbitrary",)),
        name="merge",
    )(*act_args, *weights)


def _dispatch_kernel(bounds_ref, nv_ref, dest_ref, h_ref, xg_any, hbuf, zero_scr, sem_z, sem_r):
    i = pl.program_id(0)
    tm = h_ref.shape[0]
    n_chunks = xg_any.shape[0] // CHUNK
    slot = i % 2

    def zero_chunk(row0):
        return pltpu.make_async_copy(zero_scr, xg_any.at[pl.ds(pl.multiple_of(row0, CHUNK), CHUNK)], sem_z)

    @pl.when(i == 0)
    def _():
        zero_scr[...] = jnp.zeros_like(zero_scr)
        for e in range(N_EXPERTS):
            @pl.when(bounds_ref[e + 1] > bounds_ref[e])
            def _():
                zero_chunk(bounds_ref[e + 1] - CHUNK).start()

        def start_tail(c, carry):
            zero_chunk(c * CHUNK).start()
            return carry

        lax.fori_loop(nv_ref[0], n_chunks, start_tail, 0)
        for e in range(N_EXPERTS):
            @pl.when(bounds_ref[e + 1] > bounds_ref[e])
            def _():
                zero_chunk(0).wait()

        def wait_tail(c, carry):
            zero_chunk(0).wait()
            return carry

        lax.fori_loop(nv_ref[0], n_chunks, wait_tail, 0)

    def wait_rows(sl):
        for _ in range(2 * tm):
            pltpu.make_async_copy(hbuf.at[0, 0], xg_any.at[0], sem_r.at[sl]).wait()

    @pl.when(i >= 2)
    def _():
        wait_rows(slot)

    hbuf[slot] = h_ref[...]
    for r in range(2 * tm):
        pltpu.make_async_copy(hbuf.at[slot, r % tm], xg_any.at[dest_ref[0, 0, r]],
                              sem_r.at[slot]).start(priority=r % 2)

    @pl.when(i == pl.num_programs(0) - 1)
    def _():
        wait_rows(slot)

        @pl.when(i >= 1)
        def _():
            wait_rows(1 - slot)


def _dispatch(bounds, n_valid, dest3, h3, n_rows, tm):
    n = h3.shape[0]
    return pl.pallas_call(
        _dispatch_kernel,
        grid_spec=pltpu.PrefetchScalarGridSpec(
            num_scalar_prefetch=2,
            grid=(n // tm,),
            in_specs=[pl.BlockSpec((1, 1, 2 * tm), lambda i, b, nv: (i, 0, 0), memory_space=pltpu.SMEM),
                      pl.BlockSpec((tm, ROW_SUB, LANES), lambda i, b, nv: (i, 0, 0))],
            out_specs=pl.BlockSpec(memory_space=pl.ANY),
            scratch_shapes=[pltpu.VMEM((2, tm, ROW_SUB, LANES), F32), pltpu.VMEM((CHUNK, ROW_SUB, LANES), F32),
                            pltpu.SemaphoreType.DMA(()), pltpu.SemaphoreType.DMA((2,))],
        ),
        out_shape=jax.ShapeDtypeStruct((n_rows, ROW_SUB, LANES), F32),
        compiler_params=_cparams(("arbitrary",)),
        name="moe_dispatch",
    )(bounds, n_valid, dest3, h3)


def _experts_kernel(ce_ref, nv_ref, x_ref, wg_ref, wu_ref, wd_ref, y_ref, wg_scr, wu_scr, wd_scr, tile_scr):
    c = pl.program_id(0)
    nv = nv_ref[0]

    @pl.when((c < nv) & ((c == 0) | (ce_ref[c] != ce_ref[jnp.maximum(c - 1, 0)])))
    def _():
        wg_scr[...] = wg_ref[0].astype(BF16)
        wu_scr[...] = wu_ref[0].astype(BF16)
        wd_scr[...] = wd_ref[0].astype(BF16)

    @pl.when(c < nv)
    def _():
        x = _load_tiles_as_rows(x_ref, tile_scr).astype(BF16)
        a = jnp.dot(x, wg_scr[...], preferred_element_type=F32)
        b = jnp.dot(x, wu_scr[...], preferred_element_type=F32)
        y = jnp.dot((jax.nn.silu(a) * b).astype(BF16), wd_scr[...], preferred_element_type=F32)
        _store_rows_as_tiles(y_ref, y, tile_scr)

    @pl.when(c >= nv)
    def _():
        y_ref[...] = jnp.zeros_like(y_ref)


def _experts(chunk_expert, n_valid, xg, w_eg, w_eu, w_ed):
    n_chunks = xg.shape[0] // CHUNK
    last = lambda nv: jnp.maximum(nv[0] - 1, 0)
    wsel = lambda c, ce, nv: (ce[jnp.minimum(c, last(nv))], 0, 0)
    return pl.pallas_call(
        _experts_kernel,
        grid_spec=pltpu.PrefetchScalarGridSpec(
            num_scalar_prefetch=2,
            grid=(n_chunks,),
            in_specs=[pl.BlockSpec((CHUNK, ROW_SUB, LANES), lambda c, ce, nv: (jnp.minimum(c, last(nv)), 0, 0)),
                      pl.BlockSpec((1, D_MODEL, D_EXPERT), wsel), pl.BlockSpec((1, D_MODEL, D_EXPERT), wsel),
                      pl.BlockSpec((1, D_EXPERT, D_MODEL), wsel)],
            out_specs=pl.BlockSpec((CHUNK, ROW_SUB, LANES), lambda c, ce, nv: (c, 0, 0)),
            scratch_shapes=[pltpu.VMEM((D_MODEL, D_EXPERT), BF16), pltpu.VMEM((D_MODEL, D_EXPERT), BF16),
                            pltpu.VMEM((D_EXPERT, D_MODEL), BF16),
                            pltpu.VMEM((CHUNK // 8, 8 * ROW_SUB, LANES), F32)],
        ),
        out_shape=jax.ShapeDtypeStruct(xg.shape, F32),
        compiler_params=_cparams(("arbitrary",)),
        name="moe_experts",
    )(chunk_expert, n_valid, xg, w_eg, w_eu, w_ed)


def _combine_kernel(dest_ref, dest_next_ref, h_ref, rg_ref, g2_ref, b2_ref, yg_any, yp_ref, ys_ref,
                    ybuf, t0_scr, t1_scr, t2_scr, sem, *, prompt_tiles):
    i = pl.program_id(0)
    tm = h_ref.shape[0]
    slot = i % 2

    def start_gather(src_ref, sl):
        for r in range(2 * tm):
            pltpu.make_async_copy(yg_any.at[src_ref[0, 0, r]], ybuf.at[sl, r], sem.at[sl]).start(priority=r % 2)

    @pl.when(i == 0)
    def _():
        start_gather(dest_ref, 0)

    @pl.when(i + 1 < pl.num_programs(0))
    def _():
        start_gather(dest_next_ref, 1 - slot)

    for _ in range(2 * tm):
        pltpu.make_async_copy(yg_any.at[0], ybuf.at[0, 0], sem.at[slot]).wait()
    moe = (_load_tiles_as_rows(ybuf.at[slot, pl.ds(0, tm)], t1_scr) * rg_ref[:, 0:1]
           + _load_tiles_as_rows(ybuf.at[slot, pl.ds(tm, tm)], t2_scr) * rg_ref[:, 1:2])
    out = _layer_norm(DN_ALPHA * _load_tiles_as_rows(h_ref, t0_scr) + moe, g2_ref[...], b2_ref[...])

    @pl.when(i < prompt_tiles)
    def _():
        yp_ref[...] = out

    @pl.when(i >= prompt_tiles)
    def _():
        ys_ref[...] = out


def _combine(dest3, h3, yg, route_g, ln2_g, ln2_b, n_prompt, tm):
    n = h3.shape[0]
    pt, nt = n_prompt // tm, n // tm
    row = lambda i: (i, 0)
    whole = lambda i: (0, 0)
    return pl.pallas_call(
        functools.partial(_combine_kernel, prompt_tiles=pt),
        grid=(nt,),
        in_specs=[pl.BlockSpec((1, 1, 2 * tm), lambda i: (i, 0, 0), memory_space=pltpu.SMEM),
                  pl.BlockSpec((1, 1, 2 * tm), lambda i: (jnp.minimum(i + 1, nt - 1), 0, 0), memory_space=pltpu.SMEM),
                  pl.BlockSpec((tm, ROW_SUB, LANES), lambda i: (i, 0, 0)), pl.BlockSpec((tm, LANES), row),
                  pl.BlockSpec((1, D_MODEL), whole), pl.BlockSpec((1, D_MODEL), whole),
                  pl.BlockSpec(memory_space=pl.ANY)],
        out_specs=[pl.BlockSpec((tm, D_MODEL), lambda i: (jnp.minimum(i, pt - 1), 0)),
                   pl.BlockSpec((tm, D_MODEL), lambda i: (jnp.maximum(i - pt, 0), 0))],
        out_shape=[jax.ShapeDtypeStruct((n_prompt, D_MODEL), F32), jax.ShapeDtypeStruct((n - n_prompt, D_MODEL), F32)],
        scratch_shapes=[pltpu.VMEM((2, 2 * tm, ROW_SUB, LANES), F32)]
        + [pltpu.VMEM((tm // 8, 8 * ROW_SUB, LANES), F32) for _ in range(3)] + [pltpu.SemaphoreType.DMA((2,))],
        compiler_params=_cparams(("arbitrary",)),
        name="moe_combine",
    )(dest3, dest3, h3, route_g, ln2_g, ln2_b, yg)


def kernel(x_prompt, x_sample, cache_kv_w128, cache_kv_w512, cache_kv_w2048, state_conv, w_in, b_in, w_dw, b_dw, conv_ln_g, conv_ln_b, w_conv_out, b_conv_out, w_attn_out, w_o, ln1_g, ln1_b, w_router_group, b_router_group, w_router_expert, b_router_expert, w_expert_gate, w_expert_up, w_expert_down, ln2_g, ln2_b):
    assert w_in.shape[0] == DEPTH == 1
    bsz, seq, _ = x_prompt.shape
    db, t_new, _ = x_sample.shape
    caches = (cache_kv_w128, cache_kv_w512, cache_kv_w2048)
    past = cache_kv_w2048.shape[2]
    n_p, n_s = bsz * seq, db * t_new
    n_all = n_p + n_s
    tm = ROW_TILE
    kvs = (HEADS_PER_GROUP, HEAD_DIM)

    w_bf = w_in[0].astype(BF16)
    xp2, xs2 = x_prompt.reshape(n_p, D_MODEL), x_sample.reshape(n_s, D_MODEL)
    q0, q1, q2, kvf_p, u_p, gate_p = _in_proj(xp2, w_bf, b_in, _rotary_tables(np.arange(seq)), tm, True)
    pos_s = past + np.arange(tm) % t_new
    q_s, kvf_s, u_s, gate_s = _in_proj(xs2, w_bf, b_in, _rotary_tables(pos_s), tm, False)

    attn_p = _prompt_attn((q0, q1, q2), bsz, seq)

    kv_new = kvf_s.reshape(db, t_new, N_GROUPS, KV_ROWS)
    caches_t = [jnp.transpose(c[0], (0, 2, 3, 4, 1)).reshape(db, KV_ROWS, c.shape[2]) for c in caches]
    news = [jnp.pad(kv_new[:, :, g], ((0, 0), (0, -t_new % 8), (0, 0))) for g in range(N_GROUPS)]
    attn_s, kv_sample_t, conv_p = _sample_attn(q_s.reshape(db, t_new, ATTN_WIDTH), caches_t, news,
                                               u_p, seq, w_dw[0], b_dw)
    kv_sample = [jnp.transpose(o.reshape(db, 2, *kvs, o.shape[2]), (0, 4, 1, 2, 3))[None] for o in kv_sample_t]

    conv_s, conv_state_s = _sample_conv(state_conv[0], u_s.reshape(db, t_new, CONV_CH), w_dw[0], b_dw)

    w_r = jnp.zeros((D_MODEL, ROUTE_LANES), F32)
    w_r = w_r.at[:, 0:N_EXPERT_GROUPS].set(w_router_group[0]).at[:, LANES:LANES + N_EXPERTS].set(w_router_expert[0])
    b_r = jnp.zeros((1, ROUTE_LANES), F32)
    b_r = b_r.at[:, 0:N_EXPERT_GROUPS].set(b_router_group).at[:, LANES:LANES + N_EXPERTS].set(b_router_expert)
    weights = [conv_ln_g, conv_ln_b, w_conv_out[0].astype(BF16), b_conv_out, w_attn_out[0].astype(BF16),
               w_o[0].astype(BF16), ln1_g, ln1_b, w_r.astype(BF16), b_r]
    acts_p = (xp2, conv_p, attn_p, gate_p)
    acts_s = (xs2, conv_s.reshape(n_s, CONV_CH), attn_s.reshape(n_s, GROUP_W), gate_s)
    h3, route_i, route_g, counts = _merge(acts_p, acts_s, weights, tm)

    cnt = counts[0, :N_EXPERTS].astype(I32)
    padded = ((cnt + CHUNK - 1) // CHUNK) * CHUNK
    pad_end = jnp.cumsum(padded)
    bounds = jnp.concatenate([jnp.zeros((1,), I32), pad_end]).astype(I32)
    n_chunks = (2 * n_all) // CHUNK + N_EXPERTS
    chunk_row0 = jnp.arange(n_chunks, dtype=I32) * CHUNK
    chunk_expert = jnp.minimum(jnp.sum((pad_end[None, :] <= chunk_row0[:, None]).astype(I32), axis=1),
                               N_EXPERTS - 1).astype(I32)
    n_valid = (pad_end[-1:] // CHUNK).astype(I32)

    e_onehot = route_i[:, 0:2, None] == jnp.arange(N_EXPERTS, dtype=I32)[None, None, :]
    dest = jnp.sum(jnp.where(e_onehot, bounds[None, None, :N_EXPERTS], 0), axis=-1) + route_i[:, 2:4]
    dest3 = jnp.transpose(dest.reshape(n_all // tm, tm, 2), (0, 2, 1)).reshape(n_all // tm, 1, 2 * tm)
    xg = _dispatch(bounds, n_valid, dest3, h3, n_chunks * CHUNK, tm)
    yg = _experts(chunk_expert, n_valid, xg, w_expert_gate[0], w_expert_up[0], w_expert_down[0])
    y_p, y_s = _combine(dest3, h3, yg, route_g, ln2_g, ln2_b, n_p, tm)

    kvf3 = kvf_p.reshape(bsz, seq, N_GROUPS * KV_ROWS)
    kv_prompt = []
    for g, (w, _) in enumerate(ATTN_PATTERNS):
        win_t = _kv_window(kvf3, g, w)
        kv_prompt.append(jnp.transpose(win_t.reshape(bsz, 2, *kvs, win_t.shape[2]), (0, 4, 1, 2, 3))[None])
    conv_prompt = u_p.reshape(1, bsz, seq, CONV_CH)[:, :, seq - (CONV_WIDTH - 1):]
    return (y_p.reshape(bsz, seq, D_MODEL), y_s.reshape(db, t_new, D_MODEL),
            kv_prompt[0], kv_prompt[1], kv_prompt[2], conv_prompt,
            kv_sample[0], kv_sample[1], kv_sample[2], conv_state_s[None])
```

```python
import functools

import jax
import jax.numpy as jnp
import numpy as np
from jax import lax
from jax.experimental import pallas as pl
from jax.experimental.pallas import tpu as pltpu

F32 = jnp.float32
BF16 = jnp.bfloat16
I32 = jnp.int32

D_MODEL = 1024
HEAD_DIM = 64
HEADS_PER_GROUP = 4
GROUP_W = HEADS_PER_GROUP * HEAD_DIM
ATTN_PATTERNS = ((128, 1), (512, 4), (2048, 16))
N_GROUPS = len(ATTN_PATTERNS)
ATTN_WIDTH = N_GROUPS * GROUP_W
QKV_W = 3 * GROUP_W
STEPS = 128
ROT_HALF = 8
ROPE_THETA = 500000.0
CONV_CH = 512
CONV_WIDTH = 31
HALO = 32
N_EXPERTS = 32
N_EXPERT_GROUPS = 4
D_EXPERT = 512
DEPTH = 1
DN_ALPHA = (2 * DEPTH) ** 0.25
LN_EPS = 1e-5
SCALE = HEAD_DIM ** -0.5
NEG_INF = -1e30
LANES = 128
ROW_SUB = D_MODEL // LANES
KV_ROWS = 2 * GROUP_W
ROW_TILE = 256
CHUNK = 256
VMEM_LIMIT = 56 * 1024 * 1024

_Q0, _K0, _V0 = 0, ATTN_WIDTH, 2 * ATTN_WIDTH
_UA0 = 3 * ATTN_WIDTH
_UB0 = _UA0 + CONV_CH
_G0 = _UB0 + CONV_CH
IN_WIDTH = _G0 + 2 * D_MODEL


def _cparams(sem):
    return pltpu.CompilerParams(dimension_semantics=sem, vmem_limit_bytes=VMEM_LIMIT)


def _layer_norm(x, g, b):
    mu = jnp.mean(x, -1, keepdims=True)
    xc = x - mu
    var = jnp.mean(xc * xc, -1, keepdims=True)
    return xc * lax.rsqrt(var + LN_EPS) * g + b


def _head_masks(shape, lane_axis):
    lane = lax.broadcasted_iota(I32, shape, lane_axis)
    return [(lane >= h * HEAD_DIM) & (lane < (h + 1) * HEAD_DIM) for h in range(HEADS_PER_GROUP)]


def _store_rows_as_tiles(ref3, val2d, tmp):
    sub = 8
    groups = val2d.shape[0] // sub
    for c in range(ROW_SUB):
        tmp[:, c * sub:(c + 1) * sub, :] = val2d[:, c * LANES:(c + 1) * LANES].reshape(groups, sub, LANES)
    for s in range(sub):
        ref3[pl.ds(s, groups, stride=sub), :, :] = tmp[:, pl.ds(s, ROW_SUB, stride=sub), :]


def _load_tiles_as_rows(ref3, tmp):
    sub = 8
    groups = ref3.shape[0] // sub
    for s in range(sub):
        tmp[:, pl.ds(s, ROW_SUB, stride=sub), :] = ref3[pl.ds(s, groups, stride=sub), :, :]
    return jnp.concatenate([tmp[:, c * sub:(c + 1) * sub, :].reshape(groups * sub, LANES) for c in range(ROW_SUB)],
                           axis=1)


def _rotary_tables(pos):
    pos = np.asarray(pos, np.float64)
    inv_freq = ROPE_THETA ** (-np.arange(ROT_HALF, dtype=np.float64) / ROT_HALF)
    ang = pos[:, None] * inv_freq[None, :]
    cos, sin = np.cos(ang), np.sin(ang)
    zero = np.zeros_like(cos)
    p = pos.shape[0]
    rest = HEAD_DIM - 2 * ROT_HALF
    c_head = np.concatenate([cos, cos, np.ones((p, rest))], -1)
    sa_head = np.concatenate([zero, sin, np.zeros((p, rest))], -1)
    sb_head = np.concatenate([-sin, zero, np.zeros((p, rest))], -1)
    rep = LANES // HEAD_DIM
    return tuple(jnp.asarray(np.tile(t, (1, rep)), F32) for t in (c_head, sa_head, sb_head))


def _causal_conv_rows(ext_scr, sh_scr, wdw_ref, bdw_ref, conv_ref):
    n = conv_ref.shape[0]
    off = HALO - (CONV_WIDTH - 1)
    sub = 8
    for s in range(sub):
        rows = n + sub * ((CONV_WIDTH - s + sub - 1) // sub - 1)
        sh_scr[s, 0:rows, :] = ext_scr[off + s:off + s + rows, :]
    rb = 64
    for cc in range(CONV_CH // LANES):
        cs = slice(cc * LANES, (cc + 1) * LANES)
        for r0 in range(0, n, rb):
            acc = jnp.broadcast_to(bdw_ref[:, cs], (rb, LANES))
            for j in range(CONV_WIDTH):
                a0 = sub * (j // sub) + r0
                acc = acc + sh_scr[j % sub, a0:a0 + rb, cs] * wdw_ref[j:j + 1, cs]
            conv_ref[r0:r0 + rb, cs] = acc


def _in_proj_kernel(*refs, prompt):
    if prompt:
        (x_ref, w_ref, b_ref, c_ref, sa_ref, sb_ref,
         q0_ref, q1_ref, q2_ref, kvf_ref, u_ref, gate_ref, rm_scr) = refs
    else:
        (x_ref, w_ref, b_ref, c_ref, sa_ref, sb_ref, qs_ref, kvf_ref, u_ref, gate_ref) = refs
    tm = x_ref.shape[0]
    xb = x_ref[...].astype(BF16)

    def mm(c0, c1):
        return jnp.dot(xb, w_ref[:, c0:c1], preferred_element_type=F32) + b_ref[:, c0:c1]

    u_ref[...] = mm(_UA0, _UB0) * jax.nn.sigmoid(mm(_UB0, _G0))
    c, sa, sb = c_ref[...], sa_ref[...], sb_ref[...]

    def rotary(z):
        parts = []
        for j in range(z.shape[1] // LANES):
            zc = z[:, j * LANES:(j + 1) * LANES]
            parts.append(zc * c + pltpu.roll(zc, ROT_HALF, 1) * sa + pltpu.roll(zc, LANES - ROT_HALF, 1) * sb)
        return jnp.concatenate(parts, axis=1)

    q = rotary(mm(_Q0, _K0)) * SCALE
    k = rotary(mm(_K0, _V0))
    v = mm(_V0, _UA0)
    for g in range(N_GROUPS):
        gs = slice(g * GROUP_W, (g + 1) * GROUP_W)
        kvf_ref[:, 2 * g * GROUP_W:(2 * g + 1) * GROUP_W] = k[:, gs]
        kvf_ref[:, (2 * g + 1) * GROUP_W:(2 * g + 2) * GROUP_W] = v[:, gs]
    gate_ref[...] = jax.nn.sigmoid(mm(_G0, IN_WIDTH))

    if not prompt:
        qs_ref[...] = q
        return

    for g, out_ref in enumerate((q0_ref, q1_ref, q2_ref)):
        dil = ATTN_PATTERNS[g][1]
        gs = slice(g * GROUP_W, (g + 1) * GROUP_W)
        qkv = jnp.concatenate([q[:, gs], k[:, gs], v[:, gs]], axis=1)
        if dil == 1:
            out_ref[...] = qkv.astype(BF16)
        else:
            nck = QKV_W // LANES
            for ck in range(nck):
                rm_scr[ck] = qkv[:, ck * LANES:(ck + 1) * LANES]
            for r in range(dil):
                blk = jnp.concatenate([rm_scr[ck, pl.ds(r, tm // dil, stride=dil), :] for ck in range(nck)], axis=1)
                out_ref[:, r * QKV_W:(r + 1) * QKV_W] = blk.astype(BF16)


def _in_proj(x2d, w_bf, b2d, tables, tm, prompt):
    n = x2d.shape[0]
    p = tables[0].shape[0]
    assert n % tm == 0 and p % tm == 0
    tpb = p // tm
    row = lambda i: (i, 0)
    tab = lambda i: (i % tpb, 0)
    whole = lambda i: (0, 0)
    in_specs = [pl.BlockSpec((tm, D_MODEL), row), pl.BlockSpec((D_MODEL, IN_WIDTH), whole),
                pl.BlockSpec((1, IN_WIDTH), whole)] + [pl.BlockSpec((tm, LANES), tab)] * 3
    common_specs = [pl.BlockSpec((tm, 2 * ATTN_WIDTH), row), pl.BlockSpec((tm, CONV_CH), row)]
    common_shapes = [jax.ShapeDtypeStruct((n, 2 * ATTN_WIDTH), F32),
                     jax.ShapeDtypeStruct((n, CONV_CH), F32)]
    gate_spec, gate_shape = pl.BlockSpec((tm, 2 * D_MODEL), row), jax.ShapeDtypeStruct((n, 2 * D_MODEL), F32)
    if prompt:
        q_specs = [pl.BlockSpec((tm // d, d * QKV_W), row) for _, d in ATTN_PATTERNS]
        q_shapes = [jax.ShapeDtypeStruct((n // d, d * QKV_W), BF16) for _, d in ATTN_PATTERNS]
        out_specs = q_specs + common_specs + [gate_spec]
        out_shape = q_shapes + common_shapes + [gate_shape]
        scratch = [pltpu.VMEM((QKV_W // LANES, tm, LANES), F32)]
    else:
        out_specs = [pl.BlockSpec((tm, ATTN_WIDTH), row)] + common_specs + [gate_spec]
        out_shape = [jax.ShapeDtypeStruct((n, ATTN_WIDTH), F32)] + common_shapes + [gate_shape]
        scratch = []
    return pl.pallas_call(
        functools.partial(_in_proj_kernel, prompt=prompt),
        grid=(n // tm,),
        in_specs=in_specs,
        out_specs=out_specs,
        out_shape=out_shape,
        scratch_shapes=scratch,
        compiler_params=_cparams(("parallel",)),
        name="in_proj_prompt" if prompt else "in_proj_sample",
    )(x2d, w_bf, b2d, *tables)


def _attn_block(q, keys, vals, first):
    masks = _head_masks(q.shape, 1)
    zero = jnp.zeros_like(q)
    q_stack = jnp.concatenate([jnp.where(m, q, zero) for m in masks], axis=0)
    s = lax.dot_general(q_stack, keys, (((1,), (1,)), ((), ())), preferred_element_type=F32)
    qi = lax.broadcasted_iota(I32, s.shape, 0) % STEPS
    kj = lax.broadcasted_iota(I32, s.shape, 1)
    ok = (kj <= qi) if first else ((kj >= qi) & (kj <= qi + STEPS))
    s = jnp.where(ok, s, NEG_INF)
    m = jnp.max(s, -1, keepdims=True)
    p = jnp.exp(s - m)
    l = jnp.sum(p, -1, keepdims=True)
    pv = jnp.dot(p.astype(BF16), vals, preferred_element_type=F32)
    lse = m + jnp.log(l)
    o = jnp.zeros((STEPS, GROUP_W), F32)
    lse_o = jnp.zeros((STEPS, GROUP_W), F32)
    for h, hm in enumerate(masks):
        rs = slice(h * STEPS, (h + 1) * STEPS)
        o = jnp.where(hm, pv[rs] / l[rs], o)
        lse_o = jnp.where(hm, lse[rs], lse_o)
    return o, lse_o


def _prompt_attn_kernel(q0_ref, q1_ref, q2_ref, attn_ref, o_rm1, l_rm1, o_rm2, l_rm2, o_nat, l_nat):
    halves = GROUP_W // LANES

    def put_nat(g, rows, o, lse):
        for hf in range(halves):
            hs = slice(hf * LANES, (hf + 1) * LANES)
            o_nat[g * halves + hf, rows, :] = o[:, hs]
            l_nat[g * halves + hf, rows, :] = lse[:, hs]

    for g, ref in enumerate((q0_ref, q1_ref, q2_ref)):
        dil = ATTN_PATTERNS[g][1]
        o_rm, l_rm = (None, o_rm1, o_rm2)[g], (None, l_rm1, l_rm2)[g]
        rows = ref.shape[0]
        nb = rows // STEPS
        for r in range(dil):
            base = r * QKV_W
            qc, kc, vc = (slice(base + j * GROUP_W, base + (j + 1) * GROUP_W) for j in range(3))
            oc = slice(r * GROUP_W, (r + 1) * GROUP_W)
            o, lse = _attn_block(ref[0:STEPS, qc], ref[0:STEPS, kc], ref[0:STEPS, vc], True)
            if dil == 1:
                put_nat(g, slice(0, STEPS), o, lse)
            else:
                o_rm[0:STEPS, oc] = o
                l_rm[0:STEPS, oc] = lse

            def body(n, carry, ref=ref, qc=qc, kc=kc, vc=vc, oc=oc, dil=dil, g=g, o_rm=o_rm, l_rm=l_rm):
                q0 = pl.multiple_of(n * STEPS, STEPS)
                k0 = pl.multiple_of((n - 1) * STEPS, STEPS)
                o, lse = _attn_block(ref[pl.ds(q0, STEPS), qc], ref[pl.ds(k0, 2 * STEPS), kc],
                                     ref[pl.ds(k0, 2 * STEPS), vc], False)
                if dil == 1:
                    put_nat(g, pl.ds(q0, STEPS), o, lse)
                else:
                    o_rm[pl.ds(q0, STEPS), oc] = o
                    l_rm[pl.ds(q0, STEPS), oc] = lse
                return carry

            if nb > 1:
                lax.fori_loop(1, nb, body, 0, unroll=3)
        if dil > 1:
            for r in range(dil):
                oc = slice(r * GROUP_W, (r + 1) * GROUP_W)
                put_nat(g, pl.ds(r, rows, stride=dil), o_rm[0:rows, oc], l_rm[0:rows, oc])

    def nat(ref, g):
        return jnp.concatenate([ref[g * halves + hf] for hf in range(halves)], axis=1)

    lses = [nat(l_nat, g) for g in range(N_GROUPS)]
    mx = jnp.maximum(jnp.maximum(lses[0], lses[1]), lses[2])
    ws = [jnp.exp(ls - mx) for ls in lses]
    attn_ref[...] = (nat(o_nat, 0) * ws[0] + nat(o_nat, 1) * ws[1] + nat(o_nat, 2) * ws[2]) / (ws[0] + ws[1] + ws[2])


def _prompt_attn(qkv_groups, bsz, seq):
    row = lambda b: (b, 0)
    in_specs = [pl.BlockSpec((seq // d, d * QKV_W), row) for _, d in ATTN_PATTERNS]
    return pl.pallas_call(
        _prompt_attn_kernel,
        grid=(bsz,),
        in_specs=in_specs,
        out_specs=pl.BlockSpec((seq, GROUP_W), row),
        out_shape=jax.ShapeDtypeStruct((bsz * seq, GROUP_W), F32),
        scratch_shapes=[pltpu.VMEM((seq // d, d * GROUP_W), F32) for _, d in ATTN_PATTERNS[1:] for _ in range(2)]
        + [pltpu.VMEM((N_GROUPS * (GROUP_W // LANES), seq, LANES), F32) for _ in range(2)],
        compiler_params=_cparams(("parallel",)),
        name="prompt_attn",
    )(*qkv_groups)


def _sample_attn_kernel(q_ref, c0_ref, c1_ref, c2_ref, n0_ref, n1_ref, n2_ref, u_ref, halo_ref, wdw_ref, bdw_ref,
                        attn_ref, o0_ref, o1_ref, o2_ref, conv_ref, ext_scr, sh_scr, *, t_new, conv_tiles_per_seq):
    n_conv = u_ref.shape[0]
    first = pl.program_id(0) % conv_tiles_per_seq == 0
    ext_scr[0:HALO, :] = jnp.where(first, 0.0, halo_ref[...])
    ext_scr[HALO:HALO + n_conv, :] = u_ref[...]
    _causal_conv_rows(ext_scr, sh_scr, wdw_ref, bdw_ref, conv_ref)

    q = q_ref[0]
    outs, lses = [], []
    for g, (cref, nref, oref) in enumerate(((c0_ref, n0_ref, o0_ref), (c1_ref, n1_ref, o1_ref),
                                            (c2_ref, n2_ref, o2_ref))):
        dil = ATTN_PATTERNS[g][1]
        wb = cref.shape[2]
        new_t = jnp.concatenate([nref[0], jnp.zeros((LANES - nref.shape[1], KV_ROWS), F32)], axis=0).T
        full = jnp.concatenate([cref[0], new_t], axis=1)
        oref[0] = full[:, t_new:wb + t_new]
        k_t = full[0:GROUP_W].astype(BF16)
        v_t = full[GROUP_W:KV_ROWS].astype(BF16)
        qg = q[:, g * GROUP_W:(g + 1) * GROUP_W]
        masks = _head_masks(qg.shape, 1)
        q_stack = jnp.concatenate([jnp.where(m, qg, 0.0) for m in masks], axis=0).astype(BF16)
        s = jnp.dot(q_stack, k_t, preferred_element_type=F32)
        r = lax.broadcasted_iota(I32, s.shape, 1)
        t = lax.broadcasted_iota(I32, s.shape, 0) % t_new
        ok = (r >= t) & (r <= wb + t) & (((r - t) & (dil - 1)) == 0)
        s = jnp.where(ok, s, NEG_INF)
        m = jnp.max(s, -1, keepdims=True)
        p = jnp.exp(s - m)
        l = jnp.sum(p, -1, keepdims=True)
        pv = lax.dot_general(p.astype(BF16), v_t, (((1,), (1,)), ((), ())), preferred_element_type=F32)
        lse = m + jnp.log(l)
        o = jnp.zeros((t_new, GROUP_W), F32)
        lse_o = jnp.zeros((t_new, GROUP_W), F32)
        for h, hm in enumerate(masks):
            rs = slice(h * t_new, (h + 1) * t_new)
            o = jnp.where(hm, pv[rs] / l[rs], o)
            lse_o = jnp.where(hm, lse[rs], lse_o)
        outs.append(o)
        lses.append(lse_o)
    mx = jnp.maximum(jnp.maximum(lses[0], lses[1]), lses[2])
    ws = [jnp.exp(ls - mx) for ls in lses]
    attn_ref[0] = (outs[0] * ws[0] + outs[1] * ws[1] + outs[2] * ws[2]) / (ws[0] + ws[1] + ws[2])


def _sample_attn(q3, caches_t, news, u_prompt, prompt_seq, w_dw, b_dw):
    db, t_new, _ = q3.shape
    for c, (_, dil) in zip(caches_t, ATTN_PATTERNS):
        assert c.shape[2] == STEPS * dil and t_new <= LANES and dil & (dil - 1) == 0
    n_conv = u_prompt.shape[0] // db
    assert u_prompt.shape[0] == n_conv * db and n_conv % 64 == 0 and n_conv % HALO == 0 and prompt_seq % n_conv == 0
    blk = lambda b: (b, 0, 0)
    row = lambda b: (b, 0)
    whole = lambda b: (0, 0)
    cache_specs = [pl.BlockSpec((1, KV_ROWS, c.shape[2]), blk) for c in caches_t]
    halo_spec = pl.BlockSpec((HALO, CONV_CH), lambda b: (jnp.maximum(b * (n_conv // HALO) - 1, 0), 0))
    outs = pl.pallas_call(
        functools.partial(_sample_attn_kernel, t_new=t_new, conv_tiles_per_seq=prompt_seq // n_conv),
        grid=(db,),
        in_specs=[pl.BlockSpec((1, t_new, ATTN_WIDTH), blk)] + cache_specs
        + [pl.BlockSpec((1, news[0].shape[1], KV_ROWS), blk)] * 3
        + [pl.BlockSpec((n_conv, CONV_CH), row), halo_spec,
           pl.BlockSpec((CONV_WIDTH, CONV_CH), whole), pl.BlockSpec((1, CONV_CH), whole)],
        out_specs=[pl.BlockSpec((1, t_new, GROUP_W), blk)] + cache_specs + [pl.BlockSpec((n_conv, CONV_CH), row)],
        out_shape=[jax.ShapeDtypeStruct((db, t_new, GROUP_W), F32)]
        + [jax.ShapeDtypeStruct(c.shape, c.dtype) for c in caches_t]
        + [jax.ShapeDtypeStruct(u_prompt.shape, F32)],
        scratch_shapes=[pltpu.VMEM((HALO + n_conv, CONV_CH), F32), pltpu.VMEM((8, HALO + n_conv, CONV_CH), F32)],
        compiler_params=_cparams(("parallel",)),
        name="sample_attn",
    )(q3, *caches_t, *news, u_prompt, u_prompt, w_dw, b_dw)
    return outs[0], outs[1:4], outs[4]


def _kv_window_kernel(x_ref, o_ref):
    o_ref[0] = x_ref[0].T


def _kv_window(kvf3, g, window):
    bsz, seq, _ = kvf3.shape
    w = min(window, seq)
    rows = min(w, 512)
    assert w % rows == 0 and (seq - w) % rows == 0
    first = (seq - w) // rows
    return pl.pallas_call(
        _kv_window_kernel,
        grid=(bsz, w // rows),
        in_specs=[pl.BlockSpec((1, rows, KV_ROWS), lambda b, n: (b, first + n, g))],
        out_specs=pl.BlockSpec((1, KV_ROWS, rows), lambda b, n: (b, 0, n)),
        out_shape=jax.ShapeDtypeStruct((bsz, KV_ROWS, w), F32),
        compiler_params=_cparams(("parallel", "parallel")),
        name=f"kv_window_g{g}",
    )(kvf3)


def _sample_conv_kernel(state_ref, u_ref, w_ref, b_ref, conv_ref, sout_ref):
    hist = state_ref.shape[1]
    t_new = u_ref.shape[1]
    bb = u_ref.shape[0]
    for t in range(t_new):
        acc = jnp.broadcast_to(b_ref[...], (bb, CONV_CH))
        for j in range(CONV_WIDTH):
            r = t + j
            row = state_ref[:, r, :] if r < hist else u_ref[:, r - hist, :]
            acc = acc + row * w_ref[j:j + 1, :]
        conv_ref[:, t, :] = acc
    sout_ref[:, 0:hist - t_new, :] = state_ref[:, t_new:hist, :]
    sout_ref[:, hist - t_new:hist, :] = u_ref[...]


def _sample_conv(state, u3, w_dw, b_dw, bb=32):
    db, hist, _ = state.shape
    t_new = u3.shape[1]
    assert db % bb == 0 and hist == CONV_WIDTH - 1
    blk = lambda i: (i, 0, 0)
    whole = lambda i: (0, 0)
    return pl.pallas_call(
        _sample_conv_kernel,
        grid=(db // bb,),
        in_specs=[pl.BlockSpec((bb, hist, CONV_CH), blk), pl.BlockSpec((bb, t_new, CONV_CH), blk),
                  pl.BlockSpec((CONV_WIDTH, CONV_CH), whole), pl.BlockSpec((1, CONV_CH), whole)],
        out_specs=[pl.BlockSpec((bb, t_new, CONV_CH), blk), pl.BlockSpec((bb, hist, CONV_CH), blk)],
        out_shape=[jax.ShapeDtypeStruct((db, t_new, CONV_CH), F32), jax.ShapeDtypeStruct((db, hist, CONV_CH), F32)],
        compiler_params=_cparams(("parallel",)),
        name="sample_conv",
    )(state, u3, w_dw, b_dw)


ROUTE_LANES = 2 * LANES


def _merge_kernel(xp_ref, xs_ref, cp_ref, cs_ref, ap_ref, as_ref, gp_ref, gs_ref, clg_ref, clb_ref, wco_ref, bco_ref,
                  wao_ref, wo_ref, g1_ref, b1_ref, wr_ref, br_ref, h_ref, ri_ref, rg_ref, cnt_ref, base_scr, tile_scr,
                  *, prompt_tiles):
    i = pl.program_id(0)
    tm = xp_ref.shape[0]
    is_p = i < prompt_tiles

    @pl.when(i == 0)
    def _():
        base_scr[...] = jnp.zeros_like(base_scr)

    x = jnp.where(is_p, xp_ref[...], xs_ref[...])
    conv = jnp.where(is_p, cp_ref[...], cs_ref[...])
    attn = jnp.where(is_p, ap_ref[...], as_ref[...])
    gate = jnp.where(is_p, gp_ref[...], gs_ref[...])

    cn = _layer_norm(conv, clg_ref[...], clb_ref[...])
    branch_c = jnp.dot(jax.nn.silu(cn).astype(BF16), wco_ref[...], preferred_element_type=F32) + bco_ref[...]
    branch_a = jnp.dot(attn.astype(BF16), wao_ref[...], preferred_element_type=F32)
    mixed = gate[:, 0:D_MODEL] * branch_a + gate[:, D_MODEL:2 * D_MODEL] * branch_c
    y = jnp.dot(mixed.astype(BF16), wo_ref[...], preferred_element_type=F32)
    h = _layer_norm(DN_ALPHA * x + y, g1_ref[...], b1_ref[...])
    _store_rows_as_tiles(h_ref, h, tile_scr)

    logits = jnp.dot(h.astype(BF16), wr_ref[...], preferred_element_type=F32) + br_ref[...]
    lane = lax.broadcasted_iota(I32, (tm, LANES), 1)
    lane_f = lane.astype(F32)
    big = jnp.float32(LANES)

    def first_lane(hit):
        return jnp.min(jnp.where(hit, lane_f, big), -1, keepdims=True).astype(I32)

    gl = jnp.where(lane < N_EXPERT_GROUPS, logits[:, 0:LANES], NEG_INF)
    gmax = jnp.max(gl, -1, keepdims=True)
    g_sel = first_lane(gl == gmax)
    p_g = 1.0 / jnp.sum(jnp.exp(gl - gmax), -1, keepdims=True)
    in_group = (lane < N_EXPERTS) & (lax.shift_right_logical(lane, 3) == g_sel)
    el = jnp.where(in_group, logits[:, LANES:2 * LANES], NEG_INF)
    v1 = jnp.max(el, -1, keepdims=True)
    i1 = first_lane(el == v1)
    el2 = jnp.where(lane == i1, NEG_INF, el)
    v2 = jnp.max(el2, -1, keepdims=True)
    i2 = first_lane(el2 == v2)
    e21 = jnp.exp(v2 - v1)
    gate1 = p_g / (1.0 + e21)
    gate2 = p_g * e21 / (1.0 + e21)

    oh = ((lane == i1) | (lane == i2)).astype(BF16)
    ri_ = lax.broadcasted_iota(I32, (tm, tm), 0)
    ci_ = lax.broadcasted_iota(I32, (tm, tm), 1)
    tri = (ci_ < ri_).astype(BF16)
    before = jnp.dot(tri, oh, preferred_element_type=F32) + base_scr[...]
    rank1 = jnp.sum(jnp.where(lane == i1, before, 0.0), -1, keepdims=True).astype(I32)
    rank2 = jnp.sum(jnp.where(lane == i2, before, 0.0), -1, keepdims=True).astype(I32)
    base_scr[...] = base_scr[...] + jnp.sum(oh.astype(F32), 0, keepdims=True)
    cnt_ref[...] = base_scr[...]

    ri_ref[...] = jnp.where(lane == 0, i1, jnp.where(lane == 1, i2, jnp.where(lane == 2, rank1,
                            jnp.where(lane == 3, rank2, 0))))
    rg_ref[...] = jnp.where(lane == 0, gate1, jnp.where(lane == 1, gate2, 0.0))


def _merge(acts_p, acts_s, weights, tm):
    n_p, n_s = acts_p[0].shape[0], acts_s[0].shape[0]
    assert n_p % tm == 0 and n_s % tm == 0
    pt = n_p // tm
    n_all = n_p + n_s
    prow = lambda i: (jnp.minimum(i, pt - 1), 0)
    srow = lambda i: (jnp.maximum(i - pt, 0), 0)
    row = lambda i: (i, 0)
    whole = lambda i: (0, 0)
    act_args, act_specs = [], []
    for a_p, a_s in zip(acts_p, acts_s):
        act_args += [a_p, a_s]
        act_specs += [pl.BlockSpec((tm, a_p.shape[1]), prow), pl.BlockSpec((tm, a_s.shape[1]), srow)]
    return pl.pallas_call(
        functools.partial(_merge_kernel, prompt_tiles=pt),
        grid=(n_all // tm,),
        in_specs=act_specs + [pl.BlockSpec(w.shape, whole) for w in weights],
        out_specs=[pl.BlockSpec((tm, ROW_SUB, LANES), lambda i: (i, 0, 0)), pl.BlockSpec((tm, LANES), row),
                   pl.BlockSpec((tm, LANES), row), pl.BlockSpec((1, LANES), whole)],
        out_shape=[jax.ShapeDtypeStruct((n_all, ROW_SUB, LANES), F32), jax.ShapeDtypeStruct((n_all, LANES), I32),
                   jax.ShapeDtypeStruct((n_all, LANES), F32), jax.ShapeDtypeStruct((1, LANES), F32)],
        scratch_shapes=[pltpu.VMEM((1, LANES), F32), pltpu.VMEM((tm // 8, 8 * ROW_SUB, LANES), F32)],
        compiler_params=_cparams(("arbitrary",)),
        name="merge",
    )(*act_args, *weights)


def _dispatch_kernel(bounds_ref, nv_ref, dest_ref, h_ref, xg_any, hbuf, zero_scr, sem_z, sem_r):
    i = pl.program_id(0)
    tm = h_ref.shape[0]
    n_chunks = xg_any.shape[0] // CHUNK
    slot = i % 2

    def zero_chunk(row0):
        return pltpu.make_async_copy(zero_scr, xg_any.at[pl.ds(pl.multiple_of(row0, CHUNK), CHUNK)], sem_z)

    @pl.when(i == 0)
    def _():
        zero_scr[...] = jnp.zeros_like(zero_scr)
        for e in range(N_EXPERTS):
            @pl.when(bounds_ref[e + 1] > bounds_ref[e])
            def _():
                zero_chunk(bounds_ref[e + 1] - CHUNK).start()

        def start_tail(c, carry):
            zero_chunk(c * CHUNK).start()
            return carry

        lax.fori_loop(nv_ref[0], n_chunks, start_tail, 0)
        for e in range(N_EXPERTS):
            @pl.when(bounds_ref[e + 1] > bounds_ref[e])
            def _():
                zero_chunk(0).wait()

        def wait_tail(c, carry):
            zero_chunk(0).wait()
            return carry

        lax.fori_loop(nv_ref[0], n_chunks, wait_tail, 0)

    def wait_rows(sl):
        for _ in range(2 * tm):
            pltpu.make_async_copy(hbuf.at[0, 0], xg_any.at[0], sem_r.at[sl]).wait()

    @pl.when(i >= 2)
    def _():
        wait_rows(slot)

    hbuf[slot] = h_ref[...]
    for r in range(2 * tm):
        pltpu.make_async_copy(hbuf.at[slot, r % tm], xg_any.at[dest_ref[0, 0, r]],
                              sem_r.at[slot]).start(priority=r % 2)

    @pl.when(i == pl.num_programs(0) - 1)
    def _():
        wait_rows(slot)

        @pl.when(i >= 1)
        def _():
            wait_rows(1 - slot)


def _dispatch(bounds, n_valid, dest3, h3, n_rows, tm):
    n = h3.shape[0]
    return pl.pallas_call(
        _dispatch_kernel,
        grid_spec=pltpu.PrefetchScalarGridSpec(
            num_scalar_prefetch=2,
            grid=(n // tm,),
            in_specs=[pl.BlockSpec((1, 1, 2 * tm), lambda i, b, nv: (i, 0, 0), memory_space=pltpu.SMEM),
                      pl.BlockSpec((tm, ROW_SUB, LANES), lambda i, b, nv: (i, 0, 0))],
            out_specs=pl.BlockSpec(memory_space=pl.ANY),
            scratch_shapes=[pltpu.VMEM((2, tm, ROW_SUB, LANES), F32), pltpu.VMEM((CHUNK, ROW_SUB, LANES), F32),
                            pltpu.SemaphoreType.DMA(()), pltpu.SemaphoreType.DMA((2,))],
        ),
        out_shape=jax.ShapeDtypeStruct((n_rows, ROW_SUB, LANES), F32),
        compiler_params=_cparams(("arbitrary",)),
        name="moe_dispatch",
    )(bounds, n_valid, dest3, h3)


def _experts_kernel(ce_ref, nv_ref, nxt_ref, par_ref, x_ref, wg_any, wu_any, wd_any, y_ref,
                    wg_buf, wu_buf, wd_buf, wg_scr, wu_scr, wd_scr, tile_scr, wsem):
    c = pl.program_id(0)
    nv = nv_ref[0]
    e = ce_ref[c]
    slot = par_ref[c]

    def weight_copies(expert, sl):
        return [pltpu.make_async_copy(src.at[expert], dst.at[sl], wsem.at[sl, k])
                for k, (src, dst) in enumerate(((wg_any, wg_buf), (wu_any, wu_buf), (wd_any, wd_buf)))]

    @pl.when((c == 0) & (nv > 0))
    def _():
        for cp in weight_copies(e, slot):
            cp.start()

    @pl.when((c < nv) & ((c == 0) | (e != ce_ref[jnp.maximum(c - 1, 0)])))
    def _():
        for cp in weight_copies(e, slot):
            cp.wait()
        wg_scr[...] = wg_buf[slot].astype(BF16)
        wu_scr[...] = wu_buf[slot].astype(BF16)
        wd_scr[...] = wd_buf[slot].astype(BF16)

        @pl.when(nxt_ref[c] != e)
        def _():
            for cp in weight_copies(nxt_ref[c], 1 - slot):
                cp.start()

    @pl.when(c < nv)
    def _():
        x = _load_tiles_as_rows(x_ref, tile_scr).astype(BF16)
        a = jnp.dot(x, wg_scr[...], preferred_element_type=F32)
        b = jnp.dot(x, wu_scr[...], preferred_element_type=F32)
        y = jnp.dot((jax.nn.silu(a) * b).astype(BF16), wd_scr[...], preferred_element_type=F32)
        _store_rows_as_tiles(y_ref, y, tile_scr)

    @pl.when(c >= nv)
    def _():
        y_ref[...] = jnp.zeros_like(y_ref)


def _experts(chunk_expert, n_valid, next_expert, parity, xg, w_eg, w_eu, w_ed):
    n_chunks = xg.shape[0] // CHUNK
    any_spec = pl.BlockSpec(memory_space=pl.ANY)
    return pl.pallas_call(
        _experts_kernel,
        grid_spec=pltpu.PrefetchScalarGridSpec(
            num_scalar_prefetch=4,
            grid=(n_chunks,),
            in_specs=[pl.BlockSpec((CHUNK, ROW_SUB, LANES),
                                   lambda c, ce, nv, nx, pr: (jnp.minimum(c, jnp.maximum(nv[0] - 1, 0)), 0, 0)),
                      any_spec, any_spec, any_spec],
            out_specs=pl.BlockSpec((CHUNK, ROW_SUB, LANES), lambda c, ce, nv, nx, pr: (c, 0, 0)),
            scratch_shapes=[pltpu.VMEM((2, D_MODEL, D_EXPERT), F32), pltpu.VMEM((2, D_MODEL, D_EXPERT), F32),
                            pltpu.VMEM((2, D_EXPERT, D_MODEL), F32),
                            pltpu.VMEM((D_MODEL, D_EXPERT), BF16), pltpu.VMEM((D_MODEL, D_EXPERT), BF16),
                            pltpu.VMEM((D_EXPERT, D_MODEL), BF16),
                            pltpu.VMEM((CHUNK // 8, 8 * ROW_SUB, LANES), F32),
                            pltpu.SemaphoreType.DMA((2, 3))],
        ),
        out_shape=jax.ShapeDtypeStruct(xg.shape, F32),
        compiler_params=_cparams(("arbitrary",)),
        name="moe_experts",
    )(chunk_expert, n_valid, next_expert, parity, xg, w_eg, w_eu, w_ed)


def _combine_kernel(dest_ref, dest_next_ref, h_ref, rg_ref, g2_ref, b2_ref, yg_any, yp_ref, ys_ref,
                    ybuf, t0_scr, t1_scr, t2_scr, sem, *, prompt_tiles):
    i = pl.program_id(0)
    tm = h_ref.shape[0]
    slot = i % 2

    def start_gather(src_ref, sl):
        for r in range(2 * tm):
            pltpu.make_async_copy(yg_any.at[src_ref[0, 0, r]], ybuf.at[sl, r], sem.at[sl]).start(priority=r % 2)

    @pl.when(i == 0)
    def _():
        start_gather(dest_ref, 0)

    @pl.when(i + 1 < pl.num_programs(0))
    def _():
        start_gather(dest_next_ref, 1 - slot)

    for _ in range(2 * tm):
        pltpu.make_async_copy(yg_any.at[0], ybuf.at[0, 0], sem.at[slot]).wait()
    moe = (_load_tiles_as_rows(ybuf.at[slot, pl.ds(0, tm)], t1_scr) * rg_ref[:, 0:1]
           + _load_tiles_as_rows(ybuf.at[slot, pl.ds(tm, tm)], t2_scr) * rg_ref[:, 1:2])
    out = _layer_norm(DN_ALPHA * _load_tiles_as_rows(h_ref, t0_scr) + moe, g2_ref[...], b2_ref[...])

    @pl.when(i < prompt_tiles)
    def _():
        yp_ref[...] = out

    @pl.when(i >= prompt_tiles)
    def _():
        ys_ref[...] = out


def _combine(dest3, h3, yg, route_g, ln2_g, ln2_b, n_prompt, tm):
    n = h3.shape[0]
    pt, nt = n_prompt // tm, n // tm
    row = lambda i: (i, 0)
    whole = lambda i: (0, 0)
    return pl.pallas_call(
        functools.partial(_combine_kernel, prompt_tiles=pt),
        grid=(nt,),
        in_specs=[pl.BlockSpec((1, 1, 2 * tm), lambda i: (i, 0, 0), memory_space=pltpu.SMEM),
                  pl.BlockSpec((1, 1, 2 * tm), lambda i: (jnp.minimum(i + 1, nt - 1), 0, 0), memory_space=pltpu.SMEM),
                  pl.BlockSpec((tm, ROW_SUB, LANES), lambda i: (i, 0, 0)), pl.BlockSpec((tm, LANES), row),
                  pl.BlockSpec((1, D_MODEL), whole), pl.BlockSpec((1, D_MODEL), whole),
                  pl.BlockSpec(memory_space=pl.ANY)],
        out_specs=[pl.BlockSpec((tm, D_MODEL), lambda i: (jnp.minimum(i, pt - 1), 0)),
                   pl.BlockSpec((tm, D_MODEL), lambda i: (jnp.maximum(i - pt, 0), 0))],
        out_shape=[jax.ShapeDtypeStruct((n_prompt, D_MODEL), F32), jax.ShapeDtypeStruct((n - n_prompt, D_MODEL), F32)],
        scratch_shapes=[pltpu.VMEM((2, 2 * tm, ROW_SUB, LANES), F32)]
        + [pltpu.VMEM((tm // 8, 8 * ROW_SUB, LANES), F32) for _ in range(3)] + [pltpu.SemaphoreType.DMA((2,))],
        compiler_params=_cparams(("arbitrary",)),
        name="moe_combine",
    )(dest3, dest3, h3, route_g, ln2_g, ln2_b, yg)


def kernel(x_prompt, x_sample, cache_kv_w128, cache_kv_w512, cache_kv_w2048, state_conv, w_in, b_in, w_dw, b_dw, conv_ln_g, conv_ln_b, w_conv_out, b_conv_out, w_attn_out, w_o, ln1_g, ln1_b, w_router_group, b_router_group, w_router_expert, b_router_expert, w_expert_gate, w_expert_up, w_expert_down, ln2_g, ln2_b):
    assert w_in.shape[0] == DEPTH == 1
    bsz, seq, _ = x_prompt.shape
    db, t_new, _ = x_sample.shape
    caches = (cache_kv_w128, cache_kv_w512, cache_kv_w2048)
    past = cache_kv_w2048.shape[2]
    n_p, n_s = bsz * seq, db * t_new
    n_all = n_p + n_s
    tm = ROW_TILE
    kvs = (HEADS_PER_GROUP, HEAD_DIM)

    w_bf = w_in[0].astype(BF16)
    xp2, xs2 = x_prompt.reshape(n_p, D_MODEL), x_sample.reshape(n_s, D_MODEL)
    q0, q1, q2, kvf_p, u_p, gate_p = _in_proj(xp2, w_bf, b_in, _rotary_tables(np.arange(seq)), tm, True)
    pos_s = past + np.arange(tm) % t_new
    q_s, kvf_s, u_s, gate_s = _in_proj(xs2, w_bf, b_in, _rotary_tables(pos_s), tm, False)

    attn_p = _prompt_attn((q0, q1, q2), bsz, seq)

    kv_new = kvf_s.reshape(db, t_new, N_GROUPS, KV_ROWS)
    caches_t = [jnp.transpose(c[0], (0, 2, 3, 4, 1)).reshape(db, KV_ROWS, c.shape[2]) for c in caches]
    news = [jnp.pad(kv_new[:, :, g], ((0, 0), (0, -t_new % 8), (0, 0))) for g in range(N_GROUPS)]
    attn_s, kv_sample_t, conv_p = _sample_attn(q_s.reshape(db, t_new, ATTN_WIDTH), caches_t, news,
                                               u_p, seq, w_dw[0], b_dw)
    kv_sample = [jnp.transpose(o.reshape(db, 2, *kvs, o.shape[2]), (0, 4, 1, 2, 3))[None] for o in kv_sample_t]

    conv_s, conv_state_s = _sample_conv(state_conv[0], u_s.reshape(db, t_new, CONV_CH), w_dw[0], b_dw)

    w_r = jnp.zeros((D_MODEL, ROUTE_LANES), F32)
    w_r = w_r.at[:, 0:N_EXPERT_GROUPS].set(w_router_group[0]).at[:, LANES:LANES + N_EXPERTS].set(w_router_expert[0])
    b_r = jnp.zeros((1, ROUTE_LANES), F32)
    b_r = b_r.at[:, 0:N_EXPERT_GROUPS].set(b_router_group).at[:, LANES:LANES + N_EXPERTS].set(b_router_expert)
    weights = [conv_ln_g, conv_ln_b, w_conv_out[0].astype(BF16), b_conv_out, w_attn_out[0].astype(BF16),
               w_o[0].astype(BF16), ln1_g, ln1_b, w_r.astype(BF16), b_r]
    acts_p = (xp2, conv_p, attn_p, gate_p)
    acts_s = (xs2, conv_s.reshape(n_s, CONV_CH), attn_s.reshape(n_s, GROUP_W), gate_s)
    h3, route_i, route_g, counts = _merge(acts_p, acts_s, weights, tm)

    cnt = counts[0, :N_EXPERTS].astype(I32)
    padded = ((cnt + CHUNK - 1) // CHUNK) * CHUNK
    pad_end = jnp.cumsum(padded)
    bounds = jnp.concatenate([jnp.zeros((1,), I32), pad_end]).astype(I32)
    n_chunks = (2 * n_all) // CHUNK + N_EXPERTS
    chunk_row0 = jnp.arange(n_chunks, dtype=I32) * CHUNK
    chunk_expert = jnp.minimum(jnp.sum((pad_end[None, :] <= chunk_row0[:, None]).astype(I32), axis=1),
                               N_EXPERTS - 1).astype(I32)
    n_valid = (pad_end[-1:] // CHUNK).astype(I32)

    e_onehot = route_i[:, 0:2, None] == jnp.arange(N_EXPERTS, dtype=I32)[None, None, :]
    dest = jnp.sum(jnp.where(e_onehot, bounds[None, None, :N_EXPERTS], 0), axis=-1) + route_i[:, 2:4]
    dest3 = jnp.transpose(dest.reshape(n_all // tm, tm, 2), (0, 2, 1)).reshape(n_all // tm, 1, 2 * tm)
    xg = _dispatch(bounds, n_valid, dest3, h3, n_chunks * CHUNK, tm)
    eidx = jnp.arange(N_EXPERTS, dtype=I32)
    later = (eidx[None, :] > eidx[:, None]) & (padded[None, :] > 0)
    next_nonempty = jnp.min(jnp.where(later, eidx[None, :], N_EXPERTS), axis=1)
    next_nonempty = jnp.where(next_nonempty < N_EXPERTS, next_nonempty, eidx).astype(I32)
    ce_onehot = chunk_expert[:, None] == eidx[None, :]
    next_expert = jnp.sum(jnp.where(ce_onehot, next_nonempty[None, :], 0), axis=1).astype(I32)
    changed = jnp.concatenate([jnp.zeros((1,), I32), (chunk_expert[1:] != chunk_expert[:-1]).astype(I32)])
    parity = (jnp.cumsum(changed) % 2).astype(I32)
    yg = _experts(chunk_expert, n_valid, next_expert, parity, xg,
                  w_expert_gate[0], w_expert_up[0], w_expert_down[0])
    y_p, y_s = _combine(dest3, h3, yg, route_g, ln2_g, ln2_b, n_p, tm)

    kvf3 = kvf_p.reshape(bsz, seq, N_GROUPS * KV_ROWS)
    kv_prompt = []
    for g, (w, _) in enumerate(ATTN_PATTERNS):
        win_t = _kv_window(kvf3, g, w)
        kv_prompt.append(jnp.transpose(win_t.reshape(bsz, 2, *kvs, win_t.shape[2]), (0, 4, 1, 2, 3))[None])
    conv_prompt = u_p.reshape(1, bsz, seq, CONV_CH)[:, :, seq - (CONV_WIDTH - 1):]
    return (y_p.reshape(bsz, seq, D_MODEL), y_s.reshape(db, t_new, D_MODEL),
            kv_prompt[0], kv_prompt[1], kv_prompt[2], conv_prompt,
            kv_sample[0], kv_sample[1], kv_sample[2], conv_state_s[None])
```

```python
import functools

import jax
import jax.numpy as jnp
import numpy as np
from jax import lax
from jax.experimental import pallas as pl
from jax.experimental.pallas import tpu as pltpu

F32 = jnp.float32
BF16 = jnp.bfloat16
I32 = jnp.int32

D_MODEL = 1024
HEAD_DIM = 64
HEADS_PER_GROUP = 4
GROUP_W = HEADS_PER_GROUP * HEAD_DIM
ATTN_PATTERNS = ((128, 1), (512, 4), (2048, 16))
N_GROUPS = len(ATTN_PATTERNS)
ATTN_WIDTH = N_GROUPS * GROUP_W
QKV_W = 3 * GROUP_W
STEPS = 128
ROT_HALF = 8
ROPE_THETA = 500000.0
CONV_CH = 512
CONV_WIDTH = 31
HALO = 32
N_EXPERTS = 32
N_EXPERT_GROUPS = 4
D_EXPERT = 512
DEPTH = 1
DN_ALPHA = (2 * DEPTH) ** 0.25
LN_EPS = 1e-5
SCALE = HEAD_DIM ** -0.5
NEG_INF = -1e30
LANES = 128
ROW_SUB = D_MODEL // LANES
KV_ROWS = 2 * GROUP_W
ROW_TILE = 256
CHUNK = 256
VMEM_LIMIT = 56 * 1024 * 1024

_Q0, _K0, _V0 = 0, ATTN_WIDTH, 2 * ATTN_WIDTH
_UA0 = 3 * ATTN_WIDTH
_UB0 = _UA0 + CONV_CH
_G0 = _UB0 + CONV_CH
IN_WIDTH = _G0 + 2 * D_MODEL


def _cparams(sem):
    return pltpu.CompilerParams(dimension_semantics=sem, vmem_limit_bytes=VMEM_LIMIT)


def _layer_norm(x, g, b):
    mu = jnp.mean(x, -1, keepdims=True)
    xc = x - mu
    var = jnp.mean(xc * xc, -1, keepdims=True)
    return xc * lax.rsqrt(var + LN_EPS) * g + b


def _head_masks(shape, lane_axis):
    lane = lax.broadcasted_iota(I32, shape, lane_axis)
    return [(lane >= h * HEAD_DIM) & (lane < (h + 1) * HEAD_DIM) for h in range(HEADS_PER_GROUP)]


def _store_rows_as_tiles(ref3, val2d, tmp):
    sub = 8
    groups = val2d.shape[0] // sub
    for c in range(ROW_SUB):
        tmp[:, c * sub:(c + 1) * sub, :] = val2d[:, c * LANES:(c + 1) * LANES].reshape(groups, sub, LANES)
    for s in range(sub):
        ref3[pl.ds(s, groups, stride=sub), :, :] = tmp[:, pl.ds(s, ROW_SUB, stride=sub), :]


def _load_tiles_as_rows(ref3, tmp):
    sub = 8
    groups = ref3.shape[0] // sub
    for s in range(sub):
        tmp[:, pl.ds(s, ROW_SUB, stride=sub), :] = ref3[pl.ds(s, groups, stride=sub), :, :]
    return jnp.concatenate([tmp[:, c * sub:(c + 1) * sub, :].reshape(groups * sub, LANES) for c in range(ROW_SUB)],
                           axis=1)


def _rotary_tables(pos):
    pos = np.asarray(pos, np.float64)
    inv_freq = ROPE_THETA ** (-np.arange(ROT_HALF, dtype=np.float64) / ROT_HALF)
    ang = pos[:, None] * inv_freq[None, :]
    cos, sin = np.cos(ang), np.sin(ang)
    zero = np.zeros_like(cos)
    p = pos.shape[0]
    rest = HEAD_DIM - 2 * ROT_HALF
    c_head = np.concatenate([cos, cos, np.ones((p, rest))], -1)
    sa_head = np.concatenate([zero, sin, np.zeros((p, rest))], -1)
    sb_head = np.concatenate([-sin, zero, np.zeros((p, rest))], -1)
    rep = LANES // HEAD_DIM
    return tuple(jnp.asarray(np.tile(t, (1, rep)), F32) for t in (c_head, sa_head, sb_head))


def _causal_conv_rows(ext_scr, sh_scr, wdw_ref, bdw_ref, conv_ref):
    n = conv_ref.shape[0]
    off = HALO - (CONV_WIDTH - 1)
    sub = 8
    for s in range(sub):
        rows = n + sub * ((CONV_WIDTH - s + sub - 1) // sub - 1)
        sh_scr[s, 0:rows, :] = ext_scr[off + s:off + s + rows, :]
    rb = 64
    for cc in range(CONV_CH // LANES):
        cs = slice(cc * LANES, (cc + 1) * LANES)
        for r0 in range(0, n, rb):
            acc = jnp.broadcast_to(bdw_ref[:, cs], (rb, LANES))
            for j in range(CONV_WIDTH):
                a0 = sub * (j // sub) + r0
                acc = acc + sh_scr[j % sub, a0:a0 + rb, cs] * wdw_ref[j:j + 1, cs]
            conv_ref[r0:r0 + rb, cs] = acc


def _in_proj_kernel(*refs, prompt):
    if prompt:
        (x_ref, w_ref, b_ref, c_ref, sa_ref, sb_ref,
         q0_ref, q1_ref, q2_ref, kt0_ref, kt1_ref, kt2_ref, u_ref, gate_ref, rm_scr) = refs
    else:
        (x_ref, w_ref, b_ref, c_ref, sa_ref, sb_ref, qs_ref, kvf_ref, u_ref, gate_ref) = refs
    tm = x_ref.shape[0]
    xb = x_ref[...].astype(BF16)

    def mm(c0, c1):
        return jnp.dot(xb, w_ref[:, c0:c1], preferred_element_type=F32) + b_ref[:, c0:c1]

    u_ref[...] = mm(_UA0, _UB0) * jax.nn.sigmoid(mm(_UB0, _G0))
    c, sa, sb = c_ref[...], sa_ref[...], sb_ref[...]

    def rotary(z):
        parts = []
        for j in range(z.shape[1] // LANES):
            zc = z[:, j * LANES:(j + 1) * LANES]
            parts.append(zc * c + pltpu.roll(zc, ROT_HALF, 1) * sa + pltpu.roll(zc, LANES - ROT_HALF, 1) * sb)
        return jnp.concatenate(parts, axis=1)

    q = rotary(mm(_Q0, _K0)) * SCALE
    k = rotary(mm(_K0, _V0))
    v = mm(_V0, _UA0)
    for g in range(N_GROUPS):
        gs = slice(g * GROUP_W, (g + 1) * GROUP_W)
        if prompt:
            (kt0_ref, kt1_ref, kt2_ref)[g][0] = jnp.concatenate([k[:, gs], v[:, gs]], axis=1).T
        else:
            kvf_ref[:, 2 * g * GROUP_W:(2 * g + 1) * GROUP_W] = k[:, gs]
            kvf_ref[:, (2 * g + 1) * GROUP_W:(2 * g + 2) * GROUP_W] = v[:, gs]
    gate_ref[...] = jax.nn.sigmoid(mm(_G0, IN_WIDTH))

    if not prompt:
        qs_ref[...] = q
        return

    for g, out_ref in enumerate((q0_ref, q1_ref, q2_ref)):
        dil = ATTN_PATTERNS[g][1]
        gs = slice(g * GROUP_W, (g + 1) * GROUP_W)
        qkv = jnp.concatenate([q[:, gs], k[:, gs], v[:, gs]], axis=1)
        if dil == 1:
            out_ref[...] = qkv.astype(BF16)
        else:
            nck = QKV_W // LANES
            for ck in range(nck):
                rm_scr[ck] = qkv[:, ck * LANES:(ck + 1) * LANES]
            for r in range(dil):
                blk = jnp.concatenate([rm_scr[ck, pl.ds(r, tm // dil, stride=dil), :] for ck in range(nck)], axis=1)
                out_ref[:, r * QKV_W:(r + 1) * QKV_W] = blk.astype(BF16)


def _in_proj(x2d, w_bf, b2d, tables, tm, prompt_seq=None):
    n = x2d.shape[0]
    p = tables[0].shape[0]
    prompt = prompt_seq is not None
    assert n % tm == 0 and p % tm == 0
    tpb = p // tm
    row = lambda i: (i, 0)
    tab = lambda i: (i % tpb, 0)
    whole = lambda i: (0, 0)
    in_specs = [pl.BlockSpec((tm, D_MODEL), row), pl.BlockSpec((D_MODEL, IN_WIDTH), whole),
                pl.BlockSpec((1, IN_WIDTH), whole)] + [pl.BlockSpec((tm, LANES), tab)] * 3
    u_spec, u_shape = pl.BlockSpec((tm, CONV_CH), row), jax.ShapeDtypeStruct((n, CONV_CH), F32)
    gate_spec, gate_shape = pl.BlockSpec((tm, 2 * D_MODEL), row), jax.ShapeDtypeStruct((n, 2 * D_MODEL), F32)
    if prompt:
        assert prompt_seq % tm == 0 and n % prompt_seq == 0
        tps = prompt_seq // tm
        q_specs = [pl.BlockSpec((tm // d, d * QKV_W), row) for _, d in ATTN_PATTERNS]
        q_shapes = [jax.ShapeDtypeStruct((n // d, d * QKV_W), BF16) for _, d in ATTN_PATTERNS]
        kt_specs = [pl.BlockSpec((1, KV_ROWS, tm), lambda i: (i // tps, 0, i % tps))] * N_GROUPS
        kt_shapes = [jax.ShapeDtypeStruct((n // prompt_seq, KV_ROWS, prompt_seq), F32)] * N_GROUPS
        out_specs = q_specs + kt_specs + [u_spec, gate_spec]
        out_shape = q_shapes + kt_shapes + [u_shape, gate_shape]
        scratch = [pltpu.VMEM((QKV_W // LANES, tm, LANES), F32)]
    else:
        kv_spec = pl.BlockSpec((tm, 2 * ATTN_WIDTH), row)
        out_specs = [pl.BlockSpec((tm, ATTN_WIDTH), row), kv_spec, u_spec, gate_spec]
        out_shape = [jax.ShapeDtypeStruct((n, ATTN_WIDTH), F32), jax.ShapeDtypeStruct((n, 2 * ATTN_WIDTH), F32),
                     u_shape, gate_shape]
        scratch = []
    return pl.pallas_call(
        functools.partial(_in_proj_kernel, prompt=prompt),
        grid=(n // tm,),
        in_specs=in_specs,
        out_specs=out_specs,
        out_shape=out_shape,
        scratch_shapes=scratch,
        compiler_params=_cparams(("parallel",)),
        name="in_proj_prompt" if prompt else "in_proj_sample",
    )(x2d, w_bf, b2d, *tables)


def _attn_block(q, keys, vals, first):
    masks = _head_masks(q.shape, 1)
    zero = jnp.zeros_like(q)
    q_stack = jnp.concatenate([jnp.where(m, q, zero) for m in masks], axis=0)
    s = lax.dot_general(q_stack, keys, (((1,), (1,)), ((), ())), preferred_element_type=F32)
    qi = lax.broadcasted_iota(I32, s.shape, 0) % STEPS
    kj = lax.broadcasted_iota(I32, s.shape, 1)
    ok = (kj <= qi) if first else ((kj >= qi) & (kj <= qi + STEPS))
    s = jnp.where(ok, s, NEG_INF)
    m = jnp.max(s, -1, keepdims=True)
    p = jnp.exp(s - m)
    l = jnp.sum(p, -1, keepdims=True)
    pv = jnp.dot(p.astype(BF16), vals, preferred_element_type=F32)
    lse = m + jnp.log(l)
    o = jnp.zeros((STEPS, GROUP_W), F32)
    lse_o = jnp.zeros((STEPS, GROUP_W), F32)
    for h, hm in enumerate(masks):
        rs = slice(h * STEPS, (h + 1) * STEPS)
        o = jnp.where(hm, pv[rs] / l[rs], o)
        lse_o = jnp.where(hm, lse[rs], lse_o)
    return o, lse_o


def _prompt_attn_kernel(q0_ref, q1_ref, q2_ref, attn_ref, o_rm1, l_rm1, o_rm2, l_rm2, o_nat, l_nat):
    halves = GROUP_W // LANES

    def put_nat(g, rows, o, lse):
        for hf in range(halves):
            hs = slice(hf * LANES, (hf + 1) * LANES)
            o_nat[g * halves + hf, rows, :] = o[:, hs]
            l_nat[g * halves + hf, rows, :] = lse[:, hs]

    for g, ref in enumerate((q0_ref, q1_ref, q2_ref)):
        dil = ATTN_PATTERNS[g][1]
        o_rm, l_rm = (None, o_rm1, o_rm2)[g], (None, l_rm1, l_rm2)[g]
        rows = ref.shape[0]
        nb = rows // STEPS
        for r in range(dil):
            base = r * QKV_W
            qc, kc, vc = (slice(base + j * GROUP_W, base + (j + 1) * GROUP_W) for j in range(3))
            oc = slice(r * GROUP_W, (r + 1) * GROUP_W)
            o, lse = _attn_block(ref[0:STEPS, qc], ref[0:STEPS, kc], ref[0:STEPS, vc], True)
            if dil == 1:
                put_nat(g, slice(0, STEPS), o, lse)
            else:
                o_rm[0:STEPS, oc] = o
                l_rm[0:STEPS, oc] = lse

            def body(n, carry, ref=ref, qc=qc, kc=kc, vc=vc, oc=oc, dil=dil, g=g, o_rm=o_rm, l_rm=l_rm):
                q0 = pl.multiple_of(n * STEPS, STEPS)
                k0 = pl.multiple_of((n - 1) * STEPS, STEPS)
                o, lse = _attn_block(ref[pl.ds(q0, STEPS), qc], ref[pl.ds(k0, 2 * STEPS), kc],
                                     ref[pl.ds(k0, 2 * STEPS), vc], False)
                if dil == 1:
                    put_nat(g, pl.ds(q0, STEPS), o, lse)
                else:
                    o_rm[pl.ds(q0, STEPS), oc] = o
                    l_rm[pl.ds(q0, STEPS), oc] = lse
                return carry

            if nb > 1:
                lax.fori_loop(1, nb, body, 0, unroll=3)
        if dil > 1:
            for r in range(dil):
                oc = slice(r * GROUP_W, (r + 1) * GROUP_W)
                put_nat(g, pl.ds(r, rows, stride=dil), o_rm[0:rows, oc], l_rm[0:rows, oc])

    def nat(ref, g):
        return jnp.concatenate([ref[g * halves + hf] for hf in range(halves)], axis=1)

    lses = [nat(l_nat, g) for g in range(N_GROUPS)]
    mx = jnp.maximum(jnp.maximum(lses[0], lses[1]), lses[2])
    ws = [jnp.exp(ls - mx) for ls in lses]
    attn_ref[...] = (nat(o_nat, 0) * ws[0] + nat(o_nat, 1) * ws[1] + nat(o_nat, 2) * ws[2]) / (ws[0] + ws[1] + ws[2])


def _prompt_attn(qkv_groups, bsz, seq):
    row = lambda b: (b, 0)
    in_specs = [pl.BlockSpec((seq // d, d * QKV_W), row) for _, d in ATTN_PATTERNS]
    return pl.pallas_call(
        _prompt_attn_kernel,
        grid=(bsz,),
        in_specs=in_specs,
        out_specs=pl.BlockSpec((seq, GROUP_W), row),
        out_shape=jax.ShapeDtypeStruct((bsz * seq, GROUP_W), F32),
        scratch_shapes=[pltpu.VMEM((seq // d, d * GROUP_W), F32) for _, d in ATTN_PATTERNS[1:] for _ in range(2)]
        + [pltpu.VMEM((N_GROUPS * (GROUP_W // LANES), seq, LANES), F32) for _ in range(2)],
        compiler_params=_cparams(("parallel",)),
        name="prompt_attn",
    )(*qkv_groups)


def _sample_attn_kernel(q_ref, c0_ref, c1_ref, c2_ref, n0_ref, n1_ref, n2_ref, u_ref, halo_ref, wdw_ref, bdw_ref,
                        attn_ref, o0_ref, o1_ref, o2_ref, conv_ref, ext_scr, sh_scr, *, t_new, conv_tiles_per_seq):
    n_conv = u_ref.shape[0]
    first = pl.program_id(0) % conv_tiles_per_seq == 0
    ext_scr[0:HALO, :] = jnp.where(first, 0.0, halo_ref[...])
    ext_scr[HALO:HALO + n_conv, :] = u_ref[...]
    _causal_conv_rows(ext_scr, sh_scr, wdw_ref, bdw_ref, conv_ref)

    q = q_ref[0]
    outs, lses = [], []
    for g, (cref, nref, oref) in enumerate(((c0_ref, n0_ref, o0_ref), (c1_ref, n1_ref, o1_ref),
                                            (c2_ref, n2_ref, o2_ref))):
        dil = ATTN_PATTERNS[g][1]
        wb = cref.shape[2]
        new_t = jnp.concatenate([nref[0], jnp.zeros((LANES - nref.shape[1], KV_ROWS), F32)], axis=0).T
        full = jnp.concatenate([cref[0], new_t], axis=1)
        oref[0] = full[:, t_new:wb + t_new]
        k_t = full[0:GROUP_W].astype(BF16)
        v_t = full[GROUP_W:KV_ROWS].astype(BF16)
        qg = q[:, g * GROUP_W:(g + 1) * GROUP_W]
        masks = _head_masks(qg.shape, 1)
        q_stack = jnp.concatenate([jnp.where(m, qg, 0.0) for m in masks], axis=0).astype(BF16)
        s = jnp.dot(q_stack, k_t, preferred_element_type=F32)
        r = lax.broadcasted_iota(I32, s.shape, 1)
        t = lax.broadcasted_iota(I32, s.shape, 0) % t_new
        ok = (r >= t) & (r <= wb + t) & (((r - t) & (dil - 1)) == 0)
        s = jnp.where(ok, s, NEG_INF)
        m = jnp.max(s, -1, keepdims=True)
        p = jnp.exp(s - m)
        l = jnp.sum(p, -1, keepdims=True)
        pv = lax.dot_general(p.astype(BF16), v_t, (((1,), (1,)), ((), ())), preferred_element_type=F32)
        lse = m + jnp.log(l)
        o = jnp.zeros((t_new, GROUP_W), F32)
        lse_o = jnp.zeros((t_new, GROUP_W), F32)
        for h, hm in enumerate(masks):
            rs = slice(h * t_new, (h + 1) * t_new)
            o = jnp.where(hm, pv[rs] / l[rs], o)
            lse_o = jnp.where(hm, lse[rs], lse_o)
        outs.append(o)
        lses.append(lse_o)
    mx = jnp.maximum(jnp.maximum(lses[0], lses[1]), lses[2])
    ws = [jnp.exp(ls - mx) for ls in lses]
    attn_ref[0] = (outs[0] * ws[0] + outs[1] * ws[1] + outs[2] * ws[2]) / (ws[0] + ws[1] + ws[2])


def _sample_attn(q3, caches_t, news, u_prompt, prompt_seq, w_dw, b_dw):
    db, t_new, _ = q3.shape
    for c, (_, dil) in zip(caches_t, ATTN_PATTERNS):
        assert c.shape[2] == STEPS * dil and t_new <= LANES and dil & (dil - 1) == 0
    n_conv = u_prompt.shape[0] // db
    assert u_prompt.shape[0] == n_conv * db and n_conv % 64 == 0 and n_conv % HALO == 0 and prompt_seq % n_conv == 0
    blk = lambda b: (b, 0, 0)
    row = lambda b: (b, 0)
    whole = lambda b: (0, 0)
    cache_specs = [pl.BlockSpec((1, KV_ROWS, c.shape[2]), blk) for c in caches_t]
    halo_spec = pl.BlockSpec((HALO, CONV_CH), lambda b: (jnp.maximum(b * (n_conv // HALO) - 1, 0), 0))
    outs = pl.pallas_call(
        functools.partial(_sample_attn_kernel, t_new=t_new, conv_tiles_per_seq=prompt_seq // n_conv),
        grid=(db,),
        in_specs=[pl.BlockSpec((1, t_new, ATTN_WIDTH), blk)] + cache_specs
        + [pl.BlockSpec((1, news[0].shape[1], KV_ROWS), blk)] * 3
        + [pl.BlockSpec((n_conv, CONV_CH), row), halo_spec,
           pl.BlockSpec((CONV_WIDTH, CONV_CH), whole), pl.BlockSpec((1, CONV_CH), whole)],
        out_specs=[pl.BlockSpec((1, t_new, GROUP_W), blk)] + cache_specs + [pl.BlockSpec((n_conv, CONV_CH), row)],
        out_shape=[jax.ShapeDtypeStruct((db, t_new, GROUP_W), F32)]
        + [jax.ShapeDtypeStruct(c.shape, c.dtype) for c in caches_t]
        + [jax.ShapeDtypeStruct(u_prompt.shape, F32)],
        scratch_shapes=[pltpu.VMEM((HALO + n_conv, CONV_CH), F32), pltpu.VMEM((8, HALO + n_conv, CONV_CH), F32)],
        compiler_params=_cparams(("parallel",)),
        name="sample_attn",
    )(q3, *caches_t, *news, u_prompt, u_prompt, w_dw, b_dw)
    return outs[0], outs[1:4], outs[4]


def _sample_conv_kernel(state_ref, u_ref, w_ref, b_ref, conv_ref, sout_ref):
    hist = state_ref.shape[1]
    t_new = u_ref.shape[1]
    bb = u_ref.shape[0]
    for t in range(t_new):
        acc = jnp.broadcast_to(b_ref[...], (bb, CONV_CH))
        for j in range(CONV_WIDTH):
            r = t + j
            row = state_ref[:, r, :] if r < hist else u_ref[:, r - hist, :]
            acc = acc + row * w_ref[j:j + 1, :]
        conv_ref[:, t, :] = acc
    sout_ref[:, 0:hist - t_new, :] = state_ref[:, t_new:hist, :]
    sout_ref[:, hist - t_new:hist, :] = u_ref[...]


def _sample_conv(state, u3, w_dw, b_dw, bb=32):
    db, hist, _ = state.shape
    t_new = u3.shape[1]
    assert db % bb == 0 and hist == CONV_WIDTH - 1
    blk = lambda i: (i, 0, 0)
    whole = lambda i: (0, 0)
    return pl.pallas_call(
        _sample_conv_kernel,
        grid=(db // bb,),
        in_specs=[pl.BlockSpec((bb, hist, CONV_CH), blk), pl.BlockSpec((bb, t_new, CONV_CH), blk),
                  pl.BlockSpec((CONV_WIDTH, CONV_CH), whole), pl.BlockSpec((1, CONV_CH), whole)],
        out_specs=[pl.BlockSpec((bb, t_new, CONV_CH), blk), pl.BlockSpec((bb, hist, CONV_CH), blk)],
        out_shape=[jax.ShapeDtypeStruct((db, t_new, CONV_CH), F32), jax.ShapeDtypeStruct((db, hist, CONV_CH), F32)],
        compiler_params=_cparams(("parallel",)),
        name="sample_conv",
    )(state, u3, w_dw, b_dw)


ROUTE_LANES = 2 * LANES


def _merge_kernel(xp_ref, xs_ref, cp_ref, cs_ref, ap_ref, as_ref, gp_ref, gs_ref, clg_ref, clb_ref, wco_ref, bco_ref,
                  wao_ref, wo_ref, g1_ref, b1_ref, wr_ref, br_ref, h_ref, ri_ref, rg_ref, cnt_ref, base_scr, tile_scr,
                  *, prompt_tiles):
    i = pl.program_id(0)
    tm = xp_ref.shape[0]
    is_p = i < prompt_tiles

    @pl.when(i == 0)
    def _():
        base_scr[...] = jnp.zeros_like(base_scr)

    x = jnp.where(is_p, xp_ref[...], xs_ref[...])
    conv = jnp.where(is_p, cp_ref[...], cs_ref[...])
    attn = jnp.where(is_p, ap_ref[...], as_ref[...])
    gate = jnp.where(is_p, gp_ref[...], gs_ref[...])

    cn = _layer_norm(conv, clg_ref[...], clb_ref[...])
    branch_c = jnp.dot(jax.nn.silu(cn).astype(BF16), wco_ref[...], preferred_element_type=F32) + bco_ref[...]
    branch_a = jnp.dot(attn.astype(BF16), wao_ref[...], preferred_element_type=F32)
    mixed = gate[:, 0:D_MODEL] * branch_a + gate[:, D_MODEL:2 * D_MODEL] * branch_c
    y = jnp.dot(mixed.astype(BF16), wo_ref[...], preferred_element_type=F32)
    h = _layer_norm(DN_ALPHA * x + y, g1_ref[...], b1_ref[...])
    _store_rows_as_tiles(h_ref, h, tile_scr)

    logits = jnp.dot(h.astype(BF16), wr_ref[...], preferred_element_type=F32) + br_ref[...]
    lane = lax.broadcasted_iota(I32, (tm, LANES), 1)
    lane_f = lane.astype(F32)
    big = jnp.float32(LANES)

    def first_lane(hit):
        return jnp.min(jnp.where(hit, lane_f, big), -1, keepdims=True).astype(I32)

    gl = jnp.where(lane < N_EXPERT_GROUPS, logits[:, 0:LANES], NEG_INF)
    gmax = jnp.max(gl, -1, keepdims=True)
    g_sel = first_lane(gl == gmax)
    p_g = 1.0 / jnp.sum(jnp.exp(gl - gmax), -1, keepdims=True)
    in_group = (lane < N_EXPERTS) & (lax.shift_right_logical(lane, 3) == g_sel)
    el = jnp.where(in_group, logits[:, LANES:2 * LANES], NEG_INF)
    v1 = jnp.max(el, -1, keepdims=True)
    i1 = first_lane(el == v1)
    el2 = jnp.where(lane == i1, NEG_INF, el)
    v2 = jnp.max(el2, -1, keepdims=True)
    i2 = first_lane(el2 == v2)
    e21 = jnp.exp(v2 - v1)
    gate1 = p_g / (1.0 + e21)
    gate2 = p_g * e21 / (1.0 + e21)

    oh = ((lane == i1) | (lane == i2)).astype(BF16)
    ri_ = lax.broadcasted_iota(I32, (tm, tm), 0)
    ci_ = lax.broadcasted_iota(I32, (tm, tm), 1)
    tri = (ci_ < ri_).astype(BF16)
    before = jnp.dot(tri, oh, preferred_element_type=F32) + base_scr[...]
    rank1 = jnp.sum(jnp.where(lane == i1, before, 0.0), -1, keepdims=True).astype(I32)
    rank2 = jnp.sum(jnp.where(lane == i2, before, 0.0), -1, keepdims=True).astype(I32)
    base_scr[...] = base_scr[...] + jnp.sum(oh.astype(F32), 0, keepdims=True)
    cnt_ref[...] = base_scr[...]

    ri_ref[...] = jnp.where(lane == 0, i1, jnp.where(lane == 1, i2, jnp.where(lane == 2, rank1,
                            jnp.where(lane == 3, rank2, 0))))
    rg_ref[...] = jnp.where(lane == 0, gate1, jnp.where(lane == 1, gate2, 0.0))


def _merge(acts_p, acts_s, weights, tm):
    n_p, n_s = acts_p[0].shape[0], acts_s[0].shape[0]
    assert n_p % tm == 0 and n_s % tm == 0
    pt = n_p // tm
    n_all = n_p + n_s
    prow = lambda i: (jnp.minimum(i, pt - 1), 0)
    srow = lambda i: (jnp.maximum(i - pt, 0), 0)
    row = lambda i: (i, 0)
    whole = lambda i: (0, 0)
    act_args, act_specs = [], []
    for a_p, a_s in zip(acts_p, acts_s):
        act_args += [a_p, a_s]
        act_specs += [pl.BlockSpec((tm, a_p.shape[1]), prow), pl.BlockSpec((tm, a_s.shape[1]), srow)]
    return pl.pallas_call(
        functools.partial(_merge_kernel, prompt_tiles=pt),
        grid=(n_all // tm,),
        in_specs=act_specs + [pl.BlockSpec(w.shape, whole) for w in weights],
        out_specs=[pl.BlockSpec((tm, ROW_SUB, LANES), lambda i: (i, 0, 0)), pl.BlockSpec((tm, LANES), row),
                   pl.BlockSpec((tm, LANES), row), pl.BlockSpec((1, LANES), whole)],
        out_shape=[jax.ShapeDtypeStruct((n_all, ROW_SUB, LANES), F32), jax.ShapeDtypeStruct((n_all, LANES), I32),
                   jax.ShapeDtypeStruct((n_all, LANES), F32), jax.ShapeDtypeStruct((1, LANES), F32)],
        scratch_shapes=[pltpu.VMEM((1, LANES), F32), pltpu.VMEM((tm // 8, 8 * ROW_SUB, LANES), F32)],
        compiler_params=_cparams(("arbitrary",)),
        name="merge",
    )(*act_args, *weights)


def _dispatch_kernel(bounds_ref, nv_ref, dest_ref, h_ref, xg_any, hbuf, zero_scr, sem_z, sem_r):
    i = pl.program_id(0)
    tm = h_ref.shape[0]
    n_chunks = xg_any.shape[0] // CHUNK
    slot = i % 2

    def zero_chunk(row0):
        return pltpu.make_async_copy(zero_scr, xg_any.at[pl.ds(pl.multiple_of(row0, CHUNK), CHUNK)], sem_z)

    @pl.when(i == 0)
    def _():
        zero_scr[...] = jnp.zeros_like(zero_scr)
        for e in range(N_EXPERTS):
            @pl.when(bounds_ref[e + 1] > bounds_ref[e])
            def _():
                zero_chunk(bounds_ref[e + 1] - CHUNK).start()

        def start_tail(c, carry):
            zero_chunk(c * CHUNK).start()
            return carry

        lax.fori_loop(nv_ref[0], n_chunks, start_tail, 0)
        for e in range(N_EXPERTS):
            @pl.when(bounds_ref[e + 1] > bounds_ref[e])
            def _():
                zero_chunk(0).wait()

        def wait_tail(c, carry):
            zero_chunk(0).wait()
            return carry

        lax.fori_loop(nv_ref[0], n_chunks, wait_tail, 0)

    def wait_rows(sl):
        for _ in range(2 * tm):
            pltpu.make_async_copy(hbuf.at[0, 0], xg_any.at[0], sem_r.at[sl]).wait()

    @pl.when(i >= 2)
    def _():
        wait_rows(slot)

    hbuf[slot] = h_ref[...]
    for r in range(2 * tm):
        pltpu.make_async_copy(hbuf.at[slot, r % tm], xg_any.at[dest_ref[0, 0, r]],
                              sem_r.at[slot]).start(priority=r % 2)

    @pl.when(i == pl.num_programs(0) - 1)
    def _():
        wait_rows(slot)

        @pl.when(i >= 1)
        def _():
            wait_rows(1 - slot)


def _dispatch(bounds, n_valid, dest3, h3, n_rows, tm):
    n = h3.shape[0]
    return pl.pallas_call(
        _dispatch_kernel,
        grid_spec=pltpu.PrefetchScalarGridSpec(
            num_scalar_prefetch=2,
            grid=(n // tm,),
            in_specs=[pl.BlockSpec((1, 1, 2 * tm), lambda i, b, nv: (i, 0, 0), memory_space=pltpu.SMEM),
                      pl.BlockSpec((tm, ROW_SUB, LANES), lambda i, b, nv: (i, 0, 0))],
            out_specs=pl.BlockSpec(memory_space=pl.ANY),
            scratch_shapes=[pltpu.VMEM((2, tm, ROW_SUB, LANES), F32), pltpu.VMEM((CHUNK, ROW_SUB, LANES), F32),
                            pltpu.SemaphoreType.DMA(()), pltpu.SemaphoreType.DMA((2,))],
        ),
        out_shape=jax.ShapeDtypeStruct((n_rows, ROW_SUB, LANES), F32),
        compiler_params=_cparams(("arbitrary",)),
        name="moe_dispatch",
    )(bounds, n_valid, dest3, h3)


def _experts_kernel(ce_ref, nv_ref, nxt_ref, par_ref, x_ref, wg_any, wu_any, wd_any, y_ref,
                    wg_buf, wu_buf, wd_buf, wg_scr, wu_scr, wd_scr, tile_scr, wsem):
    c = pl.program_id(0)
    nv = nv_ref[0]
    e = ce_ref[c]
    slot = par_ref[c]

    def weight_copies(expert, sl):
        return [pltpu.make_async_copy(src.at[expert], dst.at[sl], wsem.at[sl, k])
                for k, (src, dst) in enumerate(((wg_any, wg_buf), (wu_any, wu_buf), (wd_any, wd_buf)))]

    @pl.when((c == 0) & (nv > 0))
    def _():
        for cp in weight_copies(e, slot):
            cp.start()

    @pl.when((c < nv) & ((c == 0) | (e != ce_ref[jnp.maximum(c - 1, 0)])))
    def _():
        for cp in weight_copies(e, slot):
            cp.wait()
        wg_scr[...] = wg_buf[slot].astype(BF16)
        wu_scr[...] = wu_buf[slot].astype(BF16)
        wd_scr[...] = wd_buf[slot].astype(BF16)

        @pl.when(nxt_ref[c] != e)
        def _():
            for cp in weight_copies(nxt_ref[c], 1 - slot):
                cp.start()

    @pl.when(c < nv)
    def _():
        x = _load_tiles_as_rows(x_ref, tile_scr).astype(BF16)
        a = jnp.dot(x, wg_scr[...], preferred_element_type=F32)
        b = jnp.dot(x, wu_scr[...], preferred_element_type=F32)
        y = jnp.dot((jax.nn.silu(a) * b).astype(BF16), wd_scr[...], preferred_element_type=F32)
        _store_rows_as_tiles(y_ref, y, tile_scr)

    @pl.when(c >= nv)
    def _():
        y_ref[...] = jnp.zeros_like(y_ref)


def _experts(chunk_expert, n_valid, next_expert, parity, xg, w_eg, w_eu, w_ed):
    n_chunks = xg.shape[0] // CHUNK
    any_spec = pl.BlockSpec(memory_space=pl.ANY)
    return pl.pallas_call(
        _experts_kernel,
        grid_spec=pltpu.PrefetchScalarGridSpec(
            num_scalar_prefetch=4,
            grid=(n_chunks,),
            in_specs=[pl.BlockSpec((CHUNK, ROW_SUB, LANES),
                                   lambda c, ce, nv, nx, pr: (jnp.minimum(c, jnp.maximum(nv[0] - 1, 0)), 0, 0)),
                      any_spec, any_spec, any_spec],
            out_specs=pl.BlockSpec((CHUNK, ROW_SUB, LANES), lambda c, ce, nv, nx, pr: (c, 0, 0)),
            scratch_shapes=[pltpu.VMEM((2, D_MODEL, D_EXPERT), F32), pltpu.VMEM((2, D_MODEL, D_EXPERT), F32),
                            pltpu.VMEM((2, D_EXPERT, D_MODEL), F32),
                            pltpu.VMEM((D_MODEL, D_EXPERT), BF16), pltpu.VMEM((D_MODEL, D_EXPERT), BF16),
                            pltpu.VMEM((D_EXPERT, D_MODEL), BF16),
                            pltpu.VMEM((CHUNK // 8, 8 * ROW_SUB, LANES), F32),
                            pltpu.SemaphoreType.DMA((2, 3))],
        ),
        out_shape=jax.ShapeDtypeStruct(xg.shape, F32),
        compiler_params=_cparams(("arbitrary",)),
        name="moe_experts",
    )(chunk_expert, n_valid, next_expert, parity, xg, w_eg, w_eu, w_ed)


def _combine_kernel(dest_ref, dest_next_ref, h_ref, rg_ref, g2_ref, b2_ref, yg_any, yp_ref, ys_ref,
                    ybuf, t0_scr, t1_scr, t2_scr, sem, *, prompt_tiles):
    i = pl.program_id(0)
    tm = h_ref.shape[0]
    slot = i % 2

    def start_gather(src_ref, sl):
        for r in range(2 * tm):
            pltpu.make_async_copy(yg_any.at[src_ref[0, 0, r]], ybuf.at[sl, r], sem.at[sl]).start(priority=r % 2)

    @pl.when(i == 0)
    def _():
        start_gather(dest_ref, 0)

    @pl.when(i + 1 < pl.num_programs(0))
    def _():
        start_gather(dest_next_ref, 1 - slot)

    for _ in range(2 * tm):
        pltpu.make_async_copy(yg_any.at[0], ybuf.at[0, 0], sem.at[slot]).wait()
    moe = (_load_tiles_as_rows(ybuf.at[slot, pl.ds(0, tm)], t1_scr) * rg_ref[:, 0:1]
           + _load_tiles_as_rows(ybuf.at[slot, pl.ds(tm, tm)], t2_scr) * rg_ref[:, 1:2])
    out = _layer_norm(DN_ALPHA * _load_tiles_as_rows(h_ref, t0_scr) + moe, g2_ref[...], b2_ref[...])

    @pl.when(i < prompt_tiles)
    def _():
        yp_ref[...] = out

    @pl.when(i >= prompt_tiles)
    def _():
        ys_ref[...] = out


def _combine(dest3, h3, yg, route_g, ln2_g, ln2_b, n_prompt, tm):
    n = h3.shape[0]
    pt, nt = n_prompt // tm, n // tm
    row = lambda i: (i, 0)
    whole = lambda i: (0, 0)
    return pl.pallas_call(
        functools.partial(_combine_kernel, prompt_tiles=pt),
        grid=(nt,),
        in_specs=[pl.BlockSpec((1, 1, 2 * tm), lambda i: (i, 0, 0), memory_space=pltpu.SMEM),
                  pl.BlockSpec((1, 1, 2 * tm), lambda i: (jnp.minimum(i + 1, nt - 1), 0, 0), memory_space=pltpu.SMEM),
                  pl.BlockSpec((tm, ROW_SUB, LANES), lambda i: (i, 0, 0)), pl.BlockSpec((tm, LANES), row),
                  pl.BlockSpec((1, D_MODEL), whole), pl.BlockSpec((1, D_MODEL), whole),
                  pl.BlockSpec(memory_space=pl.ANY)],
        out_specs=[pl.BlockSpec((tm, D_MODEL), lambda i: (jnp.minimum(i, pt - 1), 0)),
                   pl.BlockSpec((tm, D_MODEL), lambda i: (jnp.maximum(i - pt, 0), 0))],
        out_shape=[jax.ShapeDtypeStruct((n_prompt, D_MODEL), F32), jax.ShapeDtypeStruct((n - n_prompt, D_MODEL), F32)],
        scratch_shapes=[pltpu.VMEM((2, 2 * tm, ROW_SUB, LANES), F32)]
        + [pltpu.VMEM((tm // 8, 8 * ROW_SUB, LANES), F32) for _ in range(3)] + [pltpu.SemaphoreType.DMA((2,))],
        compiler_params=_cparams(("arbitrary",)),
        name="moe_combine",
    )(dest3, dest3, h3, route_g, ln2_g, ln2_b, yg)


def kernel(x_prompt, x_sample, cache_kv_w128, cache_kv_w512, cache_kv_w2048, state_conv, w_in, b_in, w_dw, b_dw, conv_ln_g, conv_ln_b, w_conv_out, b_conv_out, w_attn_out, w_o, ln1_g, ln1_b, w_router_group, b_router_group, w_router_expert, b_router_expert, w_expert_gate, w_expert_up, w_expert_down, ln2_g, ln2_b):
    assert w_in.shape[0] == DEPTH == 1
    bsz, seq, _ = x_prompt.shape
    db, t_new, _ = x_sample.shape
    caches = (cache_kv_w128, cache_kv_w512, cache_kv_w2048)
    past = cache_kv_w2048.shape[2]
    n_p, n_s = bsz * seq, db * t_new
    n_all = n_p + n_s
    tm = ROW_TILE
    kvs = (HEADS_PER_GROUP, HEAD_DIM)

    w_bf = w_in[0].astype(BF16)
    xp2, xs2 = x_prompt.reshape(n_p, D_MODEL), x_sample.reshape(n_s, D_MODEL)
    q0, q1, q2, *kt_p, u_p, gate_p = _in_proj(xp2, w_bf, b_in, _rotary_tables(np.arange(seq)), tm, prompt_seq=seq)
    pos_s = past + np.arange(tm) % t_new
    q_s, kvf_s, u_s, gate_s = _in_proj(xs2, w_bf, b_in, _rotary_tables(pos_s), tm)

    attn_p = _prompt_attn((q0, q1, q2), bsz, seq)

    kv_new = kvf_s.reshape(db, t_new, N_GROUPS, KV_ROWS)
    caches_t = [jnp.transpose(c[0], (0, 2, 3, 4, 1)).reshape(db, KV_ROWS, c.shape[2]) for c in caches]
    news = [jnp.pad(kv_new[:, :, g], ((0, 0), (0, -t_new % 8), (0, 0))) for g in range(N_GROUPS)]
    attn_s, kv_sample_t, conv_p = _sample_attn(q_s.reshape(db, t_new, ATTN_WIDTH), caches_t, news,
                                               u_p, seq, w_dw[0], b_dw)
    kv_sample = [jnp.transpose(o.reshape(db, 2, *kvs, o.shape[2]), (0, 4, 1, 2, 3))[None] for o in kv_sample_t]

    conv_s, conv_state_s = _sample_conv(state_conv[0], u_s.reshape(db, t_new, CONV_CH), w_dw[0], b_dw)

    w_r = jnp.zeros((D_MODEL, ROUTE_LANES), F32)
    w_r = w_r.at[:, 0:N_EXPERT_GROUPS].set(w_router_group[0]).at[:, LANES:LANES + N_EXPERTS].set(w_router_expert[0])
    b_r = jnp.zeros((1, ROUTE_LANES), F32)
    b_r = b_r.at[:, 0:N_EXPERT_GROUPS].set(b_router_group).at[:, LANES:LANES + N_EXPERTS].set(b_router_expert)
    weights = [conv_ln_g, conv_ln_b, w_conv_out[0].astype(BF16), b_conv_out, w_attn_out[0].astype(BF16),
               w_o[0].astype(BF16), ln1_g, ln1_b, w_r.astype(BF16), b_r]
    acts_p = (xp2, conv_p, attn_p, gate_p)
    acts_s = (xs2, conv_s.reshape(n_s, CONV_CH), attn_s.reshape(n_s, GROUP_W), gate_s)
    h3, route_i, route_g, counts = _merge(acts_p, acts_s, weights, tm)

    cnt = counts[0, :N_EXPERTS].astype(I32)
    padded = ((cnt + CHUNK - 1) // CHUNK) * CHUNK
    pad_end = jnp.cumsum(padded)
    bounds = jnp.concatenate([jnp.zeros((1,), I32), pad_end]).astype(I32)
    n_chunks = (2 * n_all) // CHUNK + N_EXPERTS
    chunk_row0 = jnp.arange(n_chunks, dtype=I32) * CHUNK
    chunk_expert = jnp.minimum(jnp.sum((pad_end[None, :] <= chunk_row0[:, None]).astype(I32), axis=1),
                               N_EXPERTS - 1).astype(I32)
    n_valid = (pad_end[-1:] // CHUNK).astype(I32)

    e_onehot = route_i[:, 0:2, None] == jnp.arange(N_EXPERTS, dtype=I32)[None, None, :]
    dest = jnp.sum(jnp.where(e_onehot, bounds[None, None, :N_EXPERTS], 0), axis=-1) + route_i[:, 2:4]
    dest3 = jnp.transpose(dest.reshape(n_all // tm, tm, 2), (0, 2, 1)).reshape(n_all // tm, 1, 2 * tm)
    xg = _dispatch(bounds, n_valid, dest3, h3, n_chunks * CHUNK, tm)
    eidx = jnp.arange(N_EXPERTS, dtype=I32)
    later = (eidx[None, :] > eidx[:, None]) & (padded[None, :] > 0)
    next_nonempty = jnp.min(jnp.where(later, eidx[None, :], N_EXPERTS), axis=1)
    next_nonempty = jnp.where(next_nonempty < N_EXPERTS, next_nonempty, eidx).astype(I32)
    ce_onehot = chunk_expert[:, None] == eidx[None, :]
    next_expert = jnp.sum(jnp.where(ce_onehot, next_nonempty[None, :], 0), axis=1).astype(I32)
    changed = jnp.concatenate([jnp.zeros((1,), I32), (chunk_expert[1:] != chunk_expert[:-1]).astype(I32)])
    parity = (jnp.cumsum(changed) % 2).astype(I32)
    yg = _experts(chunk_expert, n_valid, next_expert, parity, xg,
                  w_expert_gate[0], w_expert_up[0], w_expert_down[0])
    y_p, y_s = _combine(dest3, h3, yg, route_g, ln2_g, ln2_b, n_p, tm)

    kv_prompt = []
    for kt, (w, _) in zip(kt_p, ATTN_PATTERNS):
        win_t = kt[:, :, seq - min(w, seq):]
        kv_prompt.append(jnp.transpose(win_t.reshape(bsz, 2, *kvs, win_t.shape[2]), (0, 4, 1, 2, 3))[None])
    conv_prompt = u_p.reshape(1, bsz, seq, CONV_CH)[:, :, seq - (CONV_WIDTH - 1):]
    return (y_p.reshape(bsz, seq, D_MODEL), y_s.reshape(db, t_new, D_MODEL),
            kv_prompt[0], kv_prompt[1], kv_prompt[2], conv_prompt,
            kv_sample[0], kv_sample[1], kv_sample[2], conv_state_s[None])
```
